```python
import math
import jax, jax.numpy as jnp
from jax import lax
import numpy as np

D_MODEL = 1024
BATCH = 8
SEQ = 4096
DEPTH = 2

GRID_W = 64
CTX_LEN = 256
N_MOD = 6
EPS = 1e-6

D_CONV = D_MODEL // 2
CONV_WIDTH = 31
D_POOL = D_MODEL // 2
POOL_WINDOWS = (2, 4, 8, 16)
POOL_GROUP = D_POOL // len(POOL_WINDOWS)
EVEN_IN = 2 * D_CONV + D_POOL
EVEN_MIX = D_CONV + D_POOL

MLA_HEADS = 4
Q_LORA = D_MODEL // 4
KV_LORA = D_MODEL // 8
QK_NOPE = 128
QK_ROPE = 64
V_HEAD = 128
DIFF_HEADS = 4
DIFF_QK = 64
DIFF_V = 128
DIFF_QK_COLS = DIFF_HEADS * 2 * DIFF_QK
Q_COLS = Q_LORA + DIFF_QK_COLS
KV_COLS = KV_LORA + QK_ROPE + DIFF_QK_COLS + DIFF_HEADS * DIFF_V
ODD_IN = Q_COLS + KV_COLS
ODD_MIX = MLA_HEADS * V_HEAD + DIFF_HEADS * DIFF_V
MLA_SCALE = (QK_NOPE + QK_ROPE) ** -0.5
DIFF_SCALE = DIFF_QK ** -0.5
ROPE_BASE = 10000.0
Q_BLOCK = 128

N_EXPERTS = 32
N_GROUPS = 8
EXPERTS_PER_GROUP = N_EXPERTS // N_GROUPS
TOP_K = 2
D_EXPERT = 256

N_EVEN = (DEPTH + 1) // 2
N_ODD = DEPTH // 2

kernel_name = "hybrid_conv_pool_mla_diffattn_moe_dit"


def rms_norm(x, g):
    xf = x.astype(jnp.float32)
    y = xf * lax.rsqrt(jnp.mean(xf * xf, axis=-1, keepdims=True) + EPS)
    return (y * g).astype(x.dtype)


def layer_norm(x, g, b):
    xf = x.astype(jnp.float32)
    mu = jnp.mean(xf, axis=-1, keepdims=True)
    xc = xf - mu
    y = xc * lax.rsqrt(jnp.mean(xc * xc, axis=-1, keepdims=True) + EPS)
    return (y * g + b).astype(x.dtype)


def modulate(h, shift, scale):
    return h * (1.0 + scale) + shift


def axial_rope_tables(rows, rot_dim):
    row = jnp.repeat(jnp.arange(rows, dtype=jnp.float32), GRID_W)
    col = jnp.tile(jnp.arange(GRID_W, dtype=jnp.float32), rows)
    axis_dim = rot_dim // 2
    inv_freq = ROPE_BASE ** (-jnp.arange(0, axis_dim, 2, dtype=jnp.float32) / axis_dim)
    ang = jnp.concatenate([row[:, None] * inv_freq, col[:, None] * inv_freq], axis=-1)
    return jnp.cos(ang), jnp.sin(ang)


def apply_rope(x, cos, sin):
    half = x.shape[-1] // 2
    shape = (1, x.shape[1]) + (1,) * (x.ndim - 3) + (half,)
    cs = cos.reshape(shape).astype(x.dtype)
    sn = sin.reshape(shape).astype(x.dtype)
    x1, x2 = x[..., :half], x[..., half:]
    return jnp.concatenate([x1 * cs - x2 * sn, x2 * cs + x1 * sn], axis=-1)


def attend(q, k, v, scale):
    b, lq, h, dk = q.shape
    nb = lq // Q_BLOCK
    qb = jnp.moveaxis(q.reshape(b, nb, Q_BLOCK, h, dk), 1, 0)

    def one_block(qi):
        s = jnp.einsum('bqhd,bkhd->bhqk', qi, k).astype(jnp.float32) * scale
        p = jax.nn.softmax(s, axis=-1).astype(v.dtype)
        return jnp.einsum('bhqk,bkhd->bqhd', p, v)

    out = lax.map(one_block, qb)
    return jnp.moveaxis(out, 0, 1).reshape(b, lq, h, v.shape[-1])


def conformer_conv(u, dw_w, dw_b, ln_g, ln_b):
    a, g = jnp.split(u, 2, axis=-1)
    y = a * jax.nn.sigmoid(g)
    y = lax.conv_general_dilated(
        y, dw_w[:, None, :], window_strides=(1,),
        padding=[(CONV_WIDTH // 2, CONV_WIDTH // 2)],
        dimension_numbers=('NWC', 'WIO', 'NWC'),
        feature_group_count=D_CONV) + dw_b
    return jax.nn.silu(layer_norm(y, ln_g, ln_b))


def centred_mean_minus_self(x, w):
    n = x.shape[1]
    xf = x.astype(jnp.float32)
    cs = jnp.concatenate([jnp.zeros_like(xf[:, :1]), jnp.cumsum(xf, axis=1)], axis=1)
    t = jnp.arange(n)
    lo = jnp.clip(t - w // 2, 0, n)
    hi = jnp.clip(t + w // 2, 0, n)
    cnt = (hi - lo).astype(jnp.float32)[None, :, None]
    return ((cs[:, hi] - cs[:, lo]) / cnt - xf).astype(x.dtype)


def pool_mix(u, pool_w, pool_scale):
    groups = [centred_mean_minus_self(u[..., g * POOL_GROUP:(g + 1) * POOL_GROUP], w)
              for g, w in enumerate(POOL_WINDOWS)]
    y = jnp.stack(groups, axis=2)
    y = jnp.einsum('blgc,gcd->blgd', y, pool_w).reshape(u.shape)
    return y * pool_scale


def even_mixer(h, w_in, dw_w, dw_b, ln_g, ln_b, pool_w, pool_scale, w_out):
    u = h @ w_in
    y_conv = conformer_conv(u[..., :2 * D_CONV], dw_w, dw_b, ln_g, ln_b)
    y_pool = pool_mix(u[..., 2 * D_CONV:], pool_w, pool_scale)
    return jnp.concatenate([y_conv, y_pool], axis=-1) @ w_out


def rope_tail(x, rope):
    return jnp.concatenate([x[..., :QK_NOPE], apply_rope(x[..., QK_NOPE:], *rope)], axis=-1)


def odd_queries(p_q, g_cq, w_uq, g_q_mla, g_q_diff, rope):
    b, l, _ = p_q.shape
    c_q = rms_norm(p_q[..., :Q_LORA], g_cq)
    q_m = rms_norm((c_q @ w_uq).reshape(b, l, MLA_HEADS, QK_NOPE + QK_ROPE), g_q_mla)
    q_d = rms_norm(p_q[..., Q_LORA:].reshape(b, l, DIFF_HEADS, 2, DIFF_QK), g_q_diff)
    if rope is not None:
        q_m = rope_tail(q_m, rope)
        q_d = apply_rope(q_d, *rope)
    return (q_m, q_d[..., 0, :], q_d[..., 1, :])


def odd_keys_values(p_kv, g_ckv, w_ukv, g_k_mla, g_k_diff, rope):
    b, l, _ = p_kv.shape
    c_kv = rms_norm(p_kv[..., :KV_LORA], g_ckv)
    k_rope = p_kv[..., KV_LORA:KV_LORA + QK_ROPE]
    off = KV_LORA + QK_ROPE
    k_d = p_kv[..., off:off + DIFF_QK_COLS].reshape(b, l, DIFF_HEADS, 2, DIFF_QK)
    v_d = p_kv[..., off + DIFF_QK_COLS:].reshape(b, l, DIFF_HEADS, DIFF_V)
    kv = (c_kv @ w_ukv).reshape(b, l, MLA_HEADS, QK_NOPE + V_HEAD)
    k_nope, v_m = kv[..., :QK_NOPE], kv[..., QK_NOPE:]
    k_m = jnp.concatenate(
        [k_nope, jnp.broadcast_to(k_rope[:, :, None, :], (b, l, MLA_HEADS, QK_ROPE))], axis=-1)
    k_m = rms_norm(k_m, g_k_mla)
    k_d = rms_norm(k_d, g_k_diff)
    if rope is not None:
        k_m = rope_tail(k_m, rope)
        k_d = apply_rope(k_d, *rope)
    return (k_m, v_m, k_d[..., 0, :], k_d[..., 1, :], v_d)


def odd_output(qs, kvs, lam, lam_init, g_sub, w_out):
    q_m, q1, q2 = qs
    k_m, v_m, k1, k2, v_d = kvs
    b, l = q_m.shape[:2]
    o_m = attend(q_m, k_m, v_m, MLA_SCALE)
    a1 = attend(q1, k1, v_d, DIFF_SCALE)
    a2 = attend(q2, k2, v_d, DIFF_SCALE)
    o_d = rms_norm(a1 - lam.astype(a1.dtype) * a2, g_sub) * (1.0 - lam_init)
    return jnp.concatenate([o_m.reshape(b, l, -1), o_d.reshape(b, l, -1)], axis=-1) @ w_out


def moe(h, router_w, router_b, w_gate, w_up, w_down):
    scores = jax.nn.sigmoid(jnp.einsum('bld,de->ble', h, router_w).astype(jnp.float32))
    biased = scores + router_b.astype(jnp.float32)
    grp = biased.reshape(biased.shape[:-1] + (N_GROUPS, EXPERTS_PER_GROUP))
    grp_score = lax.top_k(grp, TOP_K)[0].sum(-1)
    sel = jnp.argmax(grp_score, axis=-1)
    in_group = (jnp.arange(N_EXPERTS) // EXPERTS_PER_GROUP) == sel[..., None]
    _, idx = lax.top_k(jnp.where(in_group, biased, -jnp.inf), TOP_K)
    w = jnp.take_along_axis(scores, idx, axis=-1)
    w = w / jnp.sum(w, axis=-1, keepdims=True)
    combine = jnp.einsum('blk,blke->ble', w, jax.nn.one_hot(idx, N_EXPERTS, dtype=jnp.float32)).astype(h.dtype)
    y = jnp.zeros_like(h)
    for e in range(N_EXPERTS):
        he = jax.nn.silu(h @ w_gate[e]) * (h @ w_up[e])
        y = y + combine[..., e:e + 1] * (he @ w_down[e])
    return y


def setup_inputs(seed: int = 0) -> dict:
    key = jax.random.key(seed)
    keys = jax.random.split(key, 30)
    D = D_MODEL

    def nrm(n, shape, s):
        return s * jax.random.normal(keys[n], shape, jnp.float32)

    return {
        "x": nrm(0, (BATCH, SEQ, D), 1.0),
        "c": nrm(1, (BATCH, D), 1.0),
        "ctx": nrm(2, (BATCH, CTX_LEN, D), 1.0),
        "c_ctx": nrm(3, (D,), 1.0),
        "mod_w": nrm(4, (DEPTH, D, N_MOD * D), 0.5 * D ** -0.5),
        "mod_b": nrm(5, (DEPTH, N_MOD * D), 0.02),
        "norm_g": 1.0 + nrm(6, (DEPTH, 2, D), 0.1),
        "even_w_in": nrm(7, (N_EVEN, D, EVEN_IN), D ** -0.5),
        "conv_dw_w": nrm(8, (N_EVEN, CONV_WIDTH, D_CONV), CONV_WIDTH ** -0.5),
        "conv_dw_b": nrm(9, (N_EVEN, D_CONV), 0.02),
        "conv_ln_g": 1.0 + nrm(10, (N_EVEN, D_CONV), 0.1),
        "conv_ln_b": nrm(11, (N_EVEN, D_CONV), 0.02),
        "pool_w": nrm(12, (N_EVEN, len(POOL_WINDOWS), POOL_GROUP, POOL_GROUP), POOL_GROUP ** -0.5),
        "pool_scale": 1.0 + nrm(13, (N_EVEN, D_POOL), 0.1),
        "even_w_out": nrm(14, (N_EVEN, EVEN_MIX, D), EVEN_MIX ** -0.5),
        "odd_w_in": nrm(15, (N_ODD, D, ODD_IN), D ** -0.5),
        "mla_g_cq": 1.0 + nrm(16, (N_ODD, Q_LORA), 0.1),
        "mla_w_uq": nrm(17, (N_ODD, Q_LORA, MLA_HEADS * (QK_NOPE + QK_ROPE)), Q_LORA ** -0.5),
        "mla_g_ckv": 1.0 + nrm(18, (N_ODD, KV_LORA), 0.1),
        "mla_w_ukv": nrm(19, (N_ODD, KV_LORA, MLA_HEADS * (QK_NOPE + V_HEAD)), KV_LORA ** -0.5),
        "qk_g_mla": 1.0 + nrm(20, (N_ODD, 2, QK_NOPE + QK_ROPE), 0.1),
        "qk_g_diff": 1.0 + nrm(21, (N_ODD, 2, DIFF_QK), 0.1),
        "diff_lambda": nrm(22, (N_ODD, 4, DIFF_QK), 0.1),
        "diff_sub_g": 1.0 + nrm(23, (N_ODD, DIFF_V), 0.1),
        "odd_w_out": nrm(24, (N_ODD, ODD_MIX, D), ODD_MIX ** -0.5),
        "router_w": nrm(25, (D, N_EXPERTS), D ** -0.5),
        "router_b": nrm(26, (N_EXPERTS,), 0.01),
        "moe_w_gate": nrm(27, (DEPTH, N_EXPERTS, D, D_EXPERT), D ** -0.5),
        "moe_w_up": nrm(28, (DEPTH, N_EXPERTS, D, D_EXPERT), D ** -0.5),
        "moe_w_down": nrm(29, (DEPTH, N_EXPERTS, D_EXPERT, D), D_EXPERT ** -0.5),
    }


def reference(x, c, ctx, c_ctx, mod_w, mod_b, norm_g, even_w_in, conv_dw_w, conv_dw_b, conv_ln_g,
              conv_ln_b, pool_w, pool_scale, even_w_out, odd_w_in, mla_g_cq, mla_w_uq, mla_g_ckv,
              mla_w_ukv, qk_g_mla, qk_g_diff, diff_lambda, diff_sub_g, odd_w_out, router_w, router_b,
              moe_w_gate, moe_w_up, moe_w_down):
    b, n_lat, d = x.shape
    n_ctx = ctx.shape[1]
    rows = n_lat // GRID_W
    rope = axial_rope_tables(rows, QK_ROPE)
    c_act = jax.nn.silu(c)
    c_ctx_act = jax.nn.silu(c_ctx)
    xl, xc = x, ctx
    for i in range(DEPTH):
        ctx_out = i < DEPTH - 1
        j = i // 2
        mod_l = (c_act @ mod_w[i] + mod_b[i]).reshape(b, N_MOD, 1, d)
        mod_c = (c_ctx_act @ mod_w[i] + mod_b[i]).reshape(N_MOD, d)

        hl = modulate(rms_norm(xl, norm_g[i, 0]), mod_l[:, 0], mod_l[:, 1])
        if i % 2 == 0:
            ew = (even_w_in[j], conv_dw_w[j], conv_dw_b[j], conv_ln_g[j], conv_ln_b[j],
                  pool_w[j], pool_scale[j], even_w_out[j])
            yl = even_mixer(hl, *ew)
            if ctx_out:
                hc = modulate(rms_norm(xc, norm_g[i, 0]), mod_c[0], mod_c[1])
                yc = even_mixer(hc, *ew)
        else:
            hc = modulate(rms_norm(xc, norm_g[i, 0]), mod_c[0], mod_c[1])
            lam_init = 0.8 - 0.6 * math.exp(-0.3 * i)
            dl = diff_lambda[j].astype(jnp.float32)
            lam = jnp.exp(jnp.sum(dl[0] * dl[1])) - jnp.exp(jnp.sum(dl[2] * dl[3])) + lam_init
            qg = (mla_g_cq[j], mla_w_uq[j], qk_g_mla[j, 0], qk_g_diff[j, 0])
            kg = (mla_g_ckv[j], mla_w_ukv[j], qk_g_mla[j, 1], qk_g_diff[j, 1])
            og = (lam, lam_init, diff_sub_g[j], odd_w_out[j])
            w_in = odd_w_in[j]
            kv_c = odd_keys_values(hc @ w_in[:, Q_COLS:], *kg, None)
            p_l = hl @ w_in
            kv_l = odd_keys_values(p_l[..., Q_COLS:], *kg, rope)
            kv_all = tuple(jnp.concatenate([kc, kl], axis=1) for kc, kl in zip(kv_c, kv_l))
            yl = odd_output(odd_queries(p_l[..., :Q_COLS], *qg, rope), kv_all, *og)
            if ctx_out:
                yc = odd_output(odd_queries(hc @ w_in[:, :Q_COLS], *qg, None), kv_c, *og)
        xl = xl + mod_l[:, 2] * yl
        if ctx_out:
            xc = xc + mod_c[2] * yc

        h2l = modulate(rms_norm(xl, norm_g[i, 1]), mod_l[:, 3], mod_l[:, 4])
        if ctx_out:
            h2c = modulate(rms_norm(xc, norm_g[i, 1]), mod_c[3], mod_c[4])
            y2 = moe(jnp.concatenate([h2c, h2l], axis=1), router_w, router_b,
                     moe_w_gate[i], moe_w_up[i], moe_w_down[i])
            xc = xc + mod_c[5] * y2[:, :n_ctx]
            xl = xl + mod_l[:, 5] * y2[:, n_ctx:]
        else:
            xl = xl + mod_l[:, 5] * moe(h2l, router_w, router_b, moe_w_gate[i], moe_w_up[i], moe_w_down[i])
    return xl
```

```python
import functools
import math

import jax
import jax.numpy as jnp
from jax import lax
from jax.experimental import pallas as pl
from jax.experimental.pallas import tpu as pltpu

F32 = jnp.float32
BF16 = jnp.bfloat16
I32 = jnp.int32

GRID_W = 64
N_MOD = 6
EPS = 1e-6
CONV_WIDTH = 31
POOL_WINDOWS = (2, 4, 8, 16)
MLA_HEADS = 4
QK_NOPE = 128
QK_ROPE = 64
V_HEAD = 128
DIFF_HEADS = 4
DIFF_QK = 64
DIFF_V = 128
MLA_SCALE = (QK_NOPE + QK_ROPE) ** -0.5
DIFF_SCALE = DIFF_QK ** -0.5
ROPE_BASE = 10000.0
N_EXPERTS = 32
N_GROUPS = 8
EXPERTS_PER_GROUP = N_EXPERTS // N_GROUPS

LANES = 128
TS = 256
HALO = 16
TMG = 256
TQ = 512
TK = 256
VMEM_LIMIT = 48 * 1024 * 1024


def _cparams(n_axes):
    return pltpu.CompilerParams(
        dimension_semantics=("arbitrary",) * n_axes, vmem_limit_bytes=VMEM_LIMIT)


def _norm_mod(x, g, shift, scale):
    ms = jnp.mean(x * x, axis=-1, keepdims=True)
    return (x * lax.rsqrt(ms + EPS) * g) * (1.0 + scale) + shift


def _silu(x):
    return x * jax.nn.sigmoid(x)


def _mod_kernel(c_ref, w_ref, b_ref, o_ref):
    a = _silu(c_ref[...])
    o_ref[0] = jnp.dot(a.astype(BF16), w_ref[0].astype(BF16),
                       preferred_element_type=F32) + b_ref[0]


def _modulation(c_rows, mod_w, mod_b):
    depth, d, n = mod_w.shape
    tn = 1536
    rows = c_rows.shape[0]
    return pl.pallas_call(
        _mod_kernel,
        grid=(depth, n // tn),
        in_specs=[
            pl.BlockSpec((rows, d), lambda l, j: (0, 0)),
            pl.BlockSpec((1, d, tn), lambda l, j: (l, 0, j)),
            pl.BlockSpec((1, 1, tn), lambda l, j: (l, 0, j)),
        ],
        out_specs=pl.BlockSpec((1, rows, tn), lambda l, j: (l, 0, j)),
        out_shape=jax.ShapeDtypeStruct((depth, rows, n), F32),
        compiler_params=_cparams(2),
        name="modulation",
    )(c_rows, mod_w, mod_b.reshape(depth, 1, n))


def _even_kernel(xm_ref, xp_ref, xn_ref, mod_ref, g_ref, win_ref, dww_ref, dwb_ref, lng_ref,
                 lnb_ref, pw_ref, ps_ref, wout_ref, o_ref, h_s, glu_s, pool_s, mix_s, *, n_lat):
    s = pl.program_id(1)
    nt = pl.num_programs(1)
    d_conv = glu_s.shape[1]
    mod = mod_ref[0, 0]
    shift, scale, gate = mod[0:1], mod[1:2], mod[2:3]
    g = g_ref[...]
    xm = xm_ref[0]
    h_s[0:HALO, :] = _norm_mod(xp_ref[0], g, shift, scale).astype(BF16)
    h_s[HALO:HALO + TS, :] = _norm_mod(xm, g, shift, scale).astype(BF16)
    h_s[HALO + TS:, :] = _norm_mod(xn_ref[0], g, shift, scale).astype(BF16)

    rows = HALO + TS + HALO
    ridx = lax.broadcasted_iota(I32, (rows, 1), 0)
    prev_ok = s >= 2
    next_ok = jnp.logical_and(s >= 1, s <= nt - 2)
    valid = jnp.logical_and(jnp.logical_or(ridx >= HALO, prev_ok),
                            jnp.logical_or(ridx < HALO + TS, next_ok))

    h = h_s[...]
    a = jnp.dot(h, win_ref[:, 0:d_conv], preferred_element_type=F32)
    gt = jnp.dot(h, win_ref[:, d_conv:2 * d_conv], preferred_element_type=F32)
    glu_s[...] = jnp.where(valid, a * jax.nn.sigmoid(gt), 0.0)
    pu = jnp.dot(h, win_ref[:, 2 * d_conv:], preferred_element_type=F32)
    pool_s[...] = jnp.where(valid, pu, 0.0)

    seq_pos0 = jnp.where(s == 0, 0, (s - 1) * TS)
    seq_len = jnp.where(s == 0, TS, n_lat)
    half = CONV_WIDTH // 2
    rc_rows = 64
    n_cc = d_conv // LANES
    for rc in range(TS // rc_rows):
        r0 = HALO + rc * rc_rows
        ys = []
        for cc in range(n_cc):
            cs = slice(cc * LANES, (cc + 1) * LANES)
            acc = jnp.zeros((rc_rows, LANES), F32) + dwb_ref[:, cs]
            for k in range(CONV_WIDTH):
                acc = acc + dww_ref[k:k + 1, cs] * glu_s[pl.ds(r0 + k - half, rc_rows), cs]
            ys.append(acc)
        mu = sum(jnp.sum(y, axis=-1, keepdims=True) for y in ys) / d_conv
        var = sum(jnp.sum((y - mu) * (y - mu), axis=-1, keepdims=True) for y in ys) / d_conv
        rs = lax.rsqrt(var + EPS)
        for cc in range(n_cc):
            cs = slice(cc * LANES, (cc + 1) * LANES)
            z = (ys[cc] - mu) * rs * lng_ref[:, cs] + lnb_ref[:, cs]
            mix_s[rc * rc_rows:(rc + 1) * rc_rows, cs] = _silu(z).astype(BF16)
        pos = seq_pos0 + rc * rc_rows + lax.broadcasted_iota(I32, (rc_rows, 1), 0)
        for gi, w in enumerate(POOL_WINDOWS):
            cs = slice(gi * LANES, (gi + 1) * LANES)
            ssum = pool_s[pl.ds(r0 - w // 2, rc_rows), cs]
            for j in range(1 - w // 2, w // 2):
                ssum = ssum + pool_s[pl.ds(r0 + j, rc_rows), cs]
            lo = jnp.clip(pos - w // 2, 0, seq_len)
            hi = jnp.clip(pos + w // 2, 0, seq_len)
            cnt = (hi - lo).astype(F32)
            res = ssum / cnt - pool_s[pl.ds(r0, rc_rows), cs]
            yp = jnp.dot(res.astype(BF16), pw_ref[gi], preferred_element_type=F32) * ps_ref[:, cs]
            mix_s[rc * rc_rows:(rc + 1) * rc_rows, d_conv + gi * LANES:d_conv + (gi + 1) * LANES] = (
                yp.astype(BF16))

    y = jnp.dot(mix_s[...], wout_ref[...], preferred_element_type=F32)
    o_ref[0] = xm + gate * y


def _even_layer(xa, modtab, norm_g, w_in, dw_w, dw_b, ln_g, ln_b, pool_w, pool_scale, w_out, n_lat):
    b, s_tot, d = xa.shape
    nt = s_tot // TS
    d_conv = dw_w.shape[1]
    d_pool = pool_scale.shape[0]
    assert d_pool == len(POOL_WINDOWS) * LANES and pool_w.shape[1] == LANES
    hb = TS // HALO
    n_hblk = s_tot // HALO
    const2 = lambda bi, si: (0, 0)
    return pl.pallas_call(
        functools.partial(_even_kernel, n_lat=n_lat),
        grid=(b, nt),
        in_specs=[
            pl.BlockSpec((1, TS, d), lambda bi, si: (bi, si, 0)),
            pl.BlockSpec((1, HALO, d), lambda bi, si: (bi, jnp.maximum(si * hb - 1, 0), 0)),
            pl.BlockSpec((1, HALO, d), lambda bi, si: (bi, jnp.minimum((si + 1) * hb, n_hblk - 1), 0)),
            pl.BlockSpec((1, 1, N_MOD, d), lambda bi, si: (bi, jnp.minimum(si, 1), 0, 0)),
            pl.BlockSpec((1, d), const2),
            pl.BlockSpec(w_in.shape, const2),
            pl.BlockSpec(dw_w.shape, const2),
            pl.BlockSpec((1, d_conv), const2),
            pl.BlockSpec((1, d_conv), const2),
            pl.BlockSpec((1, d_conv), const2),
            pl.BlockSpec(pool_w.shape, lambda bi, si: (0, 0, 0)),
            pl.BlockSpec((1, d_pool), const2),
            pl.BlockSpec(w_out.shape, const2),
        ],
        out_specs=pl.BlockSpec((1, TS, d), lambda bi, si: (bi, si, 0)),
        out_shape=jax.ShapeDtypeStruct(xa.shape, F32),
        scratch_shapes=[
            pltpu.VMEM((TS + 2 * HALO, d), BF16),
            pltpu.VMEM((TS + 2 * HALO, d_conv), F32),
            pltpu.VMEM((TS + 2 * HALO, d_pool), F32),
            pltpu.VMEM((TS, d_conv + d_pool), BF16),
        ],
        compiler_params=_cparams(2),
        name="even_mixer",
    )(xa, xa, xa, modtab, norm_g.reshape(1, d), w_in.astype(BF16), dw_w, dw_b.reshape(1, -1),
      ln_g.reshape(1, -1), ln_b.reshape(1, -1), pool_w.astype(BF16), pool_scale.reshape(1, -1),
      w_out.astype(BF16))


def _router_kernel(x_ref, mod_ref, g_ref, rwt_ref, rb_ref, tri_ref, h_ref, grp_ref, rank_ref,
                   cnt_ref, sc_s, carry_s):
    first = jnp.logical_and(pl.program_id(0) == 0, pl.program_id(1) == 0)

    @pl.when(first)
    def _():
        carry_s[...] = jnp.zeros_like(carry_s)

    mod = mod_ref[0, 0]
    h = _norm_mod(x_ref[0], g_ref[...], mod[3:4], mod[4:5])
    h_ref[0] = h
    logits = lax.dot_general(rwt_ref[...], h.astype(BF16), (((1,), (1,)), ((), ())),
                             preferred_element_type=F32)
    biased = jax.nn.sigmoid(logits) + rb_ref[...]
    n_half = TS // LANES
    for hh in range(n_half):
        sc_s[hh] = biased[:, hh * LANES:(hh + 1) * LANES]
    a, b, c, d = [
        jnp.concatenate([sc_s[hh, pl.ds(j, N_GROUPS, stride=EXPERTS_PER_GROUP), :]
                         for hh in range(n_half)], axis=-1)
        for j in range(EXPERTS_PER_GROUP)]
    top2 = jnp.maximum(jnp.maximum(jnp.maximum(a + b, a + c), jnp.maximum(a + d, b + c)),
                       jnp.maximum(b + d, c + d))
    gmax = jnp.max(top2, axis=0, keepdims=True)
    gi = lax.broadcasted_iota(I32, top2.shape, 0)
    sel = jnp.min(jnp.where(top2 == gmax, gi, N_GROUPS), axis=0, keepdims=True)
    onehot = gi == sel
    ohf = jnp.where(onehot, 1.0, 0.0)
    prefix = jnp.dot(ohf.astype(BF16), tri_ref[...], preferred_element_type=F32)
    carry = carry_s[:, 0:1]
    rank = jnp.sum(jnp.where(onehot, carry + prefix - 1.0, 0.0), axis=0, keepdims=True)
    carry_s[...] = carry_s[...] + jnp.sum(ohf, axis=1, keepdims=True)
    grp_ref[0] = sel
    rank_ref[0] = rank.astype(I32)
    cnt_ref[...] = carry_s[...]


def _moe_route(x, modtab, norm_g, router_w, router_b, kind_of_tile):
    b, s_tot, d = x.shape
    nt = s_tot // TS
    n_tiles = b * nt
    tri = (jnp.arange(TS)[:, None] <= jnp.arange(TS)[None, :]).astype(BF16)
    const2 = lambda bi, si: (0, 0)
    flat3 = lambda bi, si: (bi * nt + si, 0, 0)
    return pl.pallas_call(
        _router_kernel,
        grid=(b, nt),
        in_specs=[
            pl.BlockSpec((1, TS, d), lambda bi, si: (bi, si, 0)),
            pl.BlockSpec((1, 1, N_MOD, d), lambda bi, si: (bi, kind_of_tile(si), 0, 0)),
            pl.BlockSpec((1, d), const2),
            pl.BlockSpec((N_EXPERTS, d), const2),
            pl.BlockSpec((N_EXPERTS, 1), const2),
            pl.BlockSpec((TS, TS), const2),
        ],
        out_specs=[
            pl.BlockSpec((1, TS, d), lambda bi, si: (bi, si, 0)),
            pl.BlockSpec((1, 1, TS), flat3),
            pl.BlockSpec((1, 1, TS), flat3),
            pl.BlockSpec((N_GROUPS, LANES), const2),
        ],
        out_shape=[
            jax.ShapeDtypeStruct((b, s_tot, d), F32),
            jax.ShapeDtypeStruct((n_tiles, 1, TS), I32),
            jax.ShapeDtypeStruct((n_tiles, 1, TS), I32),
            jax.ShapeDtypeStruct((N_GROUPS, LANES), F32),
        ],
        scratch_shapes=[pltpu.VMEM((TS // LANES, N_EXPERTS, LANES), F32),
                        pltpu.VMEM((N_GROUPS, LANES), F32)],
        compiler_params=_cparams(2),
        name="moe_route",
    )(x, modtab, norm_g.reshape(1, d), router_w.T.astype(BF16), router_b.reshape(N_EXPERTS, 1), tri)


def _dispatch_kernel(pos_ref, src_ref, dst_in_ref, dst_ref, sem):
    del dst_in_ref
    i = pl.program_id(0)
    n = pl.num_programs(0)
    slot = lax.rem(i, 2)

    def issue(r, carry):
        p = pos_ref[0, 0, r]
        pltpu.make_async_copy(src_ref.at[pl.ds(i * TS + r, 1)], dst_ref.at[pl.ds(p, 1)],
                              sem.at[slot]).start()
        return carry

    lax.fori_loop(0, TS, issue, 0, unroll=8)

    def wait_tile(which):
        pltpu.make_async_copy(src_ref.at[pl.ds(0, TS)], dst_ref.at[pl.ds(0, TS)],
                              sem.at[which]).wait()

    @pl.when(i > 0)
    def _():
        wait_tile(1 - slot)

    @pl.when(i == n - 1)
    def _():
        wait_tile(slot)


def _moe_dispatch(h_flat, pos, n_rows):
    t, d = h_flat.shape
    n_tiles = t // TS
    return pl.pallas_call(
        _dispatch_kernel,
        grid=(n_tiles,),
        in_specs=[
            pl.BlockSpec((1, 1, TS), lambda i: (i, 0, 0), memory_space=pltpu.SMEM),
            pl.BlockSpec(memory_space=pl.ANY),
            pl.BlockSpec(memory_space=pl.ANY),
        ],
        out_specs=pl.BlockSpec(memory_space=pl.ANY),
        out_shape=jax.ShapeDtypeStruct((n_rows, d), F32),
        scratch_shapes=[pltpu.SemaphoreType.DMA((2,))],
        input_output_aliases={2: 0},
        compiler_params=_cparams(1),
        name="moe_dispatch",
    )(pos, h_flat, jnp.zeros((n_rows, d), F32))


def _ffn_kernel(tg_ref, tv_ref, xs_ref, rw_ref, rb_ref, wg_ref, wu_ref, wd_ref, o_ref):
    del tg_ref
    i = pl.program_id(0)

    @pl.when(tv_ref[i] == 0)
    def _():
        o_ref[...] = jnp.zeros_like(o_ref)

    @pl.when(tv_ref[i] != 0)
    def _():
        xb = xs_ref[...].astype(BF16)
        logits = jnp.dot(xb, rw_ref[0], preferred_element_type=F32)
        sc = jax.nn.sigmoid(logits)
        bs = sc + rb_ref[0]
        s_col = [sc[:, j:j + 1] for j in range(EXPERTS_PER_GROUP)]
        b_col = [bs[:, j:j + 1] for j in range(EXPERTS_PER_GROUP)]
        sel = []
        for j in range(EXPERTS_PER_GROUP):
            beaten = jnp.zeros_like(b_col[j])
            for k in range(EXPERTS_PER_GROUP):
                if k == j:
                    continue
                wins = (b_col[k] >= b_col[j]) if k < j else (b_col[k] > b_col[j])
                beaten = beaten + jnp.where(wins, 1.0, 0.0)
            sel.append(beaten < 2.0)
        den = sum(jnp.where(sel[j], s_col[j], 0.0) for j in range(EXPERTS_PER_GROUP))
        hes = []
        for j in range(EXPERTS_PER_GROUP):
            cj = jnp.where(sel[j], s_col[j] / den, 0.0)
            gj = jnp.dot(xb, wg_ref[0, j], preferred_element_type=F32)
            uj = jnp.dot(xb, wu_ref[0, j], preferred_element_type=F32)
            hes.append((_silu(gj) * uj * cj).astype(BF16))
        he = jnp.concatenate(hes, axis=-1)
        d_e = wd_ref.shape[2]
        wd = wd_ref[0].reshape(EXPERTS_PER_GROUP * d_e, wd_ref.shape[3])
        o_ref[...] = jnp.dot(he, wd, preferred_element_type=F32)


def _moe_ffn(xs, tile_group, tile_valid, rw_g, rb_g, w_gate, w_up, w_down):
    n_rows, d = xs.shape
    d_e = w_gate.shape[-1]
    n_tiles = n_rows // TMG
    epg = EXPERTS_PER_GROUP
    grid_spec = pltpu.PrefetchScalarGridSpec(
        num_scalar_prefetch=2,
        grid=(n_tiles,),
        in_specs=[
            pl.BlockSpec((TMG, d), lambda i, tg, tv: (i, 0)),
            pl.BlockSpec((1, d, LANES), lambda i, tg, tv: (tg[i], 0, 0)),
            pl.BlockSpec((1, 1, LANES), lambda i, tg, tv: (tg[i], 0, 0)),
            pl.BlockSpec((1, epg, d, d_e), lambda i, tg, tv: (0, tg[i], 0, 0)),
            pl.BlockSpec((1, epg, d, d_e), lambda i, tg, tv: (0, tg[i], 0, 0)),
            pl.BlockSpec((1, epg, d_e, d), lambda i, tg, tv: (0, tg[i], 0, 0)),
        ],
        out_specs=pl.BlockSpec((TMG, d), lambda i, tg, tv: (i, 0)),
    )
    return pl.pallas_call(
        _ffn_kernel,
        grid_spec=grid_spec,
        out_shape=jax.ShapeDtypeStruct((n_rows, d), F32),
        compiler_params=_cparams(1),
        name="moe_ffn",
    )(tile_group, tile_valid, xs, rw_g, rb_g, w_gate[None], w_up[None], w_down[None])


def _combine_kernel(pos_ref, posn_ref, x_ref, mod_ref, ys_ref, o_ref, buf, sem):
    i = pl.program_id(0) * pl.num_programs(1) + pl.program_id(1)
    n = pl.num_programs(0) * pl.num_programs(1)
    slot = lax.rem(i, 2)

    def gather(p_ref, which):
        def issue(r, carry):
            p = p_ref[0, 0, r]
            pltpu.make_async_copy(ys_ref.at[pl.ds(p, 1)], buf.at[which, pl.ds(r, 1)],
                                  sem.at[which]).start()
            return carry
        lax.fori_loop(0, TS, issue, 0, unroll=8)

    @pl.when(i == 0)
    def _():
        gather(pos_ref, slot)

    @pl.when(i + 1 < n)
    def _():
        gather(posn_ref, 1 - slot)

    pltpu.make_async_copy(ys_ref.at[pl.ds(0, TS)], buf.at[slot], sem.at[slot]).wait()
    gate = mod_ref[0, 0][5:6]
    o_ref[0] = x_ref[0] + gate * buf[slot]


def _moe_combine(x, modtab, pos, ys, kind_of_tile, tile_off, n_out_tiles):
    b, s_tot, d = x.shape
    nt = s_tot // TS
    n_tiles = b * nt
    def cur(bi, si):
        return (bi * nt + si + tile_off, 0, 0)
    def nxt(bi, si):
        last = si == n_out_tiles - 1
        nb = jnp.where(last, bi + 1, bi)
        ns = jnp.where(last, 0, si + 1)
        return (jnp.minimum(nb * nt + ns + tile_off, n_tiles - 1), 0, 0)
    return pl.pallas_call(
        _combine_kernel,
        grid=(b, n_out_tiles),
        in_specs=[
            pl.BlockSpec((1, 1, TS), cur, memory_space=pltpu.SMEM),
            pl.BlockSpec((1, 1, TS), nxt, memory_space=pltpu.SMEM),
            pl.BlockSpec((1, TS, d), lambda bi, si: (bi, si + tile_off, 0)),
            pl.BlockSpec((1, 1, N_MOD, d), lambda bi, si: (bi, kind_of_tile(si + tile_off), 0, 0)),
            pl.BlockSpec(memory_space=pl.ANY),
        ],
        out_specs=pl.BlockSpec((1, TS, d), lambda bi, si: (bi, si, 0)),
        out_shape=jax.ShapeDtypeStruct((b, n_out_tiles * TS, d), F32),
        scratch_shapes=[pltpu.VMEM((2, TS, d), F32), pltpu.SemaphoreType.DMA((2,))],
        compiler_params=_cparams(2),
        name="moe_combine",
    )(pos, pos, x, modtab, ys)


def _moe_layer(x, modtab, norm_g, router_w, router_b, w_gate, w_up, w_down, kind_of_tile,
               tile_off, n_out_tiles):
    b, s_tot, d = x.shape
    t = b * s_tot
    h, grp, rank, cnt = _moe_route(x, modtab, norm_g, router_w, router_b, kind_of_tile)
    counts = cnt[:, 0].astype(I32)
    padded = ((counts + TMG - 1) // TMG) * TMG
    ends = jnp.cumsum(padded)
    starts = ends - padded
    pos = starts[grp] + rank
    n_rows = t + N_GROUPS * TMG
    tile_row0 = jnp.arange(n_rows // TMG, dtype=I32) * TMG
    tile_group = jnp.minimum(jnp.sum(tile_row0[:, None] >= ends[None, :], axis=1),
                             N_GROUPS - 1).astype(I32)
    tile_valid = (tile_row0 < ends[-1]).astype(I32)
    xs = _moe_dispatch(h.reshape(t, d), pos, n_rows)
    epg = EXPERTS_PER_GROUP
    rw_g = jnp.pad(router_w.reshape(d, N_GROUPS, epg).transpose(1, 0, 2),
                   ((0, 0), (0, 0), (0, LANES - epg))).astype(BF16)
    rb_g = jnp.pad(router_b.reshape(N_GROUPS, 1, epg), ((0, 0), (0, 0), (0, LANES - epg)))
    ys = _moe_ffn(xs, tile_group, tile_valid, rw_g, rb_g, w_gate.astype(BF16), w_up.astype(BF16),
                  w_down.astype(BF16))
    return _moe_combine(x, modtab, pos, ys, kind_of_tile, tile_off, n_out_tiles)


def _rope_pad(w, axis):
    half = QK_ROPE // 2
    x1, x2 = jnp.split(w, 2, axis=axis)
    z = jnp.zeros_like(x1)
    del half
    return jnp.concatenate([x1, z, x2, z], axis=axis)


def _oddproj_kernel(x_ref, mod_ref, g_ref, win_ref, gcq_ref, wuq_ref, gckv_ref, wukv_ref, gqm_ref,
                    gkm_ref, gqd_ref, gkd_ref, cs_ref, sna_ref, snb_ref, csm_ref, snm_ref,
                    qm_ref, km_ref, vm_ref, qd_ref, kd_ref, vd_ref):
    mod = mod_ref[0, 0]
    h = _norm_mod(x_ref[0], g_ref[...], mod[0:1], mod[1:2]).astype(BF16)
    q_lora = gcq_ref.shape[1]
    kv_lora = gckv_ref.shape[1]
    n_qd = DIFF_HEADS * 2 * DIFF_QK
    o_qd = q_lora
    o_ckv = o_qd + n_qd
    o_kr = o_ckv + kv_lora
    o_kd = o_kr + LANES
    o_vd = o_kd + n_qd
    lane = lax.broadcasted_iota(I32, (1, LANES), 1)
    low = lane < DIFF_QK
    d_mla = QK_NOPE + QK_ROPE

    def rms(x, g, n):
        return x * lax.rsqrt(jnp.sum(x * x, axis=-1, keepdims=True) / n + EPS) * g

    def rope_mla(y):
        return y * csm_ref[...] + pltpu.roll(y, LANES // 2, 1) * snm_ref[...]

    def diff_cols(p, g):
        sq = p * p
        s_all = jnp.sum(sq, axis=-1, keepdims=True)
        s_lo = jnp.sum(jnp.where(low, sq, 0.0), axis=-1, keepdims=True)
        r = jnp.where(low, lax.rsqrt(s_lo / DIFF_QK + EPS),
                      lax.rsqrt((s_all - s_lo) / DIFF_QK + EPS))
        y = p * r * g
        return (y * cs_ref[...] + pltpu.roll(y, LANES - DIFF_QK // 2, 1) * sna_ref[...]
                + pltpu.roll(y, DIFF_QK // 2, 1) * snb_ref[...])

    cq = rms(jnp.dot(h, win_ref[:, 0:q_lora], preferred_element_type=F32), gcq_ref[...], q_lora)
    qm = jnp.dot(cq.astype(BF16), wuq_ref[...], preferred_element_type=F32)
    for hd in range(MLA_HEADS):
        qh = rms(qm[:, 2 * LANES * hd:2 * LANES * (hd + 1)], gqm_ref[...], d_mla)
        qm_ref[0, hd] = jnp.concatenate([qh[:, :LANES], rope_mla(qh[:, LANES:])], axis=-1).astype(BF16)
    for hd in range(DIFF_HEADS):
        p = jnp.dot(h, win_ref[:, o_qd + LANES * hd:o_qd + LANES * (hd + 1)],
                    preferred_element_type=F32)
        y = diff_cols(p, gqd_ref[...])
        qd_ref[0, 2 * hd] = jnp.where(low, y, 0.0).astype(BF16)
        qd_ref[0, 2 * hd + 1] = jnp.where(low, 0.0, y).astype(BF16)

    ckv = rms(jnp.dot(h, win_ref[:, o_ckv:o_ckv + kv_lora], preferred_element_type=F32),
              gckv_ref[...], kv_lora)
    kr = jnp.dot(h, win_ref[:, o_kr:o_kr + LANES], preferred_element_type=F32)
    kv = jnp.dot(ckv.astype(BF16), wukv_ref[...], preferred_element_type=F32)
    for hd in range(MLA_HEADS):
        kcat = jnp.concatenate([kv[:, 2 * LANES * hd:2 * LANES * hd + LANES], kr], axis=-1)
        kh = rms(kcat, gkm_ref[...], d_mla)
        km_ref[0, hd] = jnp.concatenate([kh[:, :LANES], rope_mla(kh[:, LANES:])], axis=-1).astype(BF16)
        vm_ref[0, hd] = kv[:, 2 * LANES * hd + LANES:2 * LANES * (hd + 1)].astype(BF16)
    for hd in range(DIFF_HEADS):
        p = jnp.dot(h, win_ref[:, o_kd + LANES * hd:o_kd + LANES * (hd + 1)],
                    preferred_element_type=F32)
        kd_ref[0, hd] = diff_cols(p, gkd_ref[...]).astype(BF16)
        vd_ref[0, hd] = jnp.dot(h, win_ref[:, o_vd + LANES * hd:o_vd + LANES * (hd + 1)],
                                preferred_element_type=F32).astype(BF16)


def _odd_project(xa, modtab, norm_g, w_in, g_cq, w_uq, g_ckv, w_ukv, g_mla, g_diff, n_ctx):
    b, s_tot, d = xa.shape
    nt = s_tot // TS
    n_lat = s_tot - n_ctx
    q_lora = g_cq.shape[0]
    kv_lora = g_ckv.shape[0]
    n_qd = DIFF_HEADS * 2 * DIFF_QK
    d_mla = QK_NOPE + QK_ROPE
    o = 0
    w_cq = w_in[:, o:o + q_lora]; o += q_lora
    w_qd = w_in[:, o:o + n_qd]; o += n_qd
    w_ckv = w_in[:, o:o + kv_lora]; o += kv_lora
    w_kr = w_in[:, o:o + QK_ROPE]; o += QK_ROPE
    w_kd = w_in[:, o:o + n_qd]; o += n_qd
    w_vd = w_in[:, o:]
    w_in_p = jnp.concatenate([w_cq, w_qd, w_ckv, _rope_pad(w_kr, 1), w_kd, w_vd], axis=1).astype(BF16)
    wuq = w_uq.reshape(q_lora, MLA_HEADS, d_mla)
    wuq_p = jnp.concatenate([wuq[..., :QK_NOPE], _rope_pad(wuq[..., QK_NOPE:], 2)],
                            axis=-1).reshape(q_lora, MLA_HEADS * 2 * LANES).astype(BF16)

    def pad_gain(g):
        return jnp.concatenate([g[:QK_NOPE], _rope_pad(g[QK_NOPE:], 0)]).reshape(1, 2 * LANES)

    rows = n_lat // GRID_W
    row = jnp.repeat(jnp.arange(rows, dtype=F32), GRID_W)
    col = jnp.tile(jnp.arange(GRID_W, dtype=F32), rows)
    axis_dim = QK_ROPE // 2
    inv_freq = ROPE_BASE ** (-jnp.arange(0, axis_dim, 2, dtype=F32) / axis_dim)
    ang = jnp.concatenate([row[:, None] * inv_freq, col[:, None] * inv_freq], axis=-1)
    half = QK_ROPE // 2
    cos = jnp.concatenate([jnp.ones((n_ctx, half), F32), jnp.cos(ang)], axis=0)
    sin = jnp.concatenate([jnp.zeros((n_ctx, half), F32), jnp.sin(ang)], axis=0)
    z = jnp.zeros_like(sin)
    cs_d = jnp.tile(cos, (1, LANES // half))
    sna_d = jnp.tile(jnp.concatenate([-sin, z], axis=1), (1, LANES // QK_ROPE))
    snb_d = jnp.tile(jnp.concatenate([z, sin], axis=1), (1, LANES // QK_ROPE))
    cs_m = jnp.concatenate([cos, z, cos, z], axis=1)
    sn_m = jnp.concatenate([-sin, z, sin, z], axis=1)

    const2 = lambda bi, si: (0, 0)
    tok = lambda bi, si: (si, 0)
    kv4 = lambda bi, si: (bi, 0, si, 0)
    q4 = lambda bi, si: (bi, 0, jnp.maximum(si - n_ctx // TS, 0), 0)
    full = lambda a: pl.BlockSpec(a.shape, const2)
    gq_d = jnp.tile(g_diff[0], LANES // DIFF_QK).reshape(1, LANES)
    gk_d = jnp.tile(g_diff[1], LANES // DIFF_QK).reshape(1, LANES)
    args = [xa, modtab, norm_g.reshape(1, d), w_in_p, g_cq.reshape(1, -1), wuq_p, g_ckv.reshape(1, -1),
            w_ukv.astype(BF16), pad_gain(g_mla[0]), pad_gain(g_mla[1]), gq_d, gk_d,
            cs_d, sna_d, snb_d, cs_m, sn_m]
    in_specs = [
        pl.BlockSpec((1, TS, d), lambda bi, si: (bi, si, 0)),
        pl.BlockSpec((1, 1, N_MOD, d), lambda bi, si: (bi, jnp.minimum(si, 1), 0, 0)),
    ] + [full(a) for a in args[2:12]] + [pl.BlockSpec((TS, LANES), tok)] * 5
    hm, hd = MLA_HEADS, DIFF_HEADS
    return pl.pallas_call(
        _oddproj_kernel,
        grid=(b, nt),
        in_specs=in_specs,
        out_specs=[
            pl.BlockSpec((1, hm, TS, 2 * LANES), q4),
            pl.BlockSpec((1, hm, TS, 2 * LANES), kv4),
            pl.BlockSpec((1, hm, TS, V_HEAD), kv4),
            pl.BlockSpec((1, 2 * hd, TS, LANES), q4),
            pl.BlockSpec((1, hd, TS, LANES), kv4),
            pl.BlockSpec((1, hd, TS, DIFF_V), kv4),
        ],
        out_shape=[
            jax.ShapeDtypeStruct((b, hm, n_lat, 2 * LANES), BF16),
            jax.ShapeDtypeStruct((b, hm, s_tot, 2 * LANES), BF16),
            jax.ShapeDtypeStruct((b, hm, s_tot, V_HEAD), BF16),
            jax.ShapeDtypeStruct((b, 2 * hd, n_lat, LANES), BF16),
            jax.ShapeDtypeStruct((b, hd, s_tot, LANES), BF16),
            jax.ShapeDtypeStruct((b, hd, s_tot, DIFF_V), BF16),
        ],
        compiler_params=_cparams(2),
        name="odd_project",
    )(*args)


def _flash_kernel(q_ref, k_ref, v_ref, o_ref, *, scale, n_chunks):
    q = q_ref[0, 0]
    tq = q.shape[0]
    dv = v_ref.shape[-1]

    def body(c, carry):
        m, l, acc = carry
        k = k_ref[0, 0, pl.ds(pl.multiple_of(c * TK, TK), TK), :]
        v = v_ref[0, 0, pl.ds(pl.multiple_of(c * TK, TK), TK), :]
        s = lax.dot_general(q, k, (((1,), (1,)), ((), ())), preferred_element_type=F32) * scale
        m_new = jnp.maximum(m, jnp.max(s, axis=-1, keepdims=True))
        alpha = jnp.exp(m - m_new)
        p = jnp.exp(s - m_new)
        l = alpha * l + jnp.sum(p, axis=-1, keepdims=True)
        acc = alpha * acc + jnp.dot(p.astype(BF16), v, preferred_element_type=F32)
        return m_new, l, acc

    m0 = jnp.full((tq, 1), -jnp.inf, F32)
    l0 = jnp.zeros((tq, 1), F32)
    a0 = jnp.zeros((tq, dv), F32)
    _, l, acc = lax.fori_loop(0, n_chunks, body, (m0, l0, a0))
    o_ref[0] = (acc / l).astype(o_ref.dtype)


def _attention(q, k, v, scale, out_dtype, name):
    b, hq, lq, dk = q.shape
    hk, lk = k.shape[1], k.shape[2]
    dv = v.shape[-1]
    rep = hq // hk
    tq = min(TQ, lq)
    return pl.pallas_call(
        functools.partial(_flash_kernel, scale=scale, n_chunks=lk // TK),
        grid=(b, hq, lq // tq),
        in_specs=[
            pl.BlockSpec((1, 1, tq, dk), lambda bi, hi, qi: (bi, hi, qi, 0)),
            pl.BlockSpec((1, 1, lk, dk), lambda bi, hi, qi: (bi, hi // rep, 0, 0)),
            pl.BlockSpec((1, 1, lk, dv), lambda bi, hi, qi: (bi, hi // rep, 0, 0)),
        ],
        out_specs=pl.BlockSpec((1, tq, dv), lambda bi, hi, qi: (bi, qi, hi)),
        out_shape=jax.ShapeDtypeStruct((b, lq, hq * dv), out_dtype),
        compiler_params=_cparams(3),
        name=name,
    )(q, k, v)


def _oddout_kernel(x_ref, mod_ref, om_ref, ad_ref, dl_ref, gs_ref, wout_ref, o_ref, *, lam_init):
    dl = dl_ref[...]
    lam = (jnp.exp(jnp.sum(dl[0:1] * dl[1:2], axis=-1, keepdims=True))
           - jnp.exp(jnp.sum(dl[2:3] * dl[3:4], axis=-1, keepdims=True)) + lam_init)
    ad = ad_ref[0]
    parts = [om_ref[0]]
    for hd in range(DIFF_HEADS):
        a1 = ad[:, 2 * hd * DIFF_V:(2 * hd + 1) * DIFF_V]
        a2 = ad[:, (2 * hd + 1) * DIFF_V:(2 * hd + 2) * DIFF_V]
        df = a1 - lam * a2
        od = df * lax.rsqrt(jnp.mean(df * df, axis=-1, keepdims=True) + EPS) * gs_ref[...]
        parts.append((od * (1.0 - lam_init)).astype(BF16))
    y = jnp.dot(jnp.concatenate(parts, axis=-1), wout_ref[...], preferred_element_type=F32)
    o_ref[0] = x_ref[0] + mod_ref[0, 0][2:3] * y


def _odd_output(xa, modtab, o_m, a_d, diff_lambda, g_sub, w_out, lam_init, n_ctx):
    b, s_tot, d = xa.shape
    n_lat = s_tot - n_ctx
    off = n_ctx // TS
    const2 = lambda bi, si: (0, 0)
    return pl.pallas_call(
        functools.partial(_oddout_kernel, lam_init=lam_init),
        grid=(b, n_lat // TS),
        in_specs=[
            pl.BlockSpec((1, TS, d), lambda bi, si: (bi, si + off, 0)),
            pl.BlockSpec((1, 1, N_MOD, d), lambda bi, si: (bi, 1, 0, 0)),
            pl.BlockSpec((1, TS, o_m.shape[-1]), lambda bi, si: (bi, si, 0)),
            pl.BlockSpec((1, TS, a_d.shape[-1]), lambda bi, si: (bi, si, 0)),
            pl.BlockSpec(diff_lambda.shape, const2),
            pl.BlockSpec((1, DIFF_V), const2),
            pl.BlockSpec(w_out.shape, const2),
        ],
        out_specs=pl.BlockSpec((1, TS, d), lambda bi, si: (bi, si, 0)),
        out_shape=jax.ShapeDtypeStruct((b, n_lat, d), F32),
        compiler_params=_cparams(2),
        name="odd_output",
    )(xa, modtab, o_m, a_d, diff_lambda, g_sub.reshape(1, -1), w_out.astype(BF16))


def kernel(x, c, ctx, c_ctx, mod_w, mod_b, norm_g, even_w_in, conv_dw_w, conv_dw_b, conv_ln_g,
           conv_ln_b, pool_w, pool_scale, even_w_out, odd_w_in, mla_g_cq, mla_w_uq, mla_g_ckv,
           mla_w_ukv, qk_g_mla, qk_g_diff, diff_lambda, diff_sub_g, odd_w_out, router_w, router_b,
           moe_w_gate, moe_w_up, moe_w_down):
    b, n_lat, d = x.shape
    n_ctx = ctx.shape[1]
    assert n_ctx == TS and n_lat % TS == 0 and mod_w.shape[0] == 2
    nt = (n_ctx + n_lat) // TS

    c_rows = jnp.concatenate([c, c_ctx[None], jnp.zeros((16 - b - 1, d), F32)], axis=0)
    mods = _modulation(c_rows, mod_w, mod_b)
    modtabs = []
    for i in range(2):
        mod_l = mods[i, :b].reshape(b, N_MOD, d)
        mod_c = jnp.broadcast_to(mods[i, b].reshape(1, N_MOD, d), (b, N_MOD, d))
        modtabs.append(jnp.stack([mod_c, mod_l], axis=1))

    xa = jnp.concatenate([ctx, x], axis=1)
    ctx_or_lat = lambda si: jnp.minimum(si, 1)
    lat_only = lambda si: 1

    xa = _even_layer(xa, modtabs[0], norm_g[0, 0], even_w_in[0], conv_dw_w[0], conv_dw_b[0],
                     conv_ln_g[0], conv_ln_b[0], pool_w[0], pool_scale[0], even_w_out[0], n_lat)
    xa = _moe_layer(xa, modtabs[0], norm_g[0, 1], router_w, router_b, moe_w_gate[0], moe_w_up[0],
                    moe_w_down[0], ctx_or_lat, 0, nt)

    qm, km, vm, qd, kd, vd = _odd_project(xa, modtabs[1], norm_g[1, 0], odd_w_in[0], mla_g_cq[0],
                                          mla_w_uq[0], mla_g_ckv[0], mla_w_ukv[0], qk_g_mla[0],
                                          qk_g_diff[0], n_ctx)
    o_m = _attention(qm, km, vm, MLA_SCALE, BF16, "attn_mla")
    a_d = _attention(qd, kd, vd, DIFF_SCALE, F32, "attn_diff")
    lam_init = 0.8 - 0.6 * math.exp(-0.3 * 1)
    xl = _odd_output(xa, modtabs[1], o_m, a_d, diff_lambda[0], diff_sub_g[0], odd_w_out[0],
                     lam_init, n_ctx)
    return _moe_layer(xl, modtabs[1], norm_g[1, 1], router_w, router_b, moe_w_gate[1], moe_w_up[1],
                      moe_w_down[1], lat_only, 0, n_lat // TS)
```

```python
import functools
import math

import jax
import jax.numpy as jnp
from jax import lax
from jax.experimental import pallas as pl
from jax.experimental.pallas import tpu as pltpu

F32 = jnp.float32
BF16 = jnp.bfloat16
I32 = jnp.int32

GRID_W = 64
N_MOD = 6
EPS = 1e-6
CONV_WIDTH = 31
POOL_WINDOWS = (2, 4, 8, 16)
MLA_HEADS = 4
QK_NOPE = 128
QK_ROPE = 64
V_HEAD = 128
DIFF_HEADS = 4
DIFF_QK = 64
DIFF_V = 128
MLA_SCALE = (QK_NOPE + QK_ROPE) ** -0.5
DIFF_SCALE = DIFF_QK ** -0.5
ROPE_BASE = 10000.0
N_EXPERTS = 32
N_GROUPS = 8
EXPERTS_PER_GROUP = N_EXPERTS // N_GROUPS

LANES = 128
SUBLANES = 8
TS = 256
HALO = 16
TMG = 256
TQ = 256
TK = 256
DISPATCH_BUFS = 3
VMEM_LIMIT = 48 * 1024 * 1024


def _cparams(n_axes):
    return pltpu.CompilerParams(
        dimension_semantics=("arbitrary",) * n_axes, vmem_limit_bytes=VMEM_LIMIT)


def _norm_mod(x, g, shift, scale):
    ms = jnp.mean(x * x, axis=-1, keepdims=True)
    return (x * lax.rsqrt(ms + EPS) * g) * (1.0 + scale) + shift


def _silu(x):
    return x * jax.nn.sigmoid(x)


def _mod_kernel(c_ref, w_ref, b_ref, o_ref):
    a = _silu(c_ref[...])
    o_ref[0] = jnp.dot(a.astype(BF16), w_ref[0].astype(BF16),
                       preferred_element_type=F32) + b_ref[0]


def _modulation(c_rows, mod_w, mod_b):
    depth, d, n = mod_w.shape
    tn = 1536
    rows = c_rows.shape[0]
    return pl.pallas_call(
        _mod_kernel,
        grid=(depth, n // tn),
        in_specs=[
            pl.BlockSpec((rows, d), lambda l, j: (0, 0)),
            pl.BlockSpec((1, d, tn), lambda l, j: (l, 0, j)),
            pl.BlockSpec((1, 1, tn), lambda l, j: (l, 0, j)),
        ],
        out_specs=pl.BlockSpec((1, rows, tn), lambda l, j: (l, 0, j)),
        out_shape=jax.ShapeDtypeStruct((depth, rows, n), F32),
        compiler_params=_cparams(2),
        name="modulation",
    )(c_rows, mod_w, mod_b.reshape(depth, 1, n))


def _even_kernel(xm_ref, xp_ref, xn_ref, mod_ref, g_ref, win_ref, dww_ref, dwb_ref, lng_ref,
                 lnb_ref, pw_ref, ps_ref, wout_ref, o_ref, h_s, glu_s, pool_s, pres_s, mix_s, *,
                 n_lat):
    s = pl.program_id(1)
    nt = pl.num_programs(1)
    d_conv = glu_s.shape[2]
    mod = mod_ref[0, 0]
    shift, scale, gate = mod[0:1], mod[1:2], mod[2:3]
    g = g_ref[...]
    xm = xm_ref[0]
    h_s[0:HALO, :] = _norm_mod(xp_ref[0], g, shift, scale).astype(BF16)
    h_s[HALO:HALO + TS, :] = _norm_mod(xm, g, shift, scale).astype(BF16)
    h_s[HALO + TS:, :] = _norm_mod(xn_ref[0], g, shift, scale).astype(BF16)

    rows = HALO + TS + HALO
    ridx = lax.broadcasted_iota(I32, (rows, 1), 0)
    prev_ok = s >= 2
    next_ok = jnp.logical_and(s >= 1, s <= nt - 2)
    valid = jnp.logical_and(jnp.logical_or(ridx >= HALO, prev_ok),
                            jnp.logical_or(ridx < HALO + TS, next_ok))

    h = h_s[...]
    a = jnp.dot(h, win_ref[:, 0:d_conv], preferred_element_type=F32)
    gt = jnp.dot(h, win_ref[:, d_conv:2 * d_conv], preferred_element_type=F32)
    glu_s[0] = jnp.where(valid, a * jax.nn.sigmoid(gt), 0.0)
    for j in range(1, SUBLANES):
        glu_s[j, 0:rows - SUBLANES, :] = glu_s[0, pl.ds(j, rows - SUBLANES), :]
    pu = jnp.dot(h, win_ref[:, 2 * d_conv:], preferred_element_type=F32)
    pool_s[...] = jnp.where(valid, pu, 0.0)

    seq_pos0 = jnp.where(s == 0, 0, (s - 1) * TS)
    seq_len = jnp.where(s == 0, TS, n_lat)
    half = CONV_WIDTH // 2
    rc_rows = 64
    n_cc = d_conv // LANES
    for rc in range(TS // rc_rows):
        r0 = HALO + rc * rc_rows
        ys = []
        for cc in range(n_cc):
            cs = slice(cc * LANES, (cc + 1) * LANES)
            acc = jnp.zeros((rc_rows, LANES), F32) + dwb_ref[:, cs]
            for k in range(CONV_WIDTH):
                sh = (r0 + k - half) % SUBLANES
                acc = acc + dww_ref[k:k + 1, cs] * glu_s[sh, pl.ds(r0 + k - half - sh, rc_rows), cs]
            ys.append(acc)
        mu = sum(jnp.sum(y, axis=-1, keepdims=True) for y in ys) / d_conv
        var = sum(jnp.sum((y - mu) * (y - mu), axis=-1, keepdims=True) for y in ys) / d_conv
        rs = lax.rsqrt(var + EPS)
        for cc in range(n_cc):
            cs = slice(cc * LANES, (cc + 1) * LANES)
            z = (ys[cc] - mu) * rs * lng_ref[:, cs] + lnb_ref[:, cs]
            mix_s[rc * rc_rows:(rc + 1) * rc_rows, cs] = _silu(z).astype(BF16)
        pos = seq_pos0 + rc * rc_rows + lax.broadcasted_iota(I32, (rc_rows, 1), 0)
        for gi, w in enumerate(POOL_WINDOWS):
            cs = slice(gi * LANES, (gi + 1) * LANES)
            ssum = pool_s[pl.ds(r0 - w // 2, rc_rows), cs]
            for j in range(1 - w // 2, w // 2):
                ssum = ssum + pool_s[pl.ds(r0 + j, rc_rows), cs]
            lo = jnp.clip(pos - w // 2, 0, seq_len)
            hi = jnp.clip(pos + w // 2, 0, seq_len)
            cnt = (hi - lo).astype(F32)
            res = ssum / cnt - pool_s[pl.ds(r0, rc_rows), cs]
            pres_s[rc * rc_rows:(rc + 1) * rc_rows, cs] = res.astype(BF16)

    yp = jnp.dot(pres_s[...], pw_ref[...], preferred_element_type=F32) * ps_ref[...]
    mix_s[:, d_conv:] = yp.astype(BF16)
    y = jnp.dot(mix_s[...], wout_ref[...], preferred_element_type=F32)
    o_ref[0] = xm + gate * y


def _even_layer(xa, modtab, norm_g, w_in, dw_w, dw_b, ln_g, ln_b, pool_w, pool_scale, w_out, n_lat):
    b, s_tot, d = xa.shape
    nt = s_tot // TS
    d_conv = dw_w.shape[1]
    d_pool = pool_scale.shape[0]
    assert d_pool == len(POOL_WINDOWS) * LANES and pool_w.shape[1] == LANES
    pool_bd = jnp.zeros((d_pool, d_pool), F32)
    for gi in range(len(POOL_WINDOWS)):
        pool_bd = pool_bd.at[gi * LANES:(gi + 1) * LANES, gi * LANES:(gi + 1) * LANES].set(pool_w[gi])
    hb = TS // HALO
    n_hblk = s_tot // HALO
    const2 = lambda bi, si: (0, 0)
    return pl.pallas_call(
        functools.partial(_even_kernel, n_lat=n_lat),
        grid=(b, nt),
        in_specs=[
            pl.BlockSpec((1, TS, d), lambda bi, si: (bi, si, 0)),
            pl.BlockSpec((1, HALO, d), lambda bi, si: (bi, jnp.maximum(si * hb - 1, 0), 0)),
            pl.BlockSpec((1, HALO, d), lambda bi, si: (bi, jnp.minimum((si + 1) * hb, n_hblk - 1), 0)),
            pl.BlockSpec((1, 1, N_MOD, d), lambda bi, si: (bi, jnp.minimum(si, 1), 0, 0)),
            pl.BlockSpec((1, d), const2),
            pl.BlockSpec(w_in.shape, const2),
            pl.BlockSpec(dw_w.shape, const2),
            pl.BlockSpec((1, d_conv), const2),
            pl.BlockSpec((1, d_conv), const2),
            pl.BlockSpec((1, d_conv), const2),
            pl.BlockSpec((d_pool, d_pool), const2),
            pl.BlockSpec((1, d_pool), const2),
            pl.BlockSpec(w_out.shape, const2),
        ],
        out_specs=pl.BlockSpec((1, TS, d), lambda bi, si: (bi, si, 0)),
        out_shape=jax.ShapeDtypeStruct(xa.shape, F32),
        scratch_shapes=[
            pltpu.VMEM((TS + 2 * HALO, d), BF16),
            pltpu.VMEM((SUBLANES, TS + 2 * HALO, d_conv), F32),
            pltpu.VMEM((TS + 2 * HALO, d_pool), F32),
            pltpu.VMEM((TS, d_pool), BF16),
            pltpu.VMEM((TS, d_conv + d_pool), BF16),
        ],
        compiler_params=_cparams(2),
        name="even_mixer",
    )(xa, xa, xa, modtab, norm_g.reshape(1, d), w_in.astype(BF16), dw_w, dw_b.reshape(1, -1),
      ln_g.reshape(1, -1), ln_b.reshape(1, -1), pool_bd.astype(BF16), pool_scale.reshape(1, -1),
      w_out.astype(BF16))


def _router_kernel(x_ref, mod_ref, g_ref, rwt_ref, rb_ref, tri_ref, h_ref, grp_ref, rank_ref,
                   cnt_ref, sc_s, carry_s):
    first = jnp.logical_and(pl.program_id(0) == 0, pl.program_id(1) == 0)

    @pl.when(first)
    def _():
        carry_s[...] = jnp.zeros_like(carry_s)

    mod = mod_ref[0, 0]
    h = _norm_mod(x_ref[0], g_ref[...], mod[3:4], mod[4:5])
    h_ref[0] = h
    logits = lax.dot_general(rwt_ref[...], h.astype(BF16), (((1,), (1,)), ((), ())),
                             preferred_element_type=F32)
    biased = jax.nn.sigmoid(logits) + rb_ref[...]
    n_half = TS // LANES
    for hh in range(n_half):
        sc_s[hh] = biased[:, hh * LANES:(hh + 1) * LANES]
    a, b, c, d = [
        jnp.concatenate([sc_s[hh, pl.ds(j, N_GROUPS, stride=EXPERTS_PER_GROUP), :]
                         for hh in range(n_half)], axis=-1)
        for j in range(EXPERTS_PER_GROUP)]
    top2 = jnp.maximum(jnp.maximum(jnp.maximum(a + b, a + c), jnp.maximum(a + d, b + c)),
                       jnp.maximum(b + d, c + d))
    gmax = jnp.max(top2, axis=0, keepdims=True)
    gi = lax.broadcasted_iota(I32, top2.shape, 0)
    sel = jnp.min(jnp.where(top2 == gmax, gi, N_GROUPS), axis=0, keepdims=True)
    onehot = gi == sel
    ohf = jnp.where(onehot, 1.0, 0.0)
    prefix = jnp.dot(ohf.astype(BF16), tri_ref[...], preferred_element_type=F32)
    carry = carry_s[:, 0:1]
    rank = jnp.sum(jnp.where(onehot, carry + prefix - 1.0, 0.0), axis=0, keepdims=True)
    carry_s[...] = carry_s[...] + jnp.sum(ohf, axis=1, keepdims=True)
    grp_ref[0] = sel
    rank_ref[0] = rank.astype(I32)
    cnt_ref[...] = carry_s[...]


def _moe_route(x, modtab, norm_g, router_w, router_b, kind_of_tile):
    b, s_tot, d = x.shape
    nt = s_tot // TS
    n_tiles = b * nt
    tri = (jnp.arange(TS)[:, None] <= jnp.arange(TS)[None, :]).astype(BF16)
    const2 = lambda bi, si: (0, 0)
    flat3 = lambda bi, si: (bi * nt + si, 0, 0)
    return pl.pallas_call(
        _router_kernel,
        grid=(b, nt),
        in_specs=[
            pl.BlockSpec((1, TS, d), lambda bi, si: (bi, si, 0)),
            pl.BlockSpec((1, 1, N_MOD, d), lambda bi, si: (bi, kind_of_tile(si), 0, 0)),
            pl.BlockSpec((1, d), const2),
            pl.BlockSpec((N_EXPERTS, d), const2),
            pl.BlockSpec((N_EXPERTS, 1), const2),
            pl.BlockSpec((TS, TS), const2),
        ],
        out_specs=[
            pl.BlockSpec((1, TS, d), lambda bi, si: (bi, si, 0)),
            pl.BlockSpec((1, 1, TS), flat3),
            pl.BlockSpec((1, 1, TS), flat3),
            pl.BlockSpec((N_GROUPS, LANES), const2),
        ],
        out_shape=[
            jax.ShapeDtypeStruct((b, s_tot, d), F32),
            jax.ShapeDtypeStruct((n_tiles, 1, TS), I32),
            jax.ShapeDtypeStruct((n_tiles, 1, TS), I32),
            jax.ShapeDtypeStruct((N_GROUPS, LANES), F32),
        ],
        scratch_shapes=[pltpu.VMEM((TS // LANES, N_EXPERTS, LANES), F32),
                        pltpu.VMEM((N_GROUPS, LANES), F32)],
        compiler_params=_cparams(2),
        name="moe_route",
    )(x, modtab, norm_g.reshape(1, d), router_w.T.astype(BF16), router_b.reshape(N_EXPERTS, 1), tri)


def _dispatch_kernel(pos_ref, src_ref, dst_in_ref, dst_ref, buf, ld_sem, st_sem):
    del dst_in_ref
    i = pl.program_id(0)
    n = pl.num_programs(0)
    slot = lax.rem(i, DISPATCH_BUFS)
    nxt = lax.rem(i + 1, DISPATCH_BUFS)

    def load(tile, which):
        return pltpu.make_async_copy(src_ref.at[pl.ds(tile * TS, TS)], buf.at[which],
                                     ld_sem.at[which])

    def wait_scatter(which):
        pltpu.make_async_copy(buf.at[which], dst_ref.at[pl.ds(0, TS)], st_sem.at[which]).wait()

    @pl.when(i == 0)
    def _():
        load(i, slot).start()

    load(i, slot).wait()

    @pl.when(i + 1 >= DISPATCH_BUFS)
    def _():
        wait_scatter(nxt)

    @pl.when(i + 1 < n)
    def _():
        load(i + 1, nxt).start()

    def issue(r, carry):
        p = pos_ref[0, 0, r]
        pltpu.make_async_copy(buf.at[slot, pl.ds(r, 1)], dst_ref.at[pl.ds(p, 1)],
                              st_sem.at[slot]).start()
        return carry

    lax.fori_loop(0, TS, issue, 0, unroll=8)

    @pl.when(i == n - 1)
    def _():
        for back in range(DISPATCH_BUFS - 1):
            @pl.when(i - back >= 0)
            def _():
                wait_scatter(lax.rem(i - back + DISPATCH_BUFS, DISPATCH_BUFS))


def _moe_dispatch(h_flat, pos, n_rows):
    t, d = h_flat.shape
    n_tiles = t // TS
    return pl.pallas_call(
        _dispatch_kernel,
        grid=(n_tiles,),
        in_specs=[
            pl.BlockSpec((1, 1, TS), lambda i: (i, 0, 0), memory_space=pltpu.SMEM),
            pl.BlockSpec(memory_space=pl.ANY),
            pl.BlockSpec(memory_space=pl.ANY),
        ],
        out_specs=pl.BlockSpec(memory_space=pl.ANY),
        out_shape=jax.ShapeDtypeStruct((n_rows, d), F32),
        scratch_shapes=[pltpu.VMEM((DISPATCH_BUFS, TS, d), F32),
                        pltpu.SemaphoreType.DMA((DISPATCH_BUFS,)),
                        pltpu.SemaphoreType.DMA((DISPATCH_BUFS,))],
        input_output_aliases={2: 0},
        compiler_params=_cparams(1),
        name="moe_dispatch",
    )(pos, h_flat, jnp.zeros((n_rows, d), F32))


def _ffn_kernel(tg_ref, tv_ref, xs_ref, rw_ref, rb_ref, wg_ref, wu_ref, wd_ref, o_ref):
    del tg_ref
    i = pl.program_id(0)

    @pl.when(tv_ref[i] == 0)
    def _():
        o_ref[...] = jnp.zeros_like(o_ref)

    @pl.when(tv_ref[i] != 0)
    def _():
        xb = xs_ref[...].astype(BF16)
        logits = jnp.dot(xb, rw_ref[0], preferred_element_type=F32)
        sc = jax.nn.sigmoid(logits)
        bs = sc + rb_ref[0]
        s_col = [sc[:, j:j + 1] for j in range(EXPERTS_PER_GROUP)]
        b_col = [bs[:, j:j + 1] for j in range(EXPERTS_PER_GROUP)]
        sel = []
        for j in range(EXPERTS_PER_GROUP):
            beaten = jnp.zeros_like(b_col[j])
            for k in range(EXPERTS_PER_GROUP):
                if k == j:
                    continue
                wins = (b_col[k] >= b_col[j]) if k < j else (b_col[k] > b_col[j])
                beaten = beaten + jnp.where(wins, 1.0, 0.0)
            sel.append(beaten < 2.0)
        den = sum(jnp.where(sel[j], s_col[j], 0.0) for j in range(EXPERTS_PER_GROUP))
        hes = []
        for j in range(EXPERTS_PER_GROUP):
            cj = jnp.where(sel[j], s_col[j] / den, 0.0)
            gj = jnp.dot(xb, wg_ref[0, j], preferred_element_type=F32)
            uj = jnp.dot(xb, wu_ref[0, j], preferred_element_type=F32)
            hes.append((_silu(gj) * uj * cj).astype(BF16))
        he = jnp.concatenate(hes, axis=-1)
        d_e = wd_ref.shape[2]
        wd = wd_ref[0].reshape(EXPERTS_PER_GROUP * d_e, wd_ref.shape[3])
        o_ref[...] = jnp.dot(he, wd, preferred_element_type=F32)


def _moe_ffn(xs, tile_group, tile_valid, rw_g, rb_g, w_gate, w_up, w_down):
    n_rows, d = xs.shape
    d_e = w_gate.shape[-1]
    n_tiles = n_rows // TMG
    epg = EXPERTS_PER_GROUP
    grid_spec = pltpu.PrefetchScalarGridSpec(
        num_scalar_prefetch=2,
        grid=(n_tiles,),
        in_specs=[
            pl.BlockSpec((TMG, d), lambda i, tg, tv: (i, 0)),
            pl.BlockSpec((1, d, LANES), lambda i, tg, tv: (tg[i], 0, 0)),
            pl.BlockSpec((1, 1, LANES), lambda i, tg, tv: (tg[i], 0, 0)),
            pl.BlockSpec((1, epg, d, d_e), lambda i, tg, tv: (0, tg[i], 0, 0)),
            pl.BlockSpec((1, epg, d, d_e), lambda i, tg, tv: (0, tg[i], 0, 0)),
            pl.BlockSpec((1, epg, d_e, d), lambda i, tg, tv: (0, tg[i], 0, 0)),
        ],
        out_specs=pl.BlockSpec((TMG, d), lambda i, tg, tv: (i, 0)),
    )
    return pl.pallas_call(
        _ffn_kernel,
        grid_spec=grid_spec,
        out_shape=jax.ShapeDtypeStruct((n_rows, d), F32),
        compiler_params=_cparams(1),
        name="moe_ffn",
    )(tile_group, tile_valid, xs, rw_g, rb_g, w_gate[None], w_up[None], w_down[None])


def _combine_kernel(pos_ref, posn_ref, x_ref, mod_ref, ys_ref, o_ref, buf, sem):
    i = pl.program_id(0) * pl.num_programs(1) + pl.program_id(1)
    n = pl.num_programs(0) * pl.num_programs(1)
    slot = lax.rem(i, 2)

    def gather(p_ref, which):
        def issue(r, carry):
            p = p_ref[0, 0, r]
            pltpu.make_async_copy(ys_ref.at[pl.ds(p, 1)], buf.at[which, pl.ds(r, 1)],
                                  sem.at[which]).start()
            return carry
        lax.fori_loop(0, TS, issue, 0, unroll=8)

    @pl.when(i == 0)
    def _():
        gather(pos_ref, slot)

    @pl.when(i + 1 < n)
    def _():
        gather(posn_ref, 1 - slot)

    pltpu.make_async_copy(ys_ref.at[pl.ds(0, TS)], buf.at[slot], sem.at[slot]).wait()
    gate = mod_ref[0, 0][5:6]
    o_ref[0] = x_ref[0] + gate * buf[slot]


def _moe_combine(x, modtab, pos, ys, kind_of_tile, tile_off, n_out_tiles):
    b, s_tot, d = x.shape
    nt = s_tot // TS
    n_tiles = b * nt
    def cur(bi, si):
        return (bi * nt + si + tile_off, 0, 0)
    def nxt(bi, si):
        last = si == n_out_tiles - 1
        nb = jnp.where(last, bi + 1, bi)
        ns = jnp.where(last, 0, si + 1)
        return (jnp.minimum(nb * nt + ns + tile_off, n_tiles - 1), 0, 0)
    return pl.pallas_call(
        _combine_kernel,
        grid=(b, n_out_tiles),
        in_specs=[
            pl.BlockSpec((1, 1, TS), cur, memory_space=pltpu.SMEM),
            pl.BlockSpec((1, 1, TS), nxt, memory_space=pltpu.SMEM),
            pl.BlockSpec((1, TS, d), lambda bi, si: (bi, si + tile_off, 0)),
            pl.BlockSpec((1, 1, N_MOD, d), lambda bi, si: (bi, kind_of_tile(si + tile_off), 0, 0)),
            pl.BlockSpec(memory_space=pl.ANY),
        ],
        out_specs=pl.BlockSpec((1, TS, d), lambda bi, si: (bi, si, 0)),
        out_shape=jax.ShapeDtypeStruct((b, n_out_tiles * TS, d), F32),
        scratch_shapes=[pltpu.VMEM((2, TS, d), F32), pltpu.SemaphoreType.DMA((2,))],
        compiler_params=_cparams(2),
        name="moe_combine",
    )(pos, pos, x, modtab, ys)


def _moe_layer(x, modtab, norm_g, router_w, router_b, w_gate, w_up, w_down, kind_of_tile,
               tile_off, n_out_tiles):
    b, s_tot, d = x.shape
    t = b * s_tot
    h, grp, rank, cnt = _moe_route(x, modtab, norm_g, router_w, router_b, kind_of_tile)
    counts = cnt[:, 0].astype(I32)
    padded = ((counts + TMG - 1) // TMG) * TMG
    ends = jnp.cumsum(padded)
    starts = ends - padded
    pos = starts[grp] + rank
    n_rows = t + N_GROUPS * TMG
    tile_row0 = jnp.arange(n_rows // TMG, dtype=I32) * TMG
    tile_group = jnp.minimum(jnp.sum(tile_row0[:, None] >= ends[None, :], axis=1),
                             N_GROUPS - 1).astype(I32)
    tile_valid = (tile_row0 < ends[-1]).astype(I32)
    xs = _moe_dispatch(h.reshape(t, d), pos, n_rows)
    epg = EXPERTS_PER_GROUP
    rw_g = jnp.pad(router_w.reshape(d, N_GROUPS, epg).transpose(1, 0, 2),
                   ((0, 0), (0, 0), (0, LANES - epg))).astype(BF16)
    rb_g = jnp.pad(router_b.reshape(N_GROUPS, 1, epg), ((0, 0), (0, 0), (0, LANES - epg)))
    ys = _moe_ffn(xs, tile_group, tile_valid, rw_g, rb_g, w_gate.astype(BF16), w_up.astype(BF16),
                  w_down.astype(BF16))
    return _moe_combine(x, modtab, pos, ys, kind_of_tile, tile_off, n_out_tiles)


def _rope_pad(w, axis):
    half = QK_ROPE // 2
    x1, x2 = jnp.split(w, 2, axis=axis)
    z = jnp.zeros_like(x1)
    del half
    return jnp.concatenate([x1, z, x2, z], axis=axis)


def _oddproj_kernel(x_ref, mod_ref, g_ref, win_ref, gcq_ref, wuq_ref, gckv_ref, wukv_ref, gqm_ref,
                    gkm_ref, gqd_ref, gkd_ref, cs_ref, sna_ref, snb_ref, csm_ref, snm_ref,
                    qm_ref, km_ref, vm_ref, qd_ref, kd_ref, vd_ref):
    mod = mod_ref[0, 0]
    h = _norm_mod(x_ref[0], g_ref[...], mod[0:1], mod[1:2]).astype(BF16)
    q_lora = gcq_ref.shape[1]
    kv_lora = gckv_ref.shape[1]
    n_qd = DIFF_HEADS * 2 * DIFF_QK
    o_qd = q_lora
    o_ckv = o_qd + n_qd
    o_kr = o_ckv + kv_lora
    o_kd = o_kr + LANES
    o_vd = o_kd + n_qd
    lane = lax.broadcasted_iota(I32, (1, LANES), 1)
    low = lane < DIFF_QK
    d_mla = QK_NOPE + QK_ROPE

    def rms(x, g, n):
        return x * lax.rsqrt(jnp.sum(x * x, axis=-1, keepdims=True) / n + EPS) * g

    def rope_mla(y):
        return y * csm_ref[...] + pltpu.roll(y, LANES // 2, 1) * snm_ref[...]

    def diff_cols(p, g):
        sq = p * p
        s_all = jnp.sum(sq, axis=-1, keepdims=True)
        s_lo = jnp.sum(jnp.where(low, sq, 0.0), axis=-1, keepdims=True)
        r = jnp.where(low, lax.rsqrt(s_lo / DIFF_QK + EPS),
                      lax.rsqrt((s_all - s_lo) / DIFF_QK + EPS))
        y = p * r * g
        return (y * cs_ref[...] + pltpu.roll(y, LANES - DIFF_QK // 2, 1) * sna_ref[...]
                + pltpu.roll(y, DIFF_QK // 2, 1) * snb_ref[...])

    cq = rms(jnp.dot(h, win_ref[:, 0:q_lora], preferred_element_type=F32), gcq_ref[...], q_lora)
    qm = jnp.dot(cq.astype(BF16), wuq_ref[...], preferred_element_type=F32)
    for hd in range(MLA_HEADS):
        qh = rms(qm[:, 2 * LANES * hd:2 * LANES * (hd + 1)], gqm_ref[...], d_mla)
        qm_ref[0, hd] = jnp.concatenate([qh[:, :LANES], rope_mla(qh[:, LANES:])], axis=-1).astype(BF16)
    pq = jnp.dot(h, win_ref[:, o_qd:o_qd + n_qd], preferred_element_type=F32)
    for hd in range(DIFF_HEADS):
        y = diff_cols(pq[:, LANES * hd:LANES * (hd + 1)], gqd_ref[...])
        qd_ref[0, 2 * hd] = jnp.where(low, y, 0.0).astype(BF16)
        qd_ref[0, 2 * hd + 1] = jnp.where(low, 0.0, y).astype(BF16)

    pkv = jnp.dot(h, win_ref[:, o_ckv:o_kr + LANES], preferred_element_type=F32)
    ckv = rms(pkv[:, :kv_lora], gckv_ref[...], kv_lora)
    kr = pkv[:, kv_lora:]
    kv = jnp.dot(ckv.astype(BF16), wukv_ref[...], preferred_element_type=F32)
    for hd in range(MLA_HEADS):
        kcat = jnp.concatenate([kv[:, 2 * LANES * hd:2 * LANES * hd + LANES], kr], axis=-1)
        kh = rms(kcat, gkm_ref[...], d_mla)
        km_ref[0, hd] = jnp.concatenate([kh[:, :LANES], rope_mla(kh[:, LANES:])], axis=-1).astype(BF16)
        vm_ref[0, hd] = kv[:, 2 * LANES * hd + LANES:2 * LANES * (hd + 1)].T.astype(BF16)
    pk = jnp.dot(h, win_ref[:, o_kd:o_kd + n_qd], preferred_element_type=F32)
    pv = jnp.dot(h, win_ref[:, o_vd:o_vd + DIFF_HEADS * DIFF_V], preferred_element_type=F32)
    for hd in range(DIFF_HEADS):
        kd_ref[0, hd] = diff_cols(pk[:, LANES * hd:LANES * (hd + 1)], gkd_ref[...]).astype(BF16)
        vd_ref[0, hd] = pv[:, DIFF_V * hd:DIFF_V * (hd + 1)].T.astype(BF16)


def _odd_project(xa, modtab, norm_g, w_in, g_cq, w_uq, g_ckv, w_ukv, g_mla, g_diff, n_ctx):
    b, s_tot, d = xa.shape
    nt = s_tot // TS
    n_lat = s_tot - n_ctx
    q_lora = g_cq.shape[0]
    kv_lora = g_ckv.shape[0]
    n_qd = DIFF_HEADS * 2 * DIFF_QK
    d_mla = QK_NOPE + QK_ROPE
    o = 0
    w_cq = w_in[:, o:o + q_lora]; o += q_lora
    w_qd = w_in[:, o:o + n_qd]; o += n_qd
    w_ckv = w_in[:, o:o + kv_lora]; o += kv_lora
    w_kr = w_in[:, o:o + QK_ROPE]; o += QK_ROPE
    w_kd = w_in[:, o:o + n_qd]; o += n_qd
    w_vd = w_in[:, o:]
    w_in_p = jnp.concatenate([w_cq, w_qd, w_ckv, _rope_pad(w_kr, 1), w_kd, w_vd], axis=1).astype(BF16)
    wuq = w_uq.reshape(q_lora, MLA_HEADS, d_mla)
    wuq_p = jnp.concatenate([wuq[..., :QK_NOPE], _rope_pad(wuq[..., QK_NOPE:], 2)],
                            axis=-1).reshape(q_lora, MLA_HEADS * 2 * LANES).astype(BF16)

    def pad_gain(g):
        return jnp.concatenate([g[:QK_NOPE], _rope_pad(g[QK_NOPE:], 0)]).reshape(1, 2 * LANES)

    rows = n_lat // GRID_W
    row = jnp.repeat(jnp.arange(rows, dtype=F32), GRID_W)
    col = jnp.tile(jnp.arange(GRID_W, dtype=F32), rows)
    axis_dim = QK_ROPE // 2
    inv_freq = ROPE_BASE ** (-jnp.arange(0, axis_dim, 2, dtype=F32) / axis_dim)
    ang = jnp.concatenate([row[:, None] * inv_freq, col[:, None] * inv_freq], axis=-1)
    half = QK_ROPE // 2
    cos = jnp.concatenate([jnp.ones((n_ctx, half), F32), jnp.cos(ang)], axis=0)
    sin = jnp.concatenate([jnp.zeros((n_ctx, half), F32), jnp.sin(ang)], axis=0)
    z = jnp.zeros_like(sin)
    cs_d = jnp.tile(cos, (1, LANES // half))
    sna_d = jnp.tile(jnp.concatenate([-sin, z], axis=1), (1, LANES // QK_ROPE))
    snb_d = jnp.tile(jnp.concatenate([z, sin], axis=1), (1, LANES // QK_ROPE))
    cs_m = jnp.concatenate([cos, z, cos, z], axis=1)
    sn_m = jnp.concatenate([-sin, z, sin, z], axis=1)

    const2 = lambda bi, si: (0, 0)
    tok = lambda bi, si: (si, 0)
    kv4 = lambda bi, si: (bi, 0, si, 0)
    vt4 = lambda bi, si: (bi, 0, 0, si)
    q4 = lambda bi, si: (bi, 0, jnp.maximum(si - n_ctx // TS, 0), 0)
    full = lambda a: pl.BlockSpec(a.shape, const2)
    gq_d = jnp.tile(g_diff[0], LANES // DIFF_QK).reshape(1, LANES)
    gk_d = jnp.tile(g_diff[1], LANES // DIFF_QK).reshape(1, LANES)
    args = [xa, modtab, norm_g.reshape(1, d), w_in_p, g_cq.reshape(1, -1), wuq_p, g_ckv.reshape(1, -1),
            w_ukv.astype(BF16), pad_gain(g_mla[0]), pad_gain(g_mla[1]), gq_d, gk_d,
            cs_d, sna_d, snb_d, cs_m, sn_m]
    in_specs = [
        pl.BlockSpec((1, TS, d), lambda bi, si: (bi, si, 0)),
        pl.BlockSpec((1, 1, N_MOD, d), lambda bi, si: (bi, jnp.minimum(si, 1), 0, 0)),
    ] + [full(a) for a in args[2:12]] + [pl.BlockSpec((TS, LANES), tok)] * 5
    hm, hd = MLA_HEADS, DIFF_HEADS
    return pl.pallas_call(
        _oddproj_kernel,
        grid=(b, nt),
        in_specs=in_specs,
        out_specs=[
            pl.BlockSpec((1, hm, TS, 2 * LANES), q4),
            pl.BlockSpec((1, hm, TS, 2 * LANES), kv4),
            pl.BlockSpec((1, hm, V_HEAD, TS), vt4),
            pl.BlockSpec((1, 2 * hd, TS, LANES), q4),
            pl.BlockSpec((1, hd, TS, LANES), kv4),
            pl.BlockSpec((1, hd, DIFF_V, TS), vt4),
        ],
        out_shape=[
            jax.ShapeDtypeStruct((b, hm, n_lat, 2 * LANES), BF16),
            jax.ShapeDtypeStruct((b, hm, s_tot, 2 * LANES), BF16),
            jax.ShapeDtypeStruct((b, hm, V_HEAD, s_tot), BF16),
            jax.ShapeDtypeStruct((b, 2 * hd, n_lat, LANES), BF16),
            jax.ShapeDtypeStruct((b, hd, s_tot, LANES), BF16),
            jax.ShapeDtypeStruct((b, hd, DIFF_V, s_tot), BF16),
        ],
        compiler_params=_cparams(2),
        name="odd_project",
    )(*args)


def _attn_kernel(q_ref, k_ref, vt_ref, o_ref, s_s, p_s, *, scale, rep):
    hq = q_ref.shape[1]
    n_chunks = k_ref.shape[2] // TK

    def pair(j, carry):
        heads = (2 * j, 2 * j + 1)
        row_max = []
        for u, h in enumerate(heads):
            qh = q_ref[0, h]
            m = jnp.full((1, TQ), -jnp.inf, F32)
            for c in range(n_chunks):
                k = k_ref[0, h // rep, pl.ds(c * TK, TK), :]
                st = lax.dot_general(k, qh, (((1,), (1,)), ((), ())),
                                     preferred_element_type=F32) * scale
                s_s[u, pl.ds(c * TK, TK), :] = st
                m = jnp.maximum(m, jnp.max(st, axis=0, keepdims=True))
            row_max.append(m)
        for u, h in enumerate(heads):
            l = jnp.zeros((1, TQ), F32)
            for c in range(n_chunks):
                p = jnp.exp(s_s[u, pl.ds(c * TK, TK), :] - row_max[u])
                l = l + jnp.sum(p, axis=0, keepdims=True)
                p_s[u, pl.ds(c * TK, TK), :] = p.astype(BF16)
            out_t = jnp.dot(vt_ref[0, h // rep], p_s[u], preferred_element_type=F32)
            o_ref[0, h] = (out_t / l).T.astype(o_ref.dtype)
        return carry

    lax.fori_loop(0, hq // 2, pair, 0)


def _attention(q, k, vt, scale, out_dtype, name):
    b, hq, lq, dk = q.shape
    hk, lk = k.shape[1], k.shape[2]
    dv = vt.shape[2]
    assert lk % TK == 0 and lq % TQ == 0 and hq % 2 == 0
    once = pl.Buffered(1)
    return pl.pallas_call(
        functools.partial(_attn_kernel, scale=scale, rep=hq // hk),
        grid=(b, lq // TQ),
        in_specs=[
            pl.BlockSpec((1, hq, TQ, dk), lambda bi, qi: (bi, 0, qi, 0)),
            pl.BlockSpec((1, hk, lk, dk), lambda bi, qi: (bi, 0, 0, 0), pipeline_mode=once),
            pl.BlockSpec((1, hk, dv, lk), lambda bi, qi: (bi, 0, 0, 0), pipeline_mode=once),
        ],
        out_specs=pl.BlockSpec((1, hq, TQ, dv), lambda bi, qi: (bi, 0, qi, 0)),
        out_shape=jax.ShapeDtypeStruct((b, hq, lq, dv), out_dtype),
        scratch_shapes=[pltpu.VMEM((2, lk, TQ), F32), pltpu.VMEM((2, lk, TQ), BF16)],
        compiler_params=_cparams(2),
        name=name,
    )(q, k, vt)


def _oddout_kernel(x_ref, mod_ref, om_ref, ad_ref, dl_ref, gs_ref, wout_ref, o_ref, *, lam_init):
    dl = dl_ref[...]
    lam = (jnp.exp(jnp.sum(dl[0:1] * dl[1:2], axis=-1, keepdims=True))
           - jnp.exp(jnp.sum(dl[2:3] * dl[3:4], axis=-1, keepdims=True)) + lam_init)
    parts = [om_ref[0, hd] for hd in range(MLA_HEADS)]
    for hd in range(DIFF_HEADS):
        df = ad_ref[0, 2 * hd] - lam * ad_ref[0, 2 * hd + 1]
        od = df * lax.rsqrt(jnp.mean(df * df, axis=-1, keepdims=True) + EPS) * gs_ref[...]
        parts.append((od * (1.0 - lam_init)).astype(BF16))
    y = jnp.dot(jnp.concatenate(parts, axis=-1), wout_ref[...], preferred_element_type=F32)
    o_ref[0] = x_ref[0] + mod_ref[0, 0][2:3] * y


def _odd_output(xa, modtab, o_m, a_d, diff_lambda, g_sub, w_out, lam_init, n_ctx):
    b, s_tot, d = xa.shape
    n_lat = s_tot - n_ctx
    off = n_ctx // TS
    const2 = lambda bi, si: (0, 0)
    return pl.pallas_call(
        functools.partial(_oddout_kernel, lam_init=lam_init),
        grid=(b, n_lat // TS),
        in_specs=[
            pl.BlockSpec((1, TS, d), lambda bi, si: (bi, si + off, 0)),
            pl.BlockSpec((1, 1, N_MOD, d), lambda bi, si: (bi, 1, 0, 0)),
            pl.BlockSpec((1, o_m.shape[1], TS, V_HEAD), lambda bi, si: (bi, 0, si, 0)),
            pl.BlockSpec((1, a_d.shape[1], TS, DIFF_V), lambda bi, si: (bi, 0, si, 0)),
            pl.BlockSpec(diff_lambda.shape, const2),
            pl.BlockSpec((1, DIFF_V), const2),
            pl.BlockSpec(w_out.shape, const2),
        ],
        out_specs=pl.BlockSpec((1, TS, d), lambda bi, si: (bi, si, 0)),
        out_shape=jax.ShapeDtypeStruct((b, n_lat, d), F32),
        compiler_params=_cparams(2),
        name="odd_output",
    )(xa, modtab, o_m, a_d, diff_lambda, g_sub.reshape(1, -1), w_out.astype(BF16))


def kernel(x, c, ctx, c_ctx, mod_w, mod_b, norm_g, even_w_in, conv_dw_w, conv_dw_b, conv_ln_g,
           conv_ln_b, pool_w, pool_scale, even_w_out, odd_w_in, mla_g_cq, mla_w_uq, mla_g_ckv,
           mla_w_ukv, qk_g_mla, qk_g_diff, diff_lambda, diff_sub_g, odd_w_out, router_w, router_b,
           moe_w_gate, moe_w_up, moe_w_down):
    b, n_lat, d = x.shape
    n_ctx = ctx.shape[1]
    assert n_ctx == TS and n_lat % TS == 0 and mod_w.shape[0] == 2
    nt = (n_ctx + n_lat) // TS

    c_rows = jnp.concatenate([c, c_ctx[None], jnp.zeros((16 - b - 1, d), F32)], axis=0)
    mods = _modulation(c_rows, mod_w, mod_b)
    modtabs = []
    for i in range(2):
        mod_l = mods[i, :b].reshape(b, N_MOD, d)
        mod_c = jnp.broadcast_to(mods[i, b].reshape(1, N_MOD, d), (b, N_MOD, d))
        modtabs.append(jnp.stack([mod_c, mod_l], axis=1))

    xa = jnp.concatenate([ctx, x], axis=1)
    ctx_or_lat = lambda si: jnp.minimum(si, 1)
    lat_only = lambda si: 1

    xa = _even_layer(xa, modtabs[0], norm_g[0, 0], even_w_in[0], conv_dw_w[0], conv_dw_b[0],
                     conv_ln_g[0], conv_ln_b[0], pool_w[0], pool_scale[0], even_w_out[0], n_lat)
    xa = _moe_layer(xa, modtabs[0], norm_g[0, 1], router_w, router_b, moe_w_gate[0], moe_w_up[0],
                    moe_w_down[0], ctx_or_lat, 0, nt)

    qm, km, vm, qd, kd, vd = _odd_project(xa, modtabs[1], norm_g[1, 0], odd_w_in[0], mla_g_cq[0],
                                          mla_w_uq[0], mla_g_ckv[0], mla_w_ukv[0], qk_g_mla[0],
                                          qk_g_diff[0], n_ctx)
    o_m = _attention(qm, km, vm, MLA_SCALE, BF16, "attn_mla")
    a_d = _attention(qd, kd, vd, DIFF_SCALE, F32, "attn_diff")
    lam_init = 0.8 - 0.6 * math.exp(-0.3 * 1)
    xl = _odd_output(xa, modtabs[1], o_m, a_d, diff_lambda[0], diff_sub_g[0], odd_w_out[0],
                     lam_init, n_ctx)
    return _moe_layer(xl, modtabs[1], norm_g[1, 1], router_w, router_b, moe_w_gate[1], moe_w_up[1],
                      moe_w_down[1], lat_only, 0, n_lat // TS)
```

```python
import functools
import math

import jax
import jax.numpy as jnp
from jax import lax
from jax.experimental import pallas as pl
from jax.experimental.pallas import tpu as pltpu

F32 = jnp.float32
BF16 = jnp.bfloat16
I32 = jnp.int32

GRID_W = 64
N_MOD = 6
EPS = 1e-6
CONV_WIDTH = 31
POOL_WINDOWS = (2, 4, 8, 16)
MLA_HEADS = 4
QK_NOPE = 128
QK_ROPE = 64
V_HEAD = 128
DIFF_HEADS = 4
DIFF_QK = 64
DIFF_V = 128
MLA_SCALE = (QK_NOPE + QK_ROPE) ** -0.5
DIFF_SCALE = DIFF_QK ** -0.5
ROPE_BASE = 10000.0
LOG2E = math.log2(math.e)
N_EXPERTS = 32
N_GROUPS = 8
EXPERTS_PER_GROUP = N_EXPERTS // N_GROUPS

LANES = 128
SUBLANES = 8
TS = 256
HALO = 16
TMG = 256
TQ = 256
TK = 256
DISPATCH_BUFS = 3
VMEM_LIMIT = 48 * 1024 * 1024


def _cparams(n_axes):
    return pltpu.CompilerParams(
        dimension_semantics=("arbitrary",) * n_axes, vmem_limit_bytes=VMEM_LIMIT)


def _norm_mod(x, g, shift, scale):
    ms = jnp.mean(x * x, axis=-1, keepdims=True)
    return (x * lax.rsqrt(ms + EPS) * g) * (1.0 + scale) + shift


def _silu(x):
    return x * jax.nn.sigmoid(x)


def _mod_kernel(c_ref, w_ref, b_ref, o_ref):
    a = _silu(c_ref[...])
    o_ref[0] = jnp.dot(a.astype(BF16), w_ref[0].astype(BF16),
                       preferred_element_type=F32) + b_ref[0]


def _modulation(c_rows, mod_w, mod_b):
    depth, d, n = mod_w.shape
    tn = 1536
    rows = c_rows.shape[0]
    return pl.pallas_call(
        _mod_kernel,
        grid=(depth, n // tn),
        in_specs=[
            pl.BlockSpec((rows, d), lambda l, j: (0, 0)),
            pl.BlockSpec((1, d, tn), lambda l, j: (l, 0, j)),
            pl.BlockSpec((1, 1, tn), lambda l, j: (l, 0, j)),
        ],
        out_specs=pl.BlockSpec((1, rows, tn), lambda l, j: (l, 0, j)),
        out_shape=jax.ShapeDtypeStruct((depth, rows, n), F32),
        compiler_params=_cparams(2),
        name="modulation",
    )(c_rows, mod_w, mod_b.reshape(depth, 1, n))


def _even_kernel(xm_ref, xp_ref, xn_ref, mod_ref, g_ref, win_ref, dww_ref, dwb_ref, lng_ref,
                 lnb_ref, pw_ref, ps_ref, wout_ref, g2_ref, rwt_ref, rb_ref, tri_ref,
                 o_ref, h2_ref, grp_ref, rank_ref, cnt_ref,
                 h_s, glu_s, pool_s, pres_s, mix_s, sc_s, carry_s, *, n_lat):
    s = pl.program_id(1)
    nt = pl.num_programs(1)
    d_conv = glu_s.shape[2]
    mod = mod_ref[0, 0]
    shift, scale, gate = mod[0:1], mod[1:2], mod[2:3]
    g = g_ref[...]
    xm = xm_ref[0]
    h_s[0:HALO, :] = _norm_mod(xp_ref[0], g, shift, scale).astype(BF16)
    h_s[HALO:HALO + TS, :] = _norm_mod(xm, g, shift, scale).astype(BF16)
    h_s[HALO + TS:, :] = _norm_mod(xn_ref[0], g, shift, scale).astype(BF16)

    rows = HALO + TS + HALO
    ridx = lax.broadcasted_iota(I32, (rows, 1), 0)
    prev_ok = s >= 2
    next_ok = jnp.logical_and(s >= 1, s <= nt - 2)
    valid = jnp.logical_and(jnp.logical_or(ridx >= HALO, prev_ok),
                            jnp.logical_or(ridx < HALO + TS, next_ok))

    h = h_s[...]
    a = jnp.dot(h, win_ref[:, 0:d_conv], preferred_element_type=F32)
    gt = jnp.dot(h, win_ref[:, d_conv:2 * d_conv], preferred_element_type=F32)
    glu_s[0] = jnp.where(valid, a * jax.nn.sigmoid(gt), 0.0)
    for j in range(1, SUBLANES):
        glu_s[j, 0:rows - SUBLANES, :] = glu_s[0, pl.ds(j, rows - SUBLANES), :]
    pu = jnp.dot(h, win_ref[:, 2 * d_conv:], preferred_element_type=F32)
    pool_s[...] = jnp.where(valid, pu, 0.0)

    seq_pos0 = jnp.where(s == 0, 0, (s - 1) * TS)
    seq_len = jnp.where(s == 0, TS, n_lat)
    half = CONV_WIDTH // 2
    rc_rows = 64
    n_cc = d_conv // LANES
    for rc in range(TS // rc_rows):
        r0 = HALO + rc * rc_rows
        ys = []
        for cc in range(n_cc):
            cs = slice(cc * LANES, (cc + 1) * LANES)
            acc = jnp.zeros((rc_rows, LANES), F32) + dwb_ref[:, cs]
            for k in range(CONV_WIDTH):
                sh = (r0 + k - half) % SUBLANES
                acc = acc + dww_ref[k:k + 1, cs] * glu_s[sh, pl.ds(r0 + k - half - sh, rc_rows), cs]
            ys.append(acc)
        mu = sum(jnp.sum(y, axis=-1, keepdims=True) for y in ys) / d_conv
        var = sum(jnp.sum((y - mu) * (y - mu), axis=-1, keepdims=True) for y in ys) / d_conv
        rs = lax.rsqrt(var + EPS)
        for cc in range(n_cc):
            cs = slice(cc * LANES, (cc + 1) * LANES)
            z = (ys[cc] - mu) * rs * lng_ref[:, cs] + lnb_ref[:, cs]
            mix_s[rc * rc_rows:(rc + 1) * rc_rows, cs] = _silu(z).astype(BF16)
        pos = seq_pos0 + rc * rc_rows + lax.broadcasted_iota(I32, (rc_rows, 1), 0)
        for gi, w in enumerate(POOL_WINDOWS):
            cs = slice(gi * LANES, (gi + 1) * LANES)
            ssum = pool_s[pl.ds(r0 - w // 2, rc_rows), cs]
            for j in range(1 - w // 2, w // 2):
                ssum = ssum + pool_s[pl.ds(r0 + j, rc_rows), cs]
            lo = jnp.clip(pos - w // 2, 0, seq_len)
            hi = jnp.clip(pos + w // 2, 0, seq_len)
            cnt = (hi - lo).astype(F32)
            res = ssum / cnt - pool_s[pl.ds(r0, rc_rows), cs]
            pres_s[rc * rc_rows:(rc + 1) * rc_rows, cs] = res.astype(BF16)

    yp = jnp.dot(pres_s[...], pw_ref[...], preferred_element_type=F32) * ps_ref[...]
    mix_s[:, d_conv:] = yp.astype(BF16)
    y = jnp.dot(mix_s[...], wout_ref[...], preferred_element_type=F32)
    x_new = xm + gate * y
    o_ref[0] = x_new
    _route_tile(x_new, mod, g2_ref, rwt_ref, rb_ref, tri_ref, h2_ref, grp_ref, rank_ref, cnt_ref,
                sc_s, carry_s)


def _even_layer(xa, modtab, norm_g, w_in, dw_w, dw_b, ln_g, ln_b, pool_w, pool_scale, w_out, n_lat,
                route):
    b, s_tot, d = xa.shape
    nt = s_tot // TS
    r_args, r_in_specs, r_out_specs, r_out_shape, r_scratch = _route_plumbing(b, nt, d, *route)
    d_conv = dw_w.shape[1]
    d_pool = pool_scale.shape[0]
    assert d_pool == len(POOL_WINDOWS) * LANES and pool_w.shape[1] == LANES
    pool_bd = jnp.zeros((d_pool, d_pool), F32)
    for gi in range(len(POOL_WINDOWS)):
        pool_bd = pool_bd.at[gi * LANES:(gi + 1) * LANES, gi * LANES:(gi + 1) * LANES].set(pool_w[gi])
    hb = TS // HALO
    n_hblk = s_tot // HALO
    const2 = lambda bi, si: (0, 0)
    return pl.pallas_call(
        functools.partial(_even_kernel, n_lat=n_lat),
        grid=(b, nt),
        in_specs=[
            pl.BlockSpec((1, TS, d), lambda bi, si: (bi, si, 0)),
            pl.BlockSpec((1, HALO, d), lambda bi, si: (bi, jnp.maximum(si * hb - 1, 0), 0)),
            pl.BlockSpec((1, HALO, d), lambda bi, si: (bi, jnp.minimum((si + 1) * hb, n_hblk - 1), 0)),
            pl.BlockSpec((1, 1, N_MOD, d), lambda bi, si: (bi, jnp.minimum(si, 1), 0, 0)),
            pl.BlockSpec((1, d), const2),
            pl.BlockSpec(w_in.shape, const2),
            pl.BlockSpec(dw_w.shape, const2),
            pl.BlockSpec((1, d_conv), const2),
            pl.BlockSpec((1, d_conv), const2),
            pl.BlockSpec((1, d_conv), const2),
            pl.BlockSpec((d_pool, d_pool), const2),
            pl.BlockSpec((1, d_pool), const2),
            pl.BlockSpec(w_out.shape, const2),
        ] + r_in_specs,
        out_specs=[pl.BlockSpec((1, TS, d), lambda bi, si: (bi, si, 0))] + r_out_specs,
        out_shape=[jax.ShapeDtypeStruct(xa.shape, F32)] + r_out_shape,
        scratch_shapes=[
            pltpu.VMEM((TS + 2 * HALO, d), BF16),
            pltpu.VMEM((SUBLANES, TS + 2 * HALO, d_conv), F32),
            pltpu.VMEM((TS + 2 * HALO, d_pool), F32),
            pltpu.VMEM((TS, d_pool), BF16),
            pltpu.VMEM((TS, d_conv + d_pool), BF16),
        ] + r_scratch,
        compiler_params=_cparams(2),
        name="even_mixer",
    )(xa, xa, xa, modtab, norm_g.reshape(1, d), w_in.astype(BF16), dw_w, dw_b.reshape(1, -1),
      ln_g.reshape(1, -1), ln_b.reshape(1, -1), pool_bd.astype(BF16), pool_scale.reshape(1, -1),
      w_out.astype(BF16), *r_args)


def _route_tile(x, mod, g_ref, rwt_ref, rb_ref, tri_ref, h_ref, grp_ref, rank_ref, cnt_ref, sc_s,
                carry_s):
    first = jnp.logical_and(pl.program_id(0) == 0, pl.program_id(1) == 0)

    @pl.when(first)
    def _():
        carry_s[...] = jnp.zeros_like(carry_s)

    h = _norm_mod(x, g_ref[...], mod[3:4], mod[4:5])
    h_ref[0] = h
    logits = lax.dot_general(rwt_ref[...], h.astype(BF16), (((1,), (1,)), ((), ())),
                             preferred_element_type=F32)
    biased = jax.nn.sigmoid(logits) + rb_ref[...]
    n_half = TS // LANES
    for hh in range(n_half):
        sc_s[hh] = biased[:, hh * LANES:(hh + 1) * LANES]
    a, b, c, d = [
        jnp.concatenate([sc_s[hh, pl.ds(j, N_GROUPS, stride=EXPERTS_PER_GROUP), :]
                         for hh in range(n_half)], axis=-1)
        for j in range(EXPERTS_PER_GROUP)]
    top2 = jnp.maximum(jnp.maximum(jnp.maximum(a + b, a + c), jnp.maximum(a + d, b + c)),
                       jnp.maximum(b + d, c + d))
    gmax = jnp.max(top2, axis=0, keepdims=True)
    gi = lax.broadcasted_iota(I32, top2.shape, 0)
    sel = jnp.min(jnp.where(top2 == gmax, gi, N_GROUPS), axis=0, keepdims=True)
    onehot = gi == sel
    ohf = jnp.where(onehot, 1.0, 0.0)
    prefix = jnp.dot(ohf.astype(BF16), tri_ref[...], preferred_element_type=F32)
    carry = carry_s[:, 0:1]
    rank = jnp.sum(jnp.where(onehot, carry + prefix - 1.0, 0.0), axis=0, keepdims=True)
    carry_s[...] = carry_s[...] + jnp.sum(ohf, axis=1, keepdims=True)
    grp_ref[0] = sel
    rank_ref[0] = rank.astype(I32)
    cnt_ref[...] = carry_s[...]


def _route_plumbing(b, nt, d, norm_g, router_w, router_b):
    const2 = lambda bi, si: (0, 0)
    flat3 = lambda bi, si: (bi * nt + si, 0, 0)
    tri = (jnp.arange(TS)[:, None] <= jnp.arange(TS)[None, :]).astype(BF16)
    args = [norm_g.reshape(1, d), router_w.T.astype(BF16), router_b.reshape(N_EXPERTS, 1), tri]
    in_specs = [pl.BlockSpec((1, d), const2), pl.BlockSpec((N_EXPERTS, d), const2),
                pl.BlockSpec((N_EXPERTS, 1), const2), pl.BlockSpec((TS, TS), const2)]
    out_specs = [pl.BlockSpec((1, TS, d), lambda bi, si: (bi, si, 0)),
                 pl.BlockSpec((1, 1, TS), flat3), pl.BlockSpec((1, 1, TS), flat3),
                 pl.BlockSpec((N_GROUPS, LANES), const2)]
    out_shape = [jax.ShapeDtypeStruct((b, nt * TS, d), F32),
                 jax.ShapeDtypeStruct((b * nt, 1, TS), I32),
                 jax.ShapeDtypeStruct((b * nt, 1, TS), I32),
                 jax.ShapeDtypeStruct((N_GROUPS, LANES), F32)]
    scratch = [pltpu.VMEM((TS // LANES, N_EXPERTS, LANES), F32), pltpu.VMEM((N_GROUPS, LANES), F32)]
    return args, in_specs, out_specs, out_shape, scratch


def _dispatch_kernel(zrow_ref, pos_ref, src_ref, dst_ref, buf, ld_sem, st_sem, z_sem):
    i = pl.program_id(0)
    n = pl.num_programs(0)
    slot = lax.rem(i, DISPATCH_BUFS)
    nxt = lax.rem(i + 1, DISPATCH_BUFS)

    def load(tile, which):
        return pltpu.make_async_copy(src_ref.at[pl.ds(tile * TS, TS)], buf.at[which],
                                     ld_sem.at[which])

    def wait_scatter(which):
        pltpu.make_async_copy(buf.at[which], dst_ref.at[pl.ds(0, TS)], st_sem.at[which]).wait()

    @pl.when(i == 0)
    def _():
        load(i, slot).start()
        zsrc = buf.at[DISPATCH_BUFS - 1]
        zsrc[...] = jnp.zeros(zsrc.shape, F32)
        for z in range(zrow_ref.shape[0]):
            @pl.when(zrow_ref[z] >= 0)
            def _():
                row = pl.multiple_of(zrow_ref[z], TS)
                pltpu.make_async_copy(zsrc, dst_ref.at[pl.ds(row, TS)], z_sem).start()
        for z in range(zrow_ref.shape[0]):
            @pl.when(zrow_ref[z] >= 0)
            def _():
                pltpu.make_async_copy(zsrc, dst_ref.at[pl.ds(0, TS)], z_sem).wait()

    load(i, slot).wait()

    @pl.when(i + 1 >= DISPATCH_BUFS)
    def _():
        wait_scatter(nxt)

    @pl.when(i + 1 < n)
    def _():
        load(i + 1, nxt).start()

    def issue(r8, carry):
        for j in range(SUBLANES):
            r = r8 * SUBLANES + j
            p = pos_ref[0, 0, r]
            pltpu.make_async_copy(buf.at[slot, pl.ds(r, 1)], dst_ref.at[pl.ds(p, 1)],
                                  st_sem.at[slot]).start(priority=j % 2)
        return carry

    lax.fori_loop(0, TS // SUBLANES, issue, 0)

    @pl.when(i == n - 1)
    def _():
        for back in range(DISPATCH_BUFS - 1):
            @pl.when(i - back >= 0)
            def _():
                wait_scatter(lax.rem(i - back + DISPATCH_BUFS, DISPATCH_BUFS))


def _moe_dispatch(h_flat, pos, zero_rows, n_rows):
    t, d = h_flat.shape
    n_tiles = t // TS
    assert TS == TMG and n_tiles >= DISPATCH_BUFS
    return pl.pallas_call(
        _dispatch_kernel,
        grid=(n_tiles,),
        in_specs=[
            pl.BlockSpec(memory_space=pltpu.SMEM),
            pl.BlockSpec((1, 1, TS), lambda i: (i, 0, 0), memory_space=pltpu.SMEM),
            pl.BlockSpec(memory_space=pl.ANY),
        ],
        out_specs=pl.BlockSpec(memory_space=pl.ANY),
        out_shape=jax.ShapeDtypeStruct((n_rows, d), F32),
        scratch_shapes=[pltpu.VMEM((DISPATCH_BUFS, TS, d), F32),
                        pltpu.SemaphoreType.DMA((DISPATCH_BUFS,)),
                        pltpu.SemaphoreType.DMA((DISPATCH_BUFS,)),
                        pltpu.SemaphoreType.DMA(())],
        compiler_params=_cparams(1),
        name="moe_dispatch",
    )(zero_rows, pos, h_flat)


def _ffn_kernel(tg_ref, tv_ref, xs_ref, rw_ref, rb_ref, wg32_ref, wu32_ref, wd32_ref, o_ref,
                wg_ref, wu_ref, wd_ref):
    i = pl.program_id(0)

    @pl.when(jnp.logical_or(i == 0, tg_ref[i] != tg_ref[jnp.maximum(i - 1, 0)]))
    def _():
        wg_ref[...] = wg32_ref[...].astype(BF16)
        wu_ref[...] = wu32_ref[...].astype(BF16)
        wd_ref[...] = wd32_ref[...].astype(BF16)

    @pl.when(tv_ref[i] == 0)
    def _():
        o_ref[...] = jnp.zeros_like(o_ref)

    @pl.when(tv_ref[i] != 0)
    def _():
        xb = xs_ref[...].astype(BF16)
        logits = jnp.dot(xb, rw_ref[0], preferred_element_type=F32)
        sc = jax.nn.sigmoid(logits)
        bs = sc + rb_ref[0]
        s_col = [sc[:, j:j + 1] for j in range(EXPERTS_PER_GROUP)]
        b_col = [bs[:, j:j + 1] for j in range(EXPERTS_PER_GROUP)]
        sel = []
        for j in range(EXPERTS_PER_GROUP):
            beaten = jnp.zeros_like(b_col[j])
            for k in range(EXPERTS_PER_GROUP):
                if k == j:
                    continue
                wins = (b_col[k] >= b_col[j]) if k < j else (b_col[k] > b_col[j])
                beaten = beaten + jnp.where(wins, 1.0, 0.0)
            sel.append(beaten < 2.0)
        den = sum(jnp.where(sel[j], s_col[j], 0.0) for j in range(EXPERTS_PER_GROUP))
        hes = []
        for j in range(EXPERTS_PER_GROUP):
            cj = jnp.where(sel[j], s_col[j] / den, 0.0)
            gj = jnp.dot(xb, wg_ref[0, j], preferred_element_type=F32)
            uj = jnp.dot(xb, wu_ref[0, j], preferred_element_type=F32)
            hes.append((_silu(gj) * uj * cj).astype(BF16))
        he = jnp.concatenate(hes, axis=-1)
        d_e = wd_ref.shape[2]
        wd = wd_ref[0].reshape(EXPERTS_PER_GROUP * d_e, wd_ref.shape[3])
        o_ref[...] = jnp.dot(he, wd, preferred_element_type=F32)


def _moe_ffn(xs, tile_group, tile_valid, rw_g, rb_g, w_gate, w_up, w_down, layer):
    n_rows, d = xs.shape
    d_e = w_gate.shape[-1]
    n_tiles = n_rows // TMG
    epg = EXPERTS_PER_GROUP
    grid_spec = pltpu.PrefetchScalarGridSpec(
        num_scalar_prefetch=2,
        grid=(n_tiles,),
        in_specs=[
            pl.BlockSpec((TMG, d), lambda i, tg, tv: (i, 0)),
            pl.BlockSpec((1, d, LANES), lambda i, tg, tv: (tg[i], 0, 0)),
            pl.BlockSpec((1, 1, LANES), lambda i, tg, tv: (tg[i], 0, 0)),
            pl.BlockSpec((1, epg, d, d_e), lambda i, tg, tv: (layer, tg[i], 0, 0)),
            pl.BlockSpec((1, epg, d, d_e), lambda i, tg, tv: (layer, tg[i], 0, 0)),
            pl.BlockSpec((1, epg, d_e, d), lambda i, tg, tv: (layer, tg[i], 0, 0)),
        ],
        out_specs=pl.BlockSpec((TMG, d), lambda i, tg, tv: (i, 0)),
        scratch_shapes=[pltpu.VMEM((1, epg, d, d_e), BF16), pltpu.VMEM((1, epg, d, d_e), BF16),
                        pltpu.VMEM((1, epg, d_e, d), BF16)],
    )
    return pl.pallas_call(
        _ffn_kernel,
        grid_spec=grid_spec,
        out_shape=jax.ShapeDtypeStruct((n_rows, d), F32),
        compiler_params=_cparams(1),
        name="moe_ffn",
    )(tile_group, tile_valid, xs, rw_g, rb_g, w_gate, w_up, w_down)


def _combine_kernel(pos_ref, posn_ref, x_ref, mod_ref, ys_ref, o_ref, buf, sem):
    i = pl.program_id(0) * pl.num_programs(1) + pl.program_id(1)
    n = pl.num_programs(0) * pl.num_programs(1)
    slot = lax.rem(i, 2)

    def gather(p_ref, which):
        def issue(r8, carry):
            for j in range(SUBLANES):
                r = r8 * SUBLANES + j
                p = p_ref[0, 0, r]
                pltpu.make_async_copy(ys_ref.at[pl.ds(p, 1)], buf.at[which, pl.ds(r, 1)],
                                      sem.at[which]).start(priority=j % 2)
            return carry
        lax.fori_loop(0, TS // SUBLANES, issue, 0)

    @pl.when(i == 0)
    def _():
        gather(pos_ref, slot)

    @pl.when(i + 1 < n)
    def _():
        gather(posn_ref, 1 - slot)

    pltpu.make_async_copy(ys_ref.at[pl.ds(0, TS)], buf.at[slot], sem.at[slot]).wait()
    gate = mod_ref[0, 0][5:6]
    o_ref[0] = x_ref[0] + gate * buf[slot]


def _moe_combine(x, modtab, pos, ys, kind_of_tile, tile_off, n_out_tiles):
    b, s_tot, d = x.shape
    nt = s_tot // TS
    n_tiles = b * nt
    def cur(bi, si):
        return (bi * nt + si + tile_off, 0, 0)
    def nxt(bi, si):
        last = si == n_out_tiles - 1
        nb = jnp.where(last, bi + 1, bi)
        ns = jnp.where(last, 0, si + 1)
        return (jnp.minimum(nb * nt + ns + tile_off, n_tiles - 1), 0, 0)
    return pl.pallas_call(
        _combine_kernel,
        grid=(b, n_out_tiles),
        in_specs=[
            pl.BlockSpec((1, 1, TS), cur, memory_space=pltpu.SMEM),
            pl.BlockSpec((1, 1, TS), nxt, memory_space=pltpu.SMEM),
            pl.BlockSpec((1, TS, d), lambda bi, si: (bi, si + tile_off, 0)),
            pl.BlockSpec((1, 1, N_MOD, d), lambda bi, si: (bi, kind_of_tile(si + tile_off), 0, 0)),
            pl.BlockSpec(memory_space=pl.ANY),
        ],
        out_specs=pl.BlockSpec((1, TS, d), lambda bi, si: (bi, si, 0)),
        out_shape=jax.ShapeDtypeStruct((b, n_out_tiles * TS, d), F32),
        scratch_shapes=[pltpu.VMEM((2, TS, d), F32), pltpu.SemaphoreType.DMA((2,))],
        compiler_params=_cparams(2),
        name="moe_combine",
    )(pos, pos, x, modtab, ys)


def _moe_layer(x, routed, modtab, router_w, router_b, w_gate, w_up, w_down, layer, kind_of_tile,
               tile_off, n_out_tiles):
    b, s_tot, d = x.shape
    t = b * s_tot
    h, grp, rank, cnt = routed
    counts = cnt[:, 0].astype(I32)
    padded = ((counts + TMG - 1) // TMG) * TMG
    ends = jnp.cumsum(padded)
    starts = ends - padded
    pos = starts[grp] + rank
    n_rows = t + N_GROUPS * TMG
    tile_row0 = jnp.arange(n_rows // TMG, dtype=I32) * TMG
    tile_group = jnp.minimum(jnp.sum(tile_row0[:, None] >= ends[None, :], axis=1),
                             N_GROUPS - 1).astype(I32)
    tile_valid = (tile_row0 < ends[-1]).astype(I32)
    tail_rows = jnp.minimum(ends[-1] + jnp.arange(N_GROUPS, dtype=I32) * TMG, n_rows - TMG)
    zero_rows = jnp.concatenate([jnp.where(padded > 0, ends - TMG, tail_rows[-1]), tail_rows])
    n_z = zero_rows.shape[0]
    repeat = jnp.any((zero_rows[:, None] == zero_rows[None, :])
                     & (jnp.arange(n_z)[None, :] < jnp.arange(n_z)[:, None]), axis=1)
    zero_rows = jnp.where(repeat, -1, zero_rows)
    xs = _moe_dispatch(h.reshape(t, d), pos, zero_rows.astype(I32), n_rows)
    epg = EXPERTS_PER_GROUP
    rw_g = jnp.pad(router_w.reshape(d, N_GROUPS, epg).transpose(1, 0, 2),
                   ((0, 0), (0, 0), (0, LANES - epg))).astype(BF16)
    rb_g = jnp.pad(router_b.reshape(N_GROUPS, 1, epg), ((0, 0), (0, 0), (0, LANES - epg)))
    ys = _moe_ffn(xs, tile_group, tile_valid, rw_g, rb_g, w_gate, w_up, w_down, layer)
    return _moe_combine(x, modtab, pos, ys, kind_of_tile, tile_off, n_out_tiles)


def _rope_pad(w, axis):
    half = QK_ROPE // 2
    x1, x2 = jnp.split(w, 2, axis=axis)
    z = jnp.zeros_like(x1)
    del half
    return jnp.concatenate([x1, z, x2, z], axis=axis)


def _oddproj_kernel(x_ref, mod_ref, g_ref, win_ref, gcq_ref, wuq_ref, gckv_ref, wukv_ref, gqm_ref,
                    gkm_ref, gqd_ref, gkd_ref, cs_ref, sna_ref, snb_ref, csm_ref, snm_ref,
                    qm_ref, km_ref, vm_ref, qd_ref, kd_ref, vd_ref, p_s, qm_s, kv_s):
    mod = mod_ref[0, 0]
    h = _norm_mod(x_ref[0], g_ref[...], mod[0:1], mod[1:2]).astype(BF16)
    q_lora = gcq_ref.shape[1]
    kv_lora = gckv_ref.shape[1]
    n_qd = DIFF_HEADS * 2 * DIFF_QK
    o_qd = q_lora
    o_ckv = o_qd + n_qd
    o_kr = o_ckv + kv_lora
    o_kd = o_kr + LANES
    o_vd = o_kd + n_qd
    lane = lax.broadcasted_iota(I32, (1, LANES), 1)
    low = lane < DIFF_QK
    d_mla = QK_NOPE + QK_ROPE

    def rms(x, g, n):
        return x * lax.rsqrt(jnp.sum(x * x, axis=-1, keepdims=True) / n + EPS) * g

    def rope_mla(y):
        return y * csm_ref[...] + pltpu.roll(y, LANES // 2, 1) * snm_ref[...]

    def diff_cols(p, g):
        sq = p * p
        s_all = jnp.sum(sq, axis=-1, keepdims=True)
        s_lo = jnp.sum(jnp.where(low, sq, 0.0), axis=-1, keepdims=True)
        r = jnp.where(low, lax.rsqrt(s_lo / DIFF_QK + EPS),
                      lax.rsqrt((s_all - s_lo) / DIFF_QK + EPS))
        y = p * r * g
        return (y * cs_ref[...] + pltpu.roll(y, LANES - DIFF_QK // 2, 1) * sna_ref[...]
                + pltpu.roll(y, DIFF_QK // 2, 1) * snb_ref[...])

    p_s[...] = jnp.dot(h, win_ref[...], preferred_element_type=F32)
    cq = rms(p_s[:, 0:q_lora], gcq_ref[...], q_lora)
    qm_s[...] = jnp.dot(cq.astype(BF16), wuq_ref[...], preferred_element_type=F32)
    ckv = rms(p_s[:, o_ckv:o_ckv + kv_lora], gckv_ref[...], kv_lora)
    kv_s[...] = jnp.dot(ckv.astype(BF16), wukv_ref[...], preferred_element_type=F32)

    for hd in range(DIFF_HEADS):
        cs = slice(LANES * hd, LANES * (hd + 1))
        y = diff_cols(p_s[:, o_qd + LANES * hd:o_qd + LANES * (hd + 1)], gqd_ref[...])
        y = y * (DIFF_SCALE * LOG2E)
        qd_ref[0, 2 * hd] = jnp.where(low, y, 0.0).astype(BF16)
        qd_ref[0, 2 * hd + 1] = jnp.where(low, 0.0, y).astype(BF16)
        kd_ref[0, hd] = diff_cols(p_s[:, o_kd + LANES * hd:o_kd + LANES * (hd + 1)],
                                  gkd_ref[...]).astype(BF16)
        vd_ref[0, hd] = p_s[:, o_vd + DIFF_V * hd:o_vd + DIFF_V * (hd + 1)].T.astype(BF16)
        del cs

    kr = p_s[:, o_kr:o_kr + LANES]
    for hd in range(MLA_HEADS):
        qh = rms(qm_s[:, 2 * LANES * hd:2 * LANES * (hd + 1)], gqm_ref[...], d_mla)
        qh = jnp.concatenate([qh[:, :LANES], rope_mla(qh[:, LANES:])], axis=-1)
        qm_ref[0, hd] = (qh * (MLA_SCALE * LOG2E)).astype(BF16)
        kcat = jnp.concatenate([kv_s[:, 2 * LANES * hd:2 * LANES * hd + LANES], kr], axis=-1)
        kh = rms(kcat, gkm_ref[...], d_mla)
        km_ref[0, hd] = jnp.concatenate([kh[:, :LANES], rope_mla(kh[:, LANES:])], axis=-1).astype(BF16)
        vm_ref[0, hd] = kv_s[:, 2 * LANES * hd + LANES:2 * LANES * (hd + 1)].T.astype(BF16)


def _odd_project(xa, modtab, norm_g, w_in, g_cq, w_uq, g_ckv, w_ukv, g_mla, g_diff, n_ctx):
    b, s_tot, d = xa.shape
    nt = s_tot // TS
    n_lat = s_tot - n_ctx
    q_lora = g_cq.shape[0]
    kv_lora = g_ckv.shape[0]
    n_qd = DIFF_HEADS * 2 * DIFF_QK
    d_mla = QK_NOPE + QK_ROPE
    o = 0
    w_cq = w_in[:, o:o + q_lora]; o += q_lora
    w_qd = w_in[:, o:o + n_qd]; o += n_qd
    w_ckv = w_in[:, o:o + kv_lora]; o += kv_lora
    w_kr = w_in[:, o:o + QK_ROPE]; o += QK_ROPE
    w_kd = w_in[:, o:o + n_qd]; o += n_qd
    w_vd = w_in[:, o:]
    w_in_p = jnp.concatenate([w_cq, w_qd, w_ckv, _rope_pad(w_kr, 1), w_kd, w_vd], axis=1).astype(BF16)
    wuq = w_uq.reshape(q_lora, MLA_HEADS, d_mla)
    wuq_p = jnp.concatenate([wuq[..., :QK_NOPE], _rope_pad(wuq[..., QK_NOPE:], 2)],
                            axis=-1).reshape(q_lora, MLA_HEADS * 2 * LANES).astype(BF16)

    def pad_gain(g):
        return jnp.concatenate([g[:QK_NOPE], _rope_pad(g[QK_NOPE:], 0)]).reshape(1, 2 * LANES)

    rows = n_lat // GRID_W
    row = jnp.repeat(jnp.arange(rows, dtype=F32), GRID_W)
    col = jnp.tile(jnp.arange(GRID_W, dtype=F32), rows)
    axis_dim = QK_ROPE // 2
    inv_freq = ROPE_BASE ** (-jnp.arange(0, axis_dim, 2, dtype=F32) / axis_dim)
    ang = jnp.concatenate([row[:, None] * inv_freq, col[:, None] * inv_freq], axis=-1)
    half = QK_ROPE // 2
    cos = jnp.concatenate([jnp.ones((n_ctx, half), F32), jnp.cos(ang)], axis=0)
    sin = jnp.concatenate([jnp.zeros((n_ctx, half), F32), jnp.sin(ang)], axis=0)
    z = jnp.zeros_like(sin)
    cs_d = jnp.tile(cos, (1, LANES // half))
    sna_d = jnp.tile(jnp.concatenate([-sin, z], axis=1), (1, LANES // QK_ROPE))
    snb_d = jnp.tile(jnp.concatenate([z, sin], axis=1), (1, LANES // QK_ROPE))
    cs_m = jnp.concatenate([cos, z, cos, z], axis=1)
    sn_m = jnp.concatenate([-sin, z, sin, z], axis=1)

    const2 = lambda bi, si: (0, 0)
    tok = lambda bi, si: (si, 0)
    kv4 = lambda bi, si: (bi, 0, si, 0)
    vt4 = lambda bi, si: (bi, 0, 0, si)
    q4 = lambda bi, si: (bi, 0, jnp.maximum(si - n_ctx // TS, 0), 0)
    full = lambda a: pl.BlockSpec(a.shape, const2)
    gq_d = jnp.tile(g_diff[0], LANES // DIFF_QK).reshape(1, LANES)
    gk_d = jnp.tile(g_diff[1], LANES // DIFF_QK).reshape(1, LANES)
    args = [xa, modtab, norm_g.reshape(1, d), w_in_p, g_cq.reshape(1, -1), wuq_p, g_ckv.reshape(1, -1),
            w_ukv.astype(BF16), pad_gain(g_mla[0]), pad_gain(g_mla[1]), gq_d, gk_d,
            cs_d, sna_d, snb_d, cs_m, sn_m]
    in_specs = [
        pl.BlockSpec((1, TS, d), lambda bi, si: (bi, si, 0)),
        pl.BlockSpec((1, 1, N_MOD, d), lambda bi, si: (bi, jnp.minimum(si, 1), 0, 0)),
    ] + [full(a) for a in args[2:12]] + [pl.BlockSpec((TS, LANES), tok)] * 5
    hm, hd = MLA_HEADS, DIFF_HEADS
    return pl.pallas_call(
        _oddproj_kernel,
        grid=(b, nt),
        in_specs=in_specs,
        out_specs=[
            pl.BlockSpec((1, hm, TS, 2 * LANES), q4),
            pl.BlockSpec((1, hm, TS, 2 * LANES), kv4),
            pl.BlockSpec((1, hm, V_HEAD, TS), vt4),
            pl.BlockSpec((1, 2 * hd, TS, LANES), q4),
            pl.BlockSpec((1, hd, TS, LANES), kv4),
            pl.BlockSpec((1, hd, DIFF_V, TS), vt4),
        ],
        out_shape=[
            jax.ShapeDtypeStruct((b, hm, n_lat, 2 * LANES), BF16),
            jax.ShapeDtypeStruct((b, hm, s_tot, 2 * LANES), BF16),
            jax.ShapeDtypeStruct((b, hm, V_HEAD, s_tot), BF16),
            jax.ShapeDtypeStruct((b, 2 * hd, n_lat, LANES), BF16),
            jax.ShapeDtypeStruct((b, hd, s_tot, LANES), BF16),
            jax.ShapeDtypeStruct((b, hd, DIFF_V, s_tot), BF16),
        ],
        scratch_shapes=[pltpu.VMEM((TS, w_in_p.shape[1]), F32),
                        pltpu.VMEM((TS, wuq_p.shape[1]), F32),
                        pltpu.VMEM((TS, w_ukv.shape[1]), F32)],
        compiler_params=_cparams(2),
        name="odd_project",
    )(*args)


def _attn_kernel(q_ref, k_ref, vt_ref, o_ref, s_s, p_s, *, rep):
    hq = q_ref.shape[1]
    n_chunks = k_ref.shape[2] // TK
    groups = TK // SUBLANES

    def scores(h):
        u = h % 2
        s_s[u] = lax.dot_general(k_ref[0, h // rep], q_ref[0, h], (((1,), (1,)), ((), ())),
                                 preferred_element_type=F32)
        m8 = jnp.full((SUBLANES, TQ), -jnp.inf, F32)
        for c in range(n_chunks):
            st = s_s[u, pl.ds(c * TK, TK), :]
            m8 = jnp.maximum(m8, jnp.max(st.reshape(groups, SUBLANES, TQ), axis=0))
        return jnp.max(m8, axis=0, keepdims=True)

    def exponentials(h, row_max):
        u = h % 2
        l8 = jnp.zeros((SUBLANES, TQ), F32)
        for c in range(n_chunks):
            p = jnp.exp2(s_s[u, pl.ds(c * TK, TK), :] - row_max)
            l8 = l8 + jnp.sum(p.reshape(groups, SUBLANES, TQ), axis=0)
            p_s[u, pl.ds(c * TK, TK), :] = p.astype(BF16)
        return jnp.sum(l8, axis=0, keepdims=True)

    def values(h, l):
        out_t = jnp.dot(vt_ref[0, h // rep], p_s[h % 2], preferred_element_type=F32)
        o_ref[0, h] = (out_t / l).T.astype(o_ref.dtype)

    row_max, row_sum = {}, {}
    for t in range(hq + 2):
        if t < hq:
            row_max[t] = scores(t)
        if 0 <= t - 1 < hq:
            row_sum[t - 1] = exponentials(t - 1, row_max[t - 1])
        if 0 <= t - 2 < hq:
            values(t - 2, row_sum[t - 2])


def _attention(q, k, vt, out_dtype, name):
    b, hq, lq, dk = q.shape
    hk, lk = k.shape[1], k.shape[2]
    dv = vt.shape[2]
    assert lk % TK == 0 and lq % TQ == 0 and hq % 2 == 0
    once = pl.Buffered(1)
    return pl.pallas_call(
        functools.partial(_attn_kernel, rep=hq // hk),
        grid=(b, lq // TQ),
        in_specs=[
            pl.BlockSpec((1, hq, TQ, dk), lambda bi, qi: (bi, 0, qi, 0)),
            pl.BlockSpec((1, hk, lk, dk), lambda bi, qi: (bi, 0, 0, 0), pipeline_mode=once),
            pl.BlockSpec((1, hk, dv, lk), lambda bi, qi: (bi, 0, 0, 0), pipeline_mode=once),
        ],
        out_specs=pl.BlockSpec((1, hq, TQ, dv), lambda bi, qi: (bi, 0, qi, 0)),
        out_shape=jax.ShapeDtypeStruct((b, hq, lq, dv), out_dtype),
        scratch_shapes=[pltpu.VMEM((2, lk, TQ), F32), pltpu.VMEM((2, lk, TQ), BF16)],
        compiler_params=_cparams(2),
        name=name,
    )(q, k, vt)


def _oddout_kernel(x_ref, mod_ref, om_ref, ad_ref, dl_ref, gs_ref, wout_ref, g2_ref, rwt_ref,
                   rb_ref, tri_ref, o_ref, h2_ref, grp_ref, rank_ref, cnt_ref, sc_s, carry_s, *,
                   lam_init):
    dl = dl_ref[...]
    lam = (jnp.exp(jnp.sum(dl[0:1] * dl[1:2], axis=-1, keepdims=True))
           - jnp.exp(jnp.sum(dl[2:3] * dl[3:4], axis=-1, keepdims=True)) + lam_init)
    parts = [om_ref[0, hd] for hd in range(MLA_HEADS)]
    for hd in range(DIFF_HEADS):
        df = ad_ref[0, 2 * hd] - lam * ad_ref[0, 2 * hd + 1]
        od = df * lax.rsqrt(jnp.mean(df * df, axis=-1, keepdims=True) + EPS) * gs_ref[...]
        parts.append((od * (1.0 - lam_init)).astype(BF16))
    y = jnp.dot(jnp.concatenate(parts, axis=-1), wout_ref[...], preferred_element_type=F32)
    mod = mod_ref[0, 0]
    x_new = x_ref[0] + mod[2:3] * y
    o_ref[0] = x_new
    _route_tile(x_new, mod, g2_ref, rwt_ref, rb_ref, tri_ref, h2_ref, grp_ref, rank_ref, cnt_ref,
                sc_s, carry_s)


def _odd_output(xa, modtab, o_m, a_d, diff_lambda, g_sub, w_out, lam_init, n_ctx, route):
    b, s_tot, d = xa.shape
    n_lat = s_tot - n_ctx
    off = n_ctx // TS
    const2 = lambda bi, si: (0, 0)
    r_args, r_in_specs, r_out_specs, r_out_shape, r_scratch = _route_plumbing(
        b, n_lat // TS, d, *route)
    return pl.pallas_call(
        functools.partial(_oddout_kernel, lam_init=lam_init),
        grid=(b, n_lat // TS),
        in_specs=[
            pl.BlockSpec((1, TS, d), lambda bi, si: (bi, si + off, 0)),
            pl.BlockSpec((1, 1, N_MOD, d), lambda bi, si: (bi, 1, 0, 0)),
            pl.BlockSpec((1, o_m.shape[1], TS, V_HEAD), lambda bi, si: (bi, 0, si, 0)),
            pl.BlockSpec((1, a_d.shape[1], TS, DIFF_V), lambda bi, si: (bi, 0, si, 0)),
            pl.BlockSpec(diff_lambda.shape, const2),
            pl.BlockSpec((1, DIFF_V), const2),
            pl.BlockSpec(w_out.shape, const2),
        ] + r_in_specs,
        out_specs=[pl.BlockSpec((1, TS, d), lambda bi, si: (bi, si, 0))] + r_out_specs,
        out_shape=[jax.ShapeDtypeStruct((b, n_lat, d), F32)] + r_out_shape,
        scratch_shapes=r_scratch,
        compiler_params=_cparams(2),
        name="odd_output",
    )(xa, modtab, o_m, a_d, diff_lambda, g_sub.reshape(1, -1), w_out.astype(BF16), *r_args)


def kernel(x, c, ctx, c_ctx, mod_w, mod_b, norm_g, even_w_in, conv_dw_w, conv_dw_b, conv_ln_g,
           conv_ln_b, pool_w, pool_scale, even_w_out, odd_w_in, mla_g_cq, mla_w_uq, mla_g_ckv,
           mla_w_ukv, qk_g_mla, qk_g_diff, diff_lambda, diff_sub_g, odd_w_out, router_w, router_b,
           moe_w_gate, moe_w_up, moe_w_down):
    b, n_lat, d = x.shape
    n_ctx = ctx.shape[1]
    assert n_ctx == TS and n_lat % TS == 0 and mod_w.shape[0] == 2
    nt = (n_ctx + n_lat) // TS

    c_rows = jnp.concatenate([c, c_ctx[None], jnp.zeros((16 - b - 1, d), F32)], axis=0)
    mods = _modulation(c_rows, mod_w, mod_b)
    modtabs = []
    for i in range(2):
        mod_l = mods[i, :b].reshape(b, N_MOD, d)
        mod_c = jnp.broadcast_to(mods[i, b].reshape(1, N_MOD, d), (b, N_MOD, d))
        modtabs.append(jnp.stack([mod_c, mod_l], axis=1))

    xa = jnp.concatenate([ctx, x], axis=1)
    ctx_or_lat = lambda si: jnp.minimum(si, 1)
    lat_only = lambda si: 1

    xa, *routed = _even_layer(xa, modtabs[0], norm_g[0, 0], even_w_in[0], conv_dw_w[0],
                              conv_dw_b[0], conv_ln_g[0], conv_ln_b[0], pool_w[0], pool_scale[0],
                              even_w_out[0], n_lat, (norm_g[0, 1], router_w, router_b))
    xa = _moe_layer(xa, routed, modtabs[0], router_w, router_b, moe_w_gate, moe_w_up, moe_w_down,
                    0, ctx_or_lat, 0, nt)

    qm, km, vm, qd, kd, vd = _odd_project(xa, modtabs[1], norm_g[1, 0], odd_w_in[0], mla_g_cq[0],
                                          mla_w_uq[0], mla_g_ckv[0], mla_w_ukv[0], qk_g_mla[0],
                                          qk_g_diff[0], n_ctx)
    o_m = _attention(qm, km, vm, BF16, "attn_mla")
    a_d = _attention(qd, kd, vd, F32, "attn_diff")
    lam_init = 0.8 - 0.6 * math.exp(-0.3 * 1)
    xl, *routed = _odd_output(xa, modtabs[1], o_m, a_d, diff_lambda[0], diff_sub_g[0],
                              odd_w_out[0], lam_init, n_ctx, (norm_g[1, 1], router_w, router_b))
    return _moe_layer(xl, routed, modtabs[1], router_w, router_b, moe_w_gate, moe_w_up,
                      moe_w_down, 1, lat_only, 0, n_lat // TS)
```

```python
import functools
import math

import jax
import jax.numpy as jnp
from jax import lax
from jax.experimental import pallas as pl
from jax.experimental.pallas import tpu as pltpu

F32 = jnp.float32
BF16 = jnp.bfloat16
I32 = jnp.int32

GRID_W = 64
N_MOD = 6
EPS = 1e-6
CONV_WIDTH = 31
POOL_WINDOWS = (2, 4, 8, 16)
MLA_HEADS = 4
QK_NOPE = 128
QK_ROPE = 64
V_HEAD = 128
DIFF_HEADS = 4
DIFF_QK = 64
DIFF_V = 128
MLA_SCALE = (QK_NOPE + QK_ROPE) ** -0.5
DIFF_SCALE = DIFF_QK ** -0.5
ROPE_BASE = 10000.0
LOG2E = math.log2(math.e)
N_EXPERTS = 32
N_GROUPS = 8
EXPERTS_PER_GROUP = N_EXPERTS // N_GROUPS

LANES = 128
SUBLANES = 8
TS = 256
HALO = 16
TMG = 256
TQ = 256
TK = 256
ATTN_Q_TILES = 2
DISPATCH_BUFS = 3
VMEM_LIMIT = 48 * 1024 * 1024


def _cparams(n_axes):
    return pltpu.CompilerParams(
        dimension_semantics=("arbitrary",) * n_axes, vmem_limit_bytes=VMEM_LIMIT)


def _norm_mod(x, g, shift, scale):
    ms = jnp.mean(x * x, axis=-1, keepdims=True)
    return (x * lax.rsqrt(ms + EPS) * g) * (1.0 + scale) + shift


def _silu(x):
    return x * jax.nn.sigmoid(x)


def _mod_kernel(c_ref, w_ref, b_ref, o_ref):
    a = _silu(c_ref[...])
    o_ref[0] = jnp.dot(a.astype(BF16), w_ref[0].astype(BF16),
                       preferred_element_type=F32) + b_ref[0]


def _modulation(c_rows, mod_w, mod_b):
    depth, d, n = mod_w.shape
    tn = 1536
    rows = c_rows.shape[0]
    return pl.pallas_call(
        _mod_kernel,
        grid=(depth, n // tn),
        in_specs=[
            pl.BlockSpec((rows, d), lambda l, j: (0, 0)),
            pl.BlockSpec((1, d, tn), lambda l, j: (l, 0, j)),
            pl.BlockSpec((1, 1, tn), lambda l, j: (l, 0, j)),
        ],
        out_specs=pl.BlockSpec((1, rows, tn), lambda l, j: (l, 0, j)),
        out_shape=jax.ShapeDtypeStruct((depth, rows, n), F32),
        compiler_params=_cparams(2),
        name="modulation",
    )(c_rows, mod_w, mod_b.reshape(depth, 1, n))


def _even_kernel(xc_ref, xl_ref, xp_ref, xn_ref, mod_ref, g_ref, win_ref, dww_ref, dwb_ref, lng_ref,
                 lnb_ref, pw_ref, ps_ref, wout_ref, g2_ref, rwt_ref, rb_ref, tri_ref,
                 o_ref, h2_ref, grp_ref, rank_ref, cnt_ref,
                 h_s, glu_s, pool_s, pres_s, mix_s, sc_s, carry_s, *, n_lat):
    s = pl.program_id(1)
    nt = pl.num_programs(1)
    d_conv = glu_s.shape[2]
    mod = mod_ref[0, 0]
    shift, scale, gate = mod[0:1], mod[1:2], mod[2:3]
    g = g_ref[...]
    xm = jnp.where(s == 0, xc_ref[0], xl_ref[0])
    h_s[0:HALO, :] = _norm_mod(xp_ref[0], g, shift, scale).astype(BF16)
    h_s[HALO:HALO + TS, :] = _norm_mod(xm, g, shift, scale).astype(BF16)
    h_s[HALO + TS:, :] = _norm_mod(xn_ref[0], g, shift, scale).astype(BF16)

    rows = HALO + TS + HALO
    ridx = lax.broadcasted_iota(I32, (rows, 1), 0)
    prev_ok = s >= 2
    next_ok = jnp.logical_and(s >= 1, s <= nt - 2)
    valid = jnp.logical_and(jnp.logical_or(ridx >= HALO, prev_ok),
                            jnp.logical_or(ridx < HALO + TS, next_ok))

    h = h_s[...]
    a = jnp.dot(h, win_ref[:, 0:d_conv], preferred_element_type=F32)
    gt = jnp.dot(h, win_ref[:, d_conv:2 * d_conv], preferred_element_type=F32)
    glu_s[0] = jnp.where(valid, a * jax.nn.sigmoid(gt), 0.0)
    for j in range(1, SUBLANES):
        glu_s[j, 0:rows - SUBLANES, :] = glu_s[0, pl.ds(j, rows - SUBLANES), :]
    pu = jnp.dot(h, win_ref[:, 2 * d_conv:], preferred_element_type=F32)
    pool_s[...] = jnp.where(valid, pu, 0.0)

    seq_pos0 = jnp.where(s == 0, 0, (s - 1) * TS)
    seq_len = jnp.where(s == 0, TS, n_lat)
    half = CONV_WIDTH // 2
    rc_rows = 64
    n_cc = d_conv // LANES
    for rc in range(TS // rc_rows):
        r0 = HALO + rc * rc_rows
        ys = []
        for cc in range(n_cc):
            cs = slice(cc * LANES, (cc + 1) * LANES)
            acc = jnp.zeros((rc_rows, LANES), F32) + dwb_ref[:, cs]
            for k in range(CONV_WIDTH):
                sh = (r0 + k - half) % SUBLANES
                acc = acc + dww_ref[k:k + 1, cs] * glu_s[sh, pl.ds(r0 + k - half - sh, rc_rows), cs]
            ys.append(acc)
        mu = sum(jnp.sum(y, axis=-1, keepdims=True) for y in ys) / d_conv
        var = sum(jnp.sum((y - mu) * (y - mu), axis=-1, keepdims=True) for y in ys) / d_conv
        rs = lax.rsqrt(var + EPS)
        for cc in range(n_cc):
            cs = slice(cc * LANES, (cc + 1) * LANES)
            z = (ys[cc] - mu) * rs * lng_ref[:, cs] + lnb_ref[:, cs]
            mix_s[rc * rc_rows:(rc + 1) * rc_rows, cs] = _silu(z).astype(BF16)
        pos = seq_pos0 + rc * rc_rows + lax.broadcasted_iota(I32, (rc_rows, 1), 0)
        for gi, w in enumerate(POOL_WINDOWS):
            cs = slice(gi * LANES, (gi + 1) * LANES)
            ssum = pool_s[pl.ds(r0 - w // 2, rc_rows), cs]
            for j in range(1 - w // 2, w // 2):
                ssum = ssum + pool_s[pl.ds(r0 + j, rc_rows), cs]
            lo = jnp.clip(pos - w // 2, 0, seq_len)
            hi = jnp.clip(pos + w // 2, 0, seq_len)
            cnt = (hi - lo).astype(F32)
            res = ssum / cnt - pool_s[pl.ds(r0, rc_rows), cs]
            pres_s[rc * rc_rows:(rc + 1) * rc_rows, cs] = res.astype(BF16)

    yp = jnp.dot(pres_s[...], pw_ref[...], preferred_element_type=F32) * ps_ref[...]
    mix_s[:, d_conv:] = yp.astype(BF16)
    y = jnp.dot(mix_s[...], wout_ref[...], preferred_element_type=F32)
    x_new = xm + gate * y
    o_ref[0] = x_new
    _route_tile(x_new, mod, g2_ref, rwt_ref, rb_ref, tri_ref, h2_ref, grp_ref, rank_ref, cnt_ref,
                sc_s, carry_s)


def _even_layer(ctx, x, modtab, norm_g, w_in, dw_w, dw_b, ln_g, ln_b, pool_w, pool_scale, w_out,
                route):
    b, n_lat, d = x.shape
    assert ctx.shape[1] == TS
    s_tot = TS + n_lat
    nt = s_tot // TS
    r_args, r_in_specs, r_out_specs, r_out_shape, r_scratch = _route_plumbing(b, nt, d, *route)
    d_conv = dw_w.shape[1]
    d_pool = pool_scale.shape[0]
    assert d_pool == len(POOL_WINDOWS) * LANES and pool_w.shape[1] == LANES
    pool_bd = jnp.zeros((d_pool, d_pool), F32)
    for gi in range(len(POOL_WINDOWS)):
        pool_bd = pool_bd.at[gi * LANES:(gi + 1) * LANES, gi * LANES:(gi + 1) * LANES].set(pool_w[gi])
    hb = TS // HALO
    n_hblk = n_lat // HALO
    const2 = lambda bi, si: (0, 0)
    return pl.pallas_call(
        functools.partial(_even_kernel, n_lat=n_lat),
        grid=(b, nt),
        in_specs=[
            pl.BlockSpec((1, TS, d), lambda bi, si: (bi, 0, 0)),
            pl.BlockSpec((1, TS, d), lambda bi, si: (bi, jnp.maximum(si - 1, 0), 0)),
            pl.BlockSpec((1, HALO, d), lambda bi, si: (bi, jnp.maximum((si - 1) * hb - 1, 0), 0)),
            pl.BlockSpec((1, HALO, d), lambda bi, si: (bi, jnp.clip(si * hb, 0, n_hblk - 1), 0)),
            pl.BlockSpec((1, 1, N_MOD, d), lambda bi, si: (bi, jnp.minimum(si, 1), 0, 0)),
            pl.BlockSpec((1, d), const2),
            pl.BlockSpec(w_in.shape, const2),
            pl.BlockSpec(dw_w.shape, const2),
            pl.BlockSpec((1, d_conv), const2),
            pl.BlockSpec((1, d_conv), const2),
            pl.BlockSpec((1, d_conv), const2),
            pl.BlockSpec((d_pool, d_pool), const2),
            pl.BlockSpec((1, d_pool), const2),
            pl.BlockSpec(w_out.shape, const2),
        ] + r_in_specs,
        out_specs=[pl.BlockSpec((1, TS, d), lambda bi, si: (bi, si, 0))] + r_out_specs,
        out_shape=[jax.ShapeDtypeStruct((b, s_tot, d), F32)] + r_out_shape,
        scratch_shapes=[
            pltpu.VMEM((TS + 2 * HALO, d), BF16),
            pltpu.VMEM((SUBLANES, TS + 2 * HALO, d_conv), F32),
            pltpu.VMEM((TS + 2 * HALO, d_pool), F32),
            pltpu.VMEM((TS, d_pool), BF16),
            pltpu.VMEM((TS, d_conv + d_pool), BF16),
        ] + r_scratch,
        compiler_params=_cparams(2),
        name="even_mixer",
    )(ctx, x, x, x, modtab, norm_g.reshape(1, d), w_in.astype(BF16), dw_w, dw_b.reshape(1, -1),
      ln_g.reshape(1, -1), ln_b.reshape(1, -1), pool_bd.astype(BF16), pool_scale.reshape(1, -1),
      w_out.astype(BF16), *r_args)


def _route_tile(x, mod, g_ref, rwt_ref, rb_ref, tri_ref, h_ref, grp_ref, rank_ref, cnt_ref, sc_s,
                carry_s):
    first = jnp.logical_and(pl.program_id(0) == 0, pl.program_id(1) == 0)

    @pl.when(first)
    def _():
        carry_s[...] = jnp.zeros_like(carry_s)

    h = _norm_mod(x, g_ref[...], mod[3:4], mod[4:5])
    h_ref[0] = h
    logits = lax.dot_general(rwt_ref[...], h.astype(BF16), (((1,), (1,)), ((), ())),
                             preferred_element_type=F32)
    biased = jax.nn.sigmoid(logits) + rb_ref[...]
    n_half = TS // LANES
    for hh in range(n_half):
        sc_s[hh] = biased[:, hh * LANES:(hh + 1) * LANES]
    a, b, c, d = [
        jnp.concatenate([sc_s[hh, pl.ds(j, N_GROUPS, stride=EXPERTS_PER_GROUP), :]
                         for hh in range(n_half)], axis=-1)
        for j in range(EXPERTS_PER_GROUP)]
    top2 = jnp.maximum(jnp.maximum(jnp.maximum(a + b, a + c), jnp.maximum(a + d, b + c)),
                       jnp.maximum(b + d, c + d))
    gmax = jnp.max(top2, axis=0, keepdims=True)
    gi = lax.broadcasted_iota(I32, top2.shape, 0)
    sel = jnp.min(jnp.where(top2 == gmax, gi, N_GROUPS), axis=0, keepdims=True)
    onehot = gi == sel
    ohf = jnp.where(onehot, 1.0, 0.0)
    prefix = jnp.dot(ohf.astype(BF16), tri_ref[...], preferred_element_type=F32)
    carry = carry_s[:, 0:1]
    rank = jnp.sum(jnp.where(onehot, carry + prefix - 1.0, 0.0), axis=0, keepdims=True)
    carry_s[...] = carry_s[...] + jnp.sum(ohf, axis=1, keepdims=True)
    grp_ref[0] = sel
    rank_ref[0] = rank.astype(I32)
    cnt_ref[...] = carry_s[...]


def _route_plumbing(b, nt, d, norm_g, router_w, router_b):
    const2 = lambda bi, si: (0, 0)
    flat3 = lambda bi, si: (bi * nt + si, 0, 0)
    tri = (jnp.arange(TS)[:, None] <= jnp.arange(TS)[None, :]).astype(BF16)
    args = [norm_g.reshape(1, d), router_w.T.astype(BF16), router_b.reshape(N_EXPERTS, 1), tri]
    in_specs = [pl.BlockSpec((1, d), const2), pl.BlockSpec((N_EXPERTS, d), const2),
                pl.BlockSpec((N_EXPERTS, 1), const2), pl.BlockSpec((TS, TS), const2)]
    out_specs = [pl.BlockSpec((1, TS, d), lambda bi, si: (bi, si, 0)),
                 pl.BlockSpec((1, 1, TS), flat3), pl.BlockSpec((1, 1, TS), flat3),
                 pl.BlockSpec((N_GROUPS, LANES), const2)]
    out_shape = [jax.ShapeDtypeStruct((b, nt * TS, d), F32),
                 jax.ShapeDtypeStruct((b * nt, 1, TS), I32),
                 jax.ShapeDtypeStruct((b * nt, 1, TS), I32),
                 jax.ShapeDtypeStruct((N_GROUPS, LANES), F32)]
    scratch = [pltpu.VMEM((TS // LANES, N_EXPERTS, LANES), F32), pltpu.VMEM((N_GROUPS, LANES), F32)]
    return args, in_specs, out_specs, out_shape, scratch


def _dispatch_kernel(zrow_ref, pos_ref, src_ref, dst_ref, buf, ld_sem, st_sem, z_sem):
    i = pl.program_id(0)
    n = pl.num_programs(0)
    slot = lax.rem(i, DISPATCH_BUFS)
    nxt = lax.rem(i + 1, DISPATCH_BUFS)
    tile_groups = TS // SUBLANES

    def load(tile, which):
        return pltpu.make_async_copy(src_ref.at[pl.ds(tile * tile_groups, tile_groups)],
                                     buf.at[which], ld_sem.at[which])

    def wait_scatter(which):
        pltpu.make_async_copy(buf.at[which], dst_ref.at[pl.ds(0, tile_groups)],
                              st_sem.at[which]).wait()

    @pl.when(i == 0)
    def _():
        load(i, slot).start()
        zsrc = buf.at[DISPATCH_BUFS - 1]
        zsrc[...] = jnp.zeros(zsrc.shape, F32)
        for z in range(zrow_ref.shape[0]):
            @pl.when(zrow_ref[z] >= 0)
            def _():
                pltpu.make_async_copy(zsrc, dst_ref.at[pl.ds(zrow_ref[z], tile_groups)],
                                      z_sem).start()
        for z in range(zrow_ref.shape[0]):
            @pl.when(zrow_ref[z] >= 0)
            def _():
                pltpu.make_async_copy(zsrc, dst_ref.at[pl.ds(0, tile_groups)], z_sem).wait()

    load(i, slot).wait()

    @pl.when(i + 1 >= DISPATCH_BUFS)
    def _():
        wait_scatter(nxt)

    @pl.when(i + 1 < n)
    def _():
        load(i + 1, nxt).start()

    def issue(r8, carry):
        for j in range(SUBLANES):
            p = pos_ref[0, 0, r8 * SUBLANES + j]
            dst_row = dst_ref.at[lax.shift_right_logical(p, 3), pl.ds(p & (SUBLANES - 1), 1)]
            pltpu.make_async_copy(buf.at[slot, r8, pl.ds(j, 1)], dst_row, st_sem.at[slot]).start()
        return carry

    lax.fori_loop(0, tile_groups, issue, 0)

    @pl.when(i == n - 1)
    def _():
        for back in range(DISPATCH_BUFS - 1):
            @pl.when(i - back >= 0)
            def _():
                wait_scatter(lax.rem(i - back + DISPATCH_BUFS, DISPATCH_BUFS))


def _moe_dispatch(h_flat, pos, zero_rows, n_rows):
    t, d = h_flat.shape
    n_tiles = t // TS
    assert TS == TMG and n_tiles >= DISPATCH_BUFS
    zero_groups = jnp.where(zero_rows >= 0, zero_rows // SUBLANES, -1).astype(I32)
    return pl.pallas_call(
        _dispatch_kernel,
        grid=(n_tiles,),
        in_specs=[
            pl.BlockSpec(memory_space=pltpu.SMEM),
            pl.BlockSpec((1, 1, TS), lambda i: (i, 0, 0), memory_space=pltpu.SMEM),
            pl.BlockSpec(memory_space=pl.ANY),
        ],
        out_specs=pl.BlockSpec(memory_space=pl.ANY),
        out_shape=jax.ShapeDtypeStruct((n_rows // SUBLANES, SUBLANES, d), F32),
        scratch_shapes=[pltpu.VMEM((DISPATCH_BUFS, TS // SUBLANES, SUBLANES, d), F32),
                        pltpu.SemaphoreType.DMA((DISPATCH_BUFS,)),
                        pltpu.SemaphoreType.DMA((DISPATCH_BUFS,)),
                        pltpu.SemaphoreType.DMA(())],
        compiler_params=_cparams(1),
        name="moe_dispatch",
    )(zero_groups, pos, h_flat.reshape(t // SUBLANES, SUBLANES, d))


def _ffn_kernel(tg_ref, tv_ref, xs_ref, rw_ref, rb_ref, wg32_ref, wu32_ref, wd32_ref, o_ref,
                wg_ref, wu_ref, wd_ref):
    i = pl.program_id(0)

    @pl.when(jnp.logical_or(i == 0, tg_ref[i] != tg_ref[jnp.maximum(i - 1, 0)]))
    def _():
        wg_ref[...] = wg32_ref[...].astype(BF16)
        wu_ref[...] = wu32_ref[...].astype(BF16)
        wd_ref[...] = wd32_ref[...].astype(BF16)

    @pl.when(tv_ref[i] == 0)
    def _():
        o_ref[...] = jnp.zeros_like(o_ref)

    @pl.when(tv_ref[i] != 0)
    def _():
        xb = xs_ref[...].astype(BF16)
        logits = jnp.dot(xb, rw_ref[0], preferred_element_type=F32)
        sc = jax.nn.sigmoid(logits)
        bs = sc + rb_ref[0]
        s_col = [sc[:, j:j + 1] for j in range(EXPERTS_PER_GROUP)]
        b_col = [bs[:, j:j + 1] for j in range(EXPERTS_PER_GROUP)]
        sel = []
        for j in range(EXPERTS_PER_GROUP):
            beaten = jnp.zeros_like(b_col[j])
            for k in range(EXPERTS_PER_GROUP):
                if k == j:
                    continue
                wins = (b_col[k] >= b_col[j]) if k < j else (b_col[k] > b_col[j])
                beaten = beaten + jnp.where(wins, 1.0, 0.0)
            sel.append(beaten < 2.0)
        den = sum(jnp.where(sel[j], s_col[j], 0.0) for j in range(EXPERTS_PER_GROUP))
        hes = []
        for j in range(EXPERTS_PER_GROUP):
            cj = jnp.where(sel[j], s_col[j] / den, 0.0)
            gj = jnp.dot(xb, wg_ref[0, j], preferred_element_type=F32)
            uj = jnp.dot(xb, wu_ref[0, j], preferred_element_type=F32)
            hes.append((_silu(gj) * uj * cj).astype(BF16))
        he = jnp.concatenate(hes, axis=-1)
        d_e = wd_ref.shape[2]
        wd = wd_ref[0].reshape(EXPERTS_PER_GROUP * d_e, wd_ref.shape[3])
        o_ref[...] = jnp.dot(he, wd, preferred_element_type=F32)


def _moe_ffn(xs, tile_group, tile_valid, rw_g, rb_g, w_gate, w_up, w_down, layer):
    n_rows, d = xs.shape
    d_e = w_gate.shape[-1]
    n_tiles = n_rows // TMG
    epg = EXPERTS_PER_GROUP
    grid_spec = pltpu.PrefetchScalarGridSpec(
        num_scalar_prefetch=2,
        grid=(n_tiles,),
        in_specs=[
            pl.BlockSpec((TMG, d), lambda i, tg, tv: (i, 0)),
            pl.BlockSpec((1, d, LANES), lambda i, tg, tv: (tg[i], 0, 0)),
            pl.BlockSpec((1, 1, LANES), lambda i, tg, tv: (tg[i], 0, 0)),
            pl.BlockSpec((1, epg, d, d_e), lambda i, tg, tv: (layer, tg[i], 0, 0)),
            pl.BlockSpec((1, epg, d, d_e), lambda i, tg, tv: (layer, tg[i], 0, 0)),
            pl.BlockSpec((1, epg, d_e, d), lambda i, tg, tv: (layer, tg[i], 0, 0)),
        ],
        out_specs=pl.BlockSpec((TMG, d), lambda i, tg, tv: (i, 0)),
        scratch_shapes=[pltpu.VMEM((1, epg, d, d_e), BF16), pltpu.VMEM((1, epg, d, d_e), BF16),
                        pltpu.VMEM((1, epg, d_e, d), BF16)],
    )
    return pl.pallas_call(
        _ffn_kernel,
        grid_spec=grid_spec,
        out_shape=jax.ShapeDtypeStruct((n_rows, d), F32),
        compiler_params=_cparams(1),
        name="moe_ffn",
    )(tile_group, tile_valid, xs, rw_g, rb_g, w_gate, w_up, w_down)


def _combine_kernel(pos_ref, posn_ref, x_ref, mod_ref, ys_ref, o_ref, buf, sem):
    i = pl.program_id(0) * pl.num_programs(1) + pl.program_id(1)
    n = pl.num_programs(0) * pl.num_programs(1)
    slot = lax.rem(i, 2)

    tile_groups = TS // SUBLANES

    def gather(p_ref, which):
        def issue(r8, carry):
            for j in range(SUBLANES):
                p = p_ref[0, 0, r8 * SUBLANES + j]
                src_row = ys_ref.at[lax.shift_right_logical(p, 3), pl.ds(p & (SUBLANES - 1), 1)]
                pltpu.make_async_copy(src_row, buf.at[which, r8, pl.ds(j, 1)], sem.at[which]).start()
            return carry
        lax.fori_loop(0, tile_groups, issue, 0)

    @pl.when(i == 0)
    def _():
        gather(pos_ref, slot)

    @pl.when(i + 1 < n)
    def _():
        gather(posn_ref, 1 - slot)

    pltpu.make_async_copy(ys_ref.at[pl.ds(0, tile_groups)], buf.at[slot], sem.at[slot]).wait()
    gate = mod_ref[0, 0][5:6]
    o_ref[0] = x_ref[0] + gate * buf[slot].reshape(TS, buf.shape[-1])


def _moe_combine(x, modtab, pos, ys, kind_of_tile, tile_off, n_out_tiles):
    b, s_tot, d = x.shape
    nt = s_tot // TS
    n_tiles = b * nt
    def cur(bi, si):
        return (bi * nt + si + tile_off, 0, 0)
    def nxt(bi, si):
        last = si == n_out_tiles - 1
        nb = jnp.where(last, bi + 1, bi)
        ns = jnp.where(last, 0, si + 1)
        return (jnp.minimum(nb * nt + ns + tile_off, n_tiles - 1), 0, 0)
    return pl.pallas_call(
        _combine_kernel,
        grid=(b, n_out_tiles),
        in_specs=[
            pl.BlockSpec((1, 1, TS), cur, memory_space=pltpu.SMEM),
            pl.BlockSpec((1, 1, TS), nxt, memory_space=pltpu.SMEM),
            pl.BlockSpec((1, TS, d), lambda bi, si: (bi, si + tile_off, 0)),
            pl.BlockSpec((1, 1, N_MOD, d), lambda bi, si: (bi, kind_of_tile(si + tile_off), 0, 0)),
            pl.BlockSpec(memory_space=pl.ANY),
        ],
        out_specs=pl.BlockSpec((1, TS, d), lambda bi, si: (bi, si, 0)),
        out_shape=jax.ShapeDtypeStruct((b, n_out_tiles * TS, d), F32),
        scratch_shapes=[pltpu.VMEM((2, TS // SUBLANES, SUBLANES, d), F32),
                        pltpu.SemaphoreType.DMA((2,))],
        compiler_params=_cparams(2),
        name="moe_combine",
    )(pos, pos, x, modtab, ys.reshape(ys.shape[0] // SUBLANES, SUBLANES, d))


def _moe_layer(x, routed, modtab, router_w, router_b, w_gate, w_up, w_down, layer, kind_of_tile,
               tile_off, n_out_tiles, combine=True):
    b, s_tot, d = x.shape
    t = b * s_tot
    h, grp, rank, cnt = routed
    counts = cnt[:, 0].astype(I32)
    padded = ((counts + TMG - 1) // TMG) * TMG
    ends = jnp.cumsum(padded)
    starts = ends - padded
    pos = starts[grp] + rank
    n_rows = t + N_GROUPS * TMG
    tile_row0 = jnp.arange(n_rows // TMG, dtype=I32) * TMG
    tile_group = jnp.minimum(jnp.sum(tile_row0[:, None] >= ends[None, :], axis=1),
                             N_GROUPS - 1).astype(I32)
    tile_valid = (tile_row0 < ends[-1]).astype(I32)
    tail_rows = jnp.minimum(ends[-1] + jnp.arange(N_GROUPS, dtype=I32) * TMG, n_rows - TMG)
    zero_rows = jnp.concatenate([jnp.where(padded > 0, ends - TMG, tail_rows[-1]), tail_rows])
    n_z = zero_rows.shape[0]
    repeat = jnp.any((zero_rows[:, None] == zero_rows[None, :])
                     & (jnp.arange(n_z)[None, :] < jnp.arange(n_z)[:, None]), axis=1)
    zero_rows = jnp.where(repeat, -1, zero_rows)
    xs = _moe_dispatch(h.reshape(t, d), pos, zero_rows.astype(I32), n_rows)
    epg = EXPERTS_PER_GROUP
    rw_g = jnp.pad(router_w.reshape(d, N_GROUPS, epg).transpose(1, 0, 2),
                   ((0, 0), (0, 0), (0, LANES - epg))).astype(BF16)
    rb_g = jnp.pad(router_b.reshape(N_GROUPS, 1, epg), ((0, 0), (0, 0), (0, LANES - epg)))
    ys = _moe_ffn(xs.reshape(n_rows, d), tile_group, tile_valid, rw_g, rb_g, w_gate, w_up, w_down,
                  layer)
    if not combine:
        return pos, ys
    return _moe_combine(x, modtab, pos, ys, kind_of_tile, tile_off, n_out_tiles)


def _rope_pad(w, axis):
    half = QK_ROPE // 2
    x1, x2 = jnp.split(w, 2, axis=axis)
    z = jnp.zeros_like(x1)
    del half
    return jnp.concatenate([x1, z, x2, z], axis=axis)


def _oddproj_kernel(pos_ref, posn_ref, x_ref, mod0_ref, ys_ref, mod_ref, g_ref, win_ref, gcq_ref,
                    wuq_ref, gckv_ref, wukv_ref, gqm_ref, gkm_ref, gqd_ref, gkd_ref, cs_ref, sna_ref,
                    snb_ref, csm_ref, snm_ref,
                    xo_ref, qm_ref, km_ref, vm_ref, qd_ref, kd_ref, vd_ref,
                    p_s, qm_s, kv_s, ybuf, ysem):
    i = pl.program_id(0) * pl.num_programs(1) + pl.program_id(1)
    n = pl.num_programs(0) * pl.num_programs(1)
    slot = lax.rem(i, 2)
    tile_groups = TS // SUBLANES

    def row_copy(p_ref, which, r8, j):
        p = p_ref[0, 0, r8 * SUBLANES + j]
        src_row = ys_ref.at[lax.shift_right_logical(p, 3), pl.ds(p & (SUBLANES - 1), 1)]
        return pltpu.make_async_copy(src_row, ybuf.at[which, r8, pl.ds(j, 1)], ysem.at[which])

    def wait_tile(which):
        pltpu.make_async_copy(ys_ref.at[pl.ds(0, tile_groups)], ybuf.at[which], ysem.at[which]).wait()

    @pl.when(i == 0)
    def _():
        def issue(r8, carry):
            for j in range(SUBLANES):
                row_copy(pos_ref, slot, r8, j).start()
            return carry
        lax.fori_loop(0, tile_groups, issue, 0)

    wait_tile(slot)
    x_new = x_ref[0] + mod0_ref[0, 0][5:6] * ybuf[slot].reshape(TS, ybuf.shape[-1])
    xo_ref[0] = x_new
    for r8 in range(tile_groups):
        for j in range(SUBLANES):
            row_copy(posn_ref, 1 - slot, r8, j).start()

    mod = mod_ref[0, 0]
    h = _norm_mod(x_new, g_ref[...], mod[0:1], mod[1:2]).astype(BF16)
    q_lora = gcq_ref.shape[1]
    kv_lora = gckv_ref.shape[1]
    n_qd = DIFF_HEADS * 2 * DIFF_QK
    o_qd = q_lora
    o_ckv = o_qd + n_qd
    o_kr = o_ckv + kv_lora
    o_kd = o_kr + LANES
    o_vd = o_kd + n_qd
    lane = lax.broadcasted_iota(I32, (1, LANES), 1)
    low = lane < DIFF_QK
    d_mla = QK_NOPE + QK_ROPE

    def rms(x, g, n):
        return x * lax.rsqrt(jnp.sum(x * x, axis=-1, keepdims=True) / n + EPS) * g

    def rope_mla(y):
        return y * csm_ref[...] + pltpu.roll(y, LANES // 2, 1) * snm_ref[...]

    def diff_cols(p, g):
        sq = p * p
        s_all = jnp.sum(sq, axis=-1, keepdims=True)
        s_lo = jnp.sum(jnp.where(low, sq, 0.0), axis=-1, keepdims=True)
        r = jnp.where(low, lax.rsqrt(s_lo / DIFF_QK + EPS),
                      lax.rsqrt((s_all - s_lo) / DIFF_QK + EPS))
        y = p * r * g
        return (y * cs_ref[...] + pltpu.roll(y, LANES - DIFF_QK // 2, 1) * sna_ref[...]
                + pltpu.roll(y, DIFF_QK // 2, 1) * snb_ref[...])

    p_s[...] = jnp.dot(h, win_ref[...], preferred_element_type=F32)
    cq = rms(p_s[:, 0:q_lora], gcq_ref[...], q_lora)
    qm_s[...] = jnp.dot(cq.astype(BF16), wuq_ref[...], preferred_element_type=F32)
    ckv = rms(p_s[:, o_ckv:o_ckv + kv_lora], gckv_ref[...], kv_lora)
    kv_s[...] = jnp.dot(ckv.astype(BF16), wukv_ref[...], preferred_element_type=F32)

    for hd in range(DIFF_HEADS):
        cs = slice(LANES * hd, LANES * (hd + 1))
        y = diff_cols(p_s[:, o_qd + LANES * hd:o_qd + LANES * (hd + 1)], gqd_ref[...])
        y = y * (DIFF_SCALE * LOG2E)
        qd_ref[0, 2 * hd] = jnp.where(low, y, 0.0).astype(BF16)
        qd_ref[0, 2 * hd + 1] = jnp.where(low, 0.0, y).astype(BF16)
        kd_ref[0, hd] = diff_cols(p_s[:, o_kd + LANES * hd:o_kd + LANES * (hd + 1)],
                                  gkd_ref[...]).astype(BF16)
        vd_ref[0, hd] = p_s[:, o_vd + DIFF_V * hd:o_vd + DIFF_V * (hd + 1)].T.astype(BF16)
        del cs

    kr = p_s[:, o_kr:o_kr + LANES]
    for hd in range(MLA_HEADS):
        qh = rms(qm_s[:, 2 * LANES * hd:2 * LANES * (hd + 1)], gqm_ref[...], d_mla)
        qh = jnp.concatenate([qh[:, :LANES], rope_mla(qh[:, LANES:])], axis=-1)
        qm_ref[0, hd] = (qh * (MLA_SCALE * LOG2E)).astype(BF16)
        kcat = jnp.concatenate([kv_s[:, 2 * LANES * hd:2 * LANES * hd + LANES], kr], axis=-1)
        kh = rms(kcat, gkm_ref[...], d_mla)
        km_ref[0, hd] = jnp.concatenate([kh[:, :LANES], rope_mla(kh[:, LANES:])], axis=-1).astype(BF16)
        vm_ref[0, hd] = kv_s[:, 2 * LANES * hd + LANES:2 * LANES * (hd + 1)].T.astype(BF16)

    @pl.when(i == n - 1)
    def _():
        wait_tile(1 - slot)


def _odd_project(xa, pos, ys, modtab0, modtab, norm_g, w_in, g_cq, w_uq, g_ckv, w_ukv, g_mla, g_diff,
                 n_ctx):
    b, s_tot, d = xa.shape
    nt = s_tot // TS
    n_tiles = b * nt
    n_lat = s_tot - n_ctx
    q_lora = g_cq.shape[0]
    kv_lora = g_ckv.shape[0]
    n_qd = DIFF_HEADS * 2 * DIFF_QK
    d_mla = QK_NOPE + QK_ROPE
    o = 0
    w_cq = w_in[:, o:o + q_lora]; o += q_lora
    w_qd = w_in[:, o:o + n_qd]; o += n_qd
    w_ckv = w_in[:, o:o + kv_lora]; o += kv_lora
    w_kr = w_in[:, o:o + QK_ROPE]; o += QK_ROPE
    w_kd = w_in[:, o:o + n_qd]; o += n_qd
    w_vd = w_in[:, o:]
    w_in_p = jnp.concatenate([w_cq, w_qd, w_ckv, _rope_pad(w_kr, 1), w_kd, w_vd], axis=1).astype(BF16)
    wuq = w_uq.reshape(q_lora, MLA_HEADS, d_mla)
    wuq_p = jnp.concatenate([wuq[..., :QK_NOPE], _rope_pad(wuq[..., QK_NOPE:], 2)],
                            axis=-1).reshape(q_lora, MLA_HEADS * 2 * LANES).astype(BF16)

    def pad_gain(g):
        return jnp.concatenate([g[:QK_NOPE], _rope_pad(g[QK_NOPE:], 0)]).reshape(1, 2 * LANES)

    rows = n_lat // GRID_W
    row = jnp.repeat(jnp.arange(rows, dtype=F32), GRID_W)
    col = jnp.tile(jnp.arange(GRID_W, dtype=F32), rows)
    axis_dim = QK_ROPE // 2
    inv_freq = ROPE_BASE ** (-jnp.arange(0, axis_dim, 2, dtype=F32) / axis_dim)
    ang = jnp.concatenate([row[:, None] * inv_freq, col[:, None] * inv_freq], axis=-1)
    half = QK_ROPE // 2
    cos = jnp.concatenate([jnp.ones((n_ctx, half), F32), jnp.cos(ang)], axis=0)
    sin = jnp.concatenate([jnp.zeros((n_ctx, half), F32), jnp.sin(ang)], axis=0)
    z = jnp.zeros_like(sin)
    cs_d = jnp.tile(cos, (1, LANES // half))
    sna_d = jnp.tile(jnp.concatenate([-sin, z], axis=1), (1, LANES // QK_ROPE))
    snb_d = jnp.tile(jnp.concatenate([z, sin], axis=1), (1, LANES // QK_ROPE))
    cs_m = jnp.concatenate([cos, z, cos, z], axis=1)
    sn_m = jnp.concatenate([-sin, z, sin, z], axis=1)

    const2 = lambda bi, si: (0, 0)
    tok = lambda bi, si: (si, 0)
    kv4 = lambda bi, si: (bi, 0, si, 0)
    vt4 = lambda bi, si: (bi, 0, 0, si)
    q4 = lambda bi, si: (bi, 0, jnp.maximum(si - n_ctx // TS, 0), 0)
    full = lambda a: pl.BlockSpec(a.shape, const2)
    gq_d = jnp.tile(g_diff[0], LANES // DIFF_QK).reshape(1, LANES)
    gk_d = jnp.tile(g_diff[1], LANES // DIFF_QK).reshape(1, LANES)
    weights = [norm_g.reshape(1, d), w_in_p, g_cq.reshape(1, -1), wuq_p, g_ckv.reshape(1, -1),
               w_ukv.astype(BF16), pad_gain(g_mla[0]), pad_gain(g_mla[1]), gq_d, gk_d]
    args = [pos, pos, xa, modtab0, ys.reshape(ys.shape[0] // SUBLANES, SUBLANES, d), modtab,
            *weights, cs_d, sna_d, snb_d, cs_m, sn_m]
    mod_spec = pl.BlockSpec((1, 1, N_MOD, d), lambda bi, si: (bi, jnp.minimum(si, 1), 0, 0))
    in_specs = [
        pl.BlockSpec((1, 1, TS), lambda bi, si: (bi * nt + si, 0, 0), memory_space=pltpu.SMEM),
        pl.BlockSpec((1, 1, TS), lambda bi, si: (jnp.minimum(bi * nt + si + 1, n_tiles - 1), 0, 0),
                     memory_space=pltpu.SMEM),
        pl.BlockSpec((1, TS, d), lambda bi, si: (bi, si, 0)),
        mod_spec,
        pl.BlockSpec(memory_space=pl.ANY),
        mod_spec,
    ] + [full(a) for a in weights] + [pl.BlockSpec((TS, LANES), tok)] * 5
    hm, hd = MLA_HEADS, DIFF_HEADS
    return pl.pallas_call(
        _oddproj_kernel,
        grid=(b, nt),
        in_specs=in_specs,
        out_specs=[
            pl.BlockSpec((1, TS, d), lambda bi, si: (bi, si, 0)),
            pl.BlockSpec((1, hm, TS, 2 * LANES), q4),
            pl.BlockSpec((1, hm, TS, 2 * LANES), kv4),
            pl.BlockSpec((1, hm, V_HEAD, TS), vt4),
            pl.BlockSpec((1, 2 * hd, TS, LANES), q4),
            pl.BlockSpec((1, hd, TS, LANES), kv4),
            pl.BlockSpec((1, hd, DIFF_V, TS), vt4),
        ],
        out_shape=[
            jax.ShapeDtypeStruct((b, s_tot, d), F32),
            jax.ShapeDtypeStruct((b, hm, n_lat, 2 * LANES), BF16),
            jax.ShapeDtypeStruct((b, hm, s_tot, 2 * LANES), BF16),
            jax.ShapeDtypeStruct((b, hm, V_HEAD, s_tot), BF16),
            jax.ShapeDtypeStruct((b, 2 * hd, n_lat, LANES), BF16),
            jax.ShapeDtypeStruct((b, hd, s_tot, LANES), BF16),
            jax.ShapeDtypeStruct((b, hd, DIFF_V, s_tot), BF16),
        ],
        scratch_shapes=[pltpu.VMEM((TS, w_in_p.shape[1]), F32),
                        pltpu.VMEM((TS, wuq_p.shape[1]), F32),
                        pltpu.VMEM((TS, w_ukv.shape[1]), F32),
                        pltpu.VMEM((2, TS // SUBLANES, SUBLANES, d), F32),
                        pltpu.SemaphoreType.DMA((2,))],
        compiler_params=_cparams(2),
        name="odd_project",
    )(*args)


def _attn_kernel(q_ref, k_ref, vt_ref, o_ref, s_s, p_s, *, rep):
    hq = q_ref.shape[1]
    n_items = hq * (q_ref.shape[2] // TQ)
    n_chunks = k_ref.shape[2] // TK
    groups = TK // SUBLANES

    def q_rows(i):
        return pl.ds((i // hq) * TQ, TQ)

    def scores(i):
        u, h = i % 2, i % hq
        s_s[u] = lax.dot_general(k_ref[0, h // rep], q_ref[0, h, q_rows(i), :],
                                 (((1,), (1,)), ((), ())), preferred_element_type=F32)
        m8 = jnp.full((SUBLANES, TQ), -jnp.inf, F32)
        for c in range(n_chunks):
            st = s_s[u, pl.ds(c * TK, TK), :]
            m8 = jnp.maximum(m8, jnp.max(st.reshape(groups, SUBLANES, TQ), axis=0))
        return jnp.max(m8, axis=0, keepdims=True)

    def exponentials(i, row_max):
        u = i % 2
        l8 = jnp.zeros((SUBLANES, TQ), F32)
        for c in range(n_chunks):
            p = jnp.exp2(s_s[u, pl.ds(c * TK, TK), :] - row_max)
            l8 = l8 + jnp.sum(p.reshape(groups, SUBLANES, TQ), axis=0)
            p_s[u, pl.ds(c * TK, TK), :] = p.astype(BF16)
        return jnp.sum(l8, axis=0, keepdims=True)

    def values(i, l):
        h = i % hq
        out_t = jnp.dot(vt_ref[0, h // rep], p_s[i % 2], preferred_element_type=F32)
        o_ref[0, h, q_rows(i), :] = (out_t / l).T.astype(o_ref.dtype)

    row_max, row_sum = {}, {}
    for t in range(n_items + 2):
        if t < n_items:
            row_max[t] = scores(t)
        if 0 <= t - 1 < n_items:
            row_sum[t - 1] = exponentials(t - 1, row_max.pop(t - 1))
        if 0 <= t - 2 < n_items:
            values(t - 2, row_sum.pop(t - 2))


def _attention(q, k, vt, out_dtype, name):
    b, hq, lq, dk = q.shape
    hk, lk = k.shape[1], k.shape[2]
    dv = vt.shape[2]
    tq_step = ATTN_Q_TILES * TQ
    assert lk % TK == 0 and lq % tq_step == 0
    once = pl.Buffered(1)
    return pl.pallas_call(
        functools.partial(_attn_kernel, rep=hq // hk),
        grid=(b, lq // tq_step),
        in_specs=[
            pl.BlockSpec((1, hq, tq_step, dk), lambda bi, qi: (bi, 0, qi, 0)),
            pl.BlockSpec((1, hk, lk, dk), lambda bi, qi: (bi, 0, 0, 0), pipeline_mode=once),
            pl.BlockSpec((1, hk, dv, lk), lambda bi, qi: (bi, 0, 0, 0), pipeline_mode=once),
        ],
        out_specs=pl.BlockSpec((1, hq, tq_step, dv), lambda bi, qi: (bi, 0, qi, 0)),
        out_shape=jax.ShapeDtypeStruct((b, hq, lq, dv), out_dtype),
        scratch_shapes=[pltpu.VMEM((2, lk, TQ), F32), pltpu.VMEM((2, lk, TQ), BF16)],
        compiler_params=_cparams(2),
        name=name,
    )(q, k, vt)


def _oddout_kernel(x_ref, mod_ref, om_ref, ad_ref, dl_ref, gs_ref, wout_ref, g2_ref, rwt_ref,
                   rb_ref, tri_ref, o_ref, h2_ref, grp_ref, rank_ref, cnt_ref, sc_s, carry_s, *,
                   lam_init):
    dl = dl_ref[...]
    lam = (jnp.exp(jnp.sum(dl[0:1] * dl[1:2], axis=-1, keepdims=True))
           - jnp.exp(jnp.sum(dl[2:3] * dl[3:4], axis=-1, keepdims=True)) + lam_init)
    parts = [om_ref[0, hd] for hd in range(MLA_HEADS)]
    for hd in range(DIFF_HEADS):
        df = ad_ref[0, 2 * hd] - lam * ad_ref[0, 2 * hd + 1]
        od = df * lax.rsqrt(jnp.mean(df * df, axis=-1, keepdims=True) + EPS) * gs_ref[...]
        parts.append((od * (1.0 - lam_init)).astype(BF16))
    y = jnp.dot(jnp.concatenate(parts, axis=-1), wout_ref[...], preferred_element_type=F32)
    mod = mod_ref[0, 0]
    x_new = x_ref[0] + mod[2:3] * y
    o_ref[0] = x_new
    _route_tile(x_new, mod, g2_ref, rwt_ref, rb_ref, tri_ref, h2_ref, grp_ref, rank_ref, cnt_ref,
                sc_s, carry_s)


def _odd_output(xa, modtab, o_m, a_d, diff_lambda, g_sub, w_out, lam_init, n_ctx, route):
    b, s_tot, d = xa.shape
    n_lat = s_tot - n_ctx
    off = n_ctx // TS
    const2 = lambda bi, si: (0, 0)
    r_args, r_in_specs, r_out_specs, r_out_shape, r_scratch = _route_plumbing(
        b, n_lat // TS, d, *route)
    return pl.pallas_call(
        functools.partial(_oddout_kernel, lam_init=lam_init),
        grid=(b, n_lat // TS),
        in_specs=[
            pl.BlockSpec((1, TS, d), lambda bi, si: (bi, si + off, 0)),
            pl.BlockSpec((1, 1, N_MOD, d), lambda bi, si: (bi, 1, 0, 0)),
            pl.BlockSpec((1, o_m.shape[1], TS, V_HEAD), lambda bi, si: (bi, 0, si, 0)),
            pl.BlockSpec((1, a_d.shape[1], TS, DIFF_V), lambda bi, si: (bi, 0, si, 0)),
            pl.BlockSpec(diff_lambda.shape, const2),
            pl.BlockSpec((1, DIFF_V), const2),
            pl.BlockSpec(w_out.shape, const2),
        ] + r_in_specs,
        out_specs=[pl.BlockSpec((1, TS, d), lambda bi, si: (bi, si, 0))] + r_out_specs,
        out_shape=[jax.ShapeDtypeStruct((b, n_lat, d), F32)] + r_out_shape,
        scratch_shapes=r_scratch,
        compiler_params=_cparams(2),
        name="odd_output",
    )(xa, modtab, o_m, a_d, diff_lambda, g_sub.reshape(1, -1), w_out.astype(BF16), *r_args)


def kernel(x, c, ctx, c_ctx, mod_w, mod_b, norm_g, even_w_in, conv_dw_w, conv_dw_b, conv_ln_g,
           conv_ln_b, pool_w, pool_scale, even_w_out, odd_w_in, mla_g_cq, mla_w_uq, mla_g_ckv,
           mla_w_ukv, qk_g_mla, qk_g_diff, diff_lambda, diff_sub_g, odd_w_out, router_w, router_b,
           moe_w_gate, moe_w_up, moe_w_down):
    b, n_lat, d = x.shape
    n_ctx = ctx.shape[1]
    assert n_ctx == TS and n_lat % TS == 0 and mod_w.shape[0] == 2
    nt = (n_ctx + n_lat) // TS

    c_rows = jnp.concatenate([c, c_ctx[None], jnp.zeros((16 - b - 1, d), F32)], axis=0)
    mods = _modulation(c_rows, mod_w, mod_b)
    modtabs = []
    for i in range(2):
        mod_l = mods[i, :b].reshape(b, N_MOD, d)
        mod_c = jnp.broadcast_to(mods[i, b].reshape(1, N_MOD, d), (b, N_MOD, d))
        modtabs.append(jnp.stack([mod_c, mod_l], axis=1))

    ctx_or_lat = lambda si: jnp.minimum(si, 1)
    lat_only = lambda si: 1

    xa, *routed = _even_layer(ctx, x, modtabs[0], norm_g[0, 0], even_w_in[0], conv_dw_w[0],
                              conv_dw_b[0], conv_ln_g[0], conv_ln_b[0], pool_w[0], pool_scale[0],
                              even_w_out[0], (norm_g[0, 1], router_w, router_b))
    pos0, ys0 = _moe_layer(xa, routed, modtabs[0], router_w, router_b, moe_w_gate, moe_w_up,
                           moe_w_down, 0, ctx_or_lat, 0, nt, combine=False)

    xa, qm, km, vm, qd, kd, vd = _odd_project(xa, pos0, ys0, modtabs[0], modtabs[1], norm_g[1, 0],
                                              odd_w_in[0], mla_g_cq[0], mla_w_uq[0], mla_g_ckv[0],
                                              mla_w_ukv[0], qk_g_mla[0], qk_g_diff[0], n_ctx)
    o_m = _attention(qm, km, vm, BF16, "attn_mla")
    a_d = _attention(qd, kd, vd, F32, "attn_diff")
    lam_init = 0.8 - 0.6 * math.exp(-0.3 * 1)
    xl, *routed = _odd_output(xa, modtabs[1], o_m, a_d, diff_lambda[0], diff_sub_g[0],
                              odd_w_out[0], lam_init, n_ctx, (norm_g[1, 1], router_w, router_b))
    return _moe_layer(xl, routed, modtabs[1], router_w, router_b, moe_w_gate, moe_w_up,
                      moe_w_down, 1, lat_only, 0, n_lat // TS)
```

```python
import functools
import math

import jax
import jax.numpy as jnp
from jax import lax
from jax.experimental import pallas as pl
from jax.experimental.pallas import tpu as pltpu

F32 = jnp.float32
BF16 = jnp.bfloat16
I32 = jnp.int32

GRID_W = 64
N_MOD = 6
EPS = 1e-6
CONV_WIDTH = 31
POOL_WINDOWS = (2, 4, 8, 16)
MLA_HEADS = 4
QK_NOPE = 128
QK_ROPE = 64
V_HEAD = 128
DIFF_HEADS = 4
DIFF_QK = 64
DIFF_V = 128
MLA_SCALE = (QK_NOPE + QK_ROPE) ** -0.5
DIFF_SCALE = DIFF_QK ** -0.5
ROPE_BASE = 10000.0
LOG2E = math.log2(math.e)
N_EXPERTS = 32
N_GROUPS = 8
EXPERTS_PER_GROUP = N_EXPERTS // N_GROUPS

LANES = 128
SUBLANES = 8
TS = 256
HALO = 16
TMG = 256
TQ = 256
TK = 256
ATTN_Q_TILES = 4
DISPATCH_BUFS = 3
VMEM_LIMIT = 48 * 1024 * 1024


def _cparams(n_axes):
    return pltpu.CompilerParams(
        dimension_semantics=("arbitrary",) * n_axes, vmem_limit_bytes=VMEM_LIMIT)


def _norm_mod(x, g, shift, scale):
    ms = jnp.mean(x * x, axis=-1, keepdims=True)
    return (x * lax.rsqrt(ms + EPS) * g) * (1.0 + scale) + shift


def _silu(x):
    return x * jax.nn.sigmoid(x)


def _mod_kernel(c_ref, w_ref, b_ref, o_ref):
    a = _silu(c_ref[...])
    o_ref[0] = jnp.dot(a.astype(BF16), w_ref[0].astype(BF16),
                       preferred_element_type=F32) + b_ref[0]


def _modulation(c_rows, mod_w, mod_b):
    depth, d, n = mod_w.shape
    tn = 1536
    rows = c_rows.shape[0]
    return pl.pallas_call(
        _mod_kernel,
        grid=(depth, n // tn),
        in_specs=[
            pl.BlockSpec((rows, d), lambda l, j: (0, 0)),
            pl.BlockSpec((1, d, tn), lambda l, j: (l, 0, j)),
            pl.BlockSpec((1, 1, tn), lambda l, j: (l, 0, j)),
        ],
        out_specs=pl.BlockSpec((1, rows, tn), lambda l, j: (l, 0, j)),
        out_shape=jax.ShapeDtypeStruct((depth, rows, n), F32),
        compiler_params=_cparams(2),
        name="modulation",
    )(c_rows, mod_w, mod_b.reshape(depth, 1, n))


def _even_kernel(xc_ref, xl_ref, xp_ref, xn_ref, mod_ref, g_ref, win_ref, dww_ref, dwb_ref, lng_ref,
                 lnb_ref, pw_ref, ps_ref, wout_ref, g2_ref, rwt_ref, rb_ref, tri_ref,
                 o_ref, h2_ref, grp_ref, rank_ref, cnt_ref,
                 h_s, glu_s, pool_s, pres_s, mix_s, sc_s, carry_s, *, n_lat):
    s = pl.program_id(1)
    nt = pl.num_programs(1)
    d_conv = glu_s.shape[2]
    mod = mod_ref[0, 0]
    shift, scale, gate = mod[0:1], mod[1:2], mod[2:3]
    g = g_ref[...]
    xm = jnp.where(s == 0, xc_ref[0], xl_ref[0])
    h_s[0:HALO, :] = _norm_mod(xp_ref[0], g, shift, scale).astype(BF16)
    h_s[HALO:HALO + TS, :] = _norm_mod(xm, g, shift, scale).astype(BF16)
    h_s[HALO + TS:, :] = _norm_mod(xn_ref[0], g, shift, scale).astype(BF16)

    rows = HALO + TS + HALO
    ridx = lax.broadcasted_iota(I32, (rows, 1), 0)
    prev_ok = s >= 2
    next_ok = jnp.logical_and(s >= 1, s <= nt - 2)
    valid = jnp.logical_and(jnp.logical_or(ridx >= HALO, prev_ok),
                            jnp.logical_or(ridx < HALO + TS, next_ok))

    h = h_s[...]
    a = jnp.dot(h, win_ref[:, 0:d_conv], preferred_element_type=F32)
    gt = jnp.dot(h, win_ref[:, d_conv:2 * d_conv], preferred_element_type=F32)
    glu_s[0] = jnp.where(valid, a * jax.nn.sigmoid(gt), 0.0)
    for j in range(1, SUBLANES):
        glu_s[j, 0:rows - SUBLANES, :] = glu_s[0, pl.ds(j, rows - SUBLANES), :]
    pu = jnp.dot(h, win_ref[:, 2 * d_conv:], preferred_element_type=F32)
    pool_s[...] = jnp.where(valid, pu, 0.0)

    seq_pos0 = jnp.where(s == 0, 0, (s - 1) * TS)
    seq_len = jnp.where(s == 0, TS, n_lat)
    half = CONV_WIDTH // 2
    rc_rows = 64
    n_cc = d_conv // LANES
    for rc in range(TS // rc_rows):
        r0 = HALO + rc * rc_rows
        ys = []
        for cc in range(n_cc):
            cs = slice(cc * LANES, (cc + 1) * LANES)
            acc = jnp.zeros((rc_rows, LANES), F32) + dwb_ref[:, cs]
            for k in range(CONV_WIDTH):
                sh = (r0 + k - half) % SUBLANES
                acc = acc + dww_ref[k:k + 1, cs] * glu_s[sh, pl.ds(r0 + k - half - sh, rc_rows), cs]
            ys.append(acc)
        mu = sum(jnp.sum(y, axis=-1, keepdims=True) for y in ys) / d_conv
        var = sum(jnp.sum((y - mu) * (y - mu), axis=-1, keepdims=True) for y in ys) / d_conv
        rs = lax.rsqrt(var + EPS)
        for cc in range(n_cc):
            cs = slice(cc * LANES, (cc + 1) * LANES)
            z = (ys[cc] - mu) * rs * lng_ref[:, cs] + lnb_ref[:, cs]
            mix_s[rc * rc_rows:(rc + 1) * rc_rows, cs] = _silu(z).astype(BF16)
        pos = seq_pos0 + rc * rc_rows + lax.broadcasted_iota(I32, (rc_rows, 1), 0)
        for gi, w in enumerate(POOL_WINDOWS):
            cs = slice(gi * LANES, (gi + 1) * LANES)
            ssum = pool_s[pl.ds(r0 - w // 2, rc_rows), cs]
            for j in range(1 - w // 2, w // 2):
                ssum = ssum + pool_s[pl.ds(r0 + j, rc_rows), cs]
            lo = jnp.clip(pos - w // 2, 0, seq_len)
            hi = jnp.clip(pos + w // 2, 0, seq_len)
            cnt = (hi - lo).astype(F32)
            res = ssum / cnt - pool_s[pl.ds(r0, rc_rows), cs]
            pres_s[rc * rc_rows:(rc + 1) * rc_rows, cs] = res.astype(BF16)

    yp = jnp.dot(pres_s[...], pw_ref[...], preferred_element_type=F32) * ps_ref[...]
    mix_s[:, d_conv:] = yp.astype(BF16)
    y = jnp.dot(mix_s[...], wout_ref[...], preferred_element_type=F32)
    x_new = xm + gate * y
    o_ref[0] = x_new
    _route_tile(x_new, mod, g2_ref, rwt_ref, rb_ref, tri_ref, h2_ref, grp_ref, rank_ref, cnt_ref,
                sc_s, carry_s)


def _even_layer(ctx, x, modtab, norm_g, w_in, dw_w, dw_b, ln_g, ln_b, pool_w, pool_scale, w_out,
                route):
    b, n_lat, d = x.shape
    assert ctx.shape[1] == TS
    s_tot = TS + n_lat
    nt = s_tot // TS
    r_args, r_in_specs, r_out_specs, r_out_shape, r_scratch = _route_plumbing(b, nt, d, *route)
    d_conv = dw_w.shape[1]
    d_pool = pool_scale.shape[0]
    assert d_pool == len(POOL_WINDOWS) * LANES and pool_w.shape[1] == LANES
    pool_bd = jnp.zeros((d_pool, d_pool), F32)
    for gi in range(len(POOL_WINDOWS)):
        pool_bd = pool_bd.at[gi * LANES:(gi + 1) * LANES, gi * LANES:(gi + 1) * LANES].set(pool_w[gi])
    hb = TS // HALO
    n_hblk = n_lat // HALO
    const2 = lambda bi, si: (0, 0)
    return pl.pallas_call(
        functools.partial(_even_kernel, n_lat=n_lat),
        grid=(b, nt),
        in_specs=[
            pl.BlockSpec((1, TS, d), lambda bi, si: (bi, 0, 0)),
            pl.BlockSpec((1, TS, d), lambda bi, si: (bi, jnp.maximum(si - 1, 0), 0)),
            pl.BlockSpec((1, HALO, d), lambda bi, si: (bi, jnp.maximum((si - 1) * hb - 1, 0), 0)),
            pl.BlockSpec((1, HALO, d), lambda bi, si: (bi, jnp.clip(si * hb, 0, n_hblk - 1), 0)),
            pl.BlockSpec((1, 1, N_MOD, d), lambda bi, si: (bi, jnp.minimum(si, 1), 0, 0)),
            pl.BlockSpec((1, d), const2),
            pl.BlockSpec(w_in.shape, const2),
            pl.BlockSpec(dw_w.shape, const2),
            pl.BlockSpec((1, d_conv), const2),
            pl.BlockSpec((1, d_conv), const2),
            pl.BlockSpec((1, d_conv), const2),
            pl.BlockSpec((d_pool, d_pool), const2),
            pl.BlockSpec((1, d_pool), const2),
            pl.BlockSpec(w_out.shape, const2),
        ] + r_in_specs,
        out_specs=[pl.BlockSpec((1, TS, d), lambda bi, si: (bi, si, 0))] + r_out_specs,
        out_shape=[jax.ShapeDtypeStruct((b, s_tot, d), F32)] + r_out_shape,
        scratch_shapes=[
            pltpu.VMEM((TS + 2 * HALO, d), BF16),
            pltpu.VMEM((SUBLANES, TS + 2 * HALO, d_conv), F32),
            pltpu.VMEM((TS + 2 * HALO, d_pool), F32),
            pltpu.VMEM((TS, d_pool), BF16),
            pltpu.VMEM((TS, d_conv + d_pool), BF16),
        ] + r_scratch,
        compiler_params=_cparams(2),
        name="even_mixer",
    )(ctx, x, x, x, modtab, norm_g.reshape(1, d), w_in.astype(BF16), dw_w, dw_b.reshape(1, -1),
      ln_g.reshape(1, -1), ln_b.reshape(1, -1), pool_bd.astype(BF16), pool_scale.reshape(1, -1),
      w_out.astype(BF16), *r_args)


def _route_tile(x, mod, g_ref, rwt_ref, rb_ref, tri_ref, h_ref, grp_ref, rank_ref, cnt_ref, sc_s,
                carry_s):
    first = jnp.logical_and(pl.program_id(0) == 0, pl.program_id(1) == 0)

    @pl.when(first)
    def _():
        carry_s[...] = jnp.zeros_like(carry_s)

    h = _norm_mod(x, g_ref[...], mod[3:4], mod[4:5])
    h_ref[0] = h
    logits = lax.dot_general(rwt_ref[...], h.astype(BF16), (((1,), (1,)), ((), ())),
                             preferred_element_type=F32)
    biased = jax.nn.sigmoid(logits) + rb_ref[...]
    n_half = TS // LANES
    for hh in range(n_half):
        sc_s[hh] = biased[:, hh * LANES:(hh + 1) * LANES]
    a, b, c, d = [
        jnp.concatenate([sc_s[hh, pl.ds(j, N_GROUPS, stride=EXPERTS_PER_GROUP), :]
                         for hh in range(n_half)], axis=-1)
        for j in range(EXPERTS_PER_GROUP)]
    top2 = jnp.maximum(jnp.maximum(jnp.maximum(a + b, a + c), jnp.maximum(a + d, b + c)),
                       jnp.maximum(b + d, c + d))
    gmax = jnp.max(top2, axis=0, keepdims=True)
    gi = lax.broadcasted_iota(I32, top2.shape, 0)
    sel = jnp.min(jnp.where(top2 == gmax, gi, N_GROUPS), axis=0, keepdims=True)
    onehot = gi == sel
    ohf = jnp.where(onehot, 1.0, 0.0)
    prefix = jnp.dot(ohf.astype(BF16), tri_ref[...], preferred_element_type=F32)
    carry = carry_s[:, 0:1]
    rank = jnp.sum(jnp.where(onehot, carry + prefix - 1.0, 0.0), axis=0, keepdims=True)
    carry_s[...] = carry_s[...] + jnp.sum(ohf, axis=1, keepdims=True)
    grp_ref[0] = sel
    rank_ref[0] = rank.astype(I32)
    cnt_ref[...] = carry_s[...]


def _route_plumbing(b, nt, d, norm_g, router_w, router_b):
    const2 = lambda bi, si: (0, 0)
    flat3 = lambda bi, si: (bi * nt + si, 0, 0)
    tri = (jnp.arange(TS)[:, None] <= jnp.arange(TS)[None, :]).astype(BF16)
    args = [norm_g.reshape(1, d), router_w.T.astype(BF16), router_b.reshape(N_EXPERTS, 1), tri]
    in_specs = [pl.BlockSpec((1, d), const2), pl.BlockSpec((N_EXPERTS, d), const2),
                pl.BlockSpec((N_EXPERTS, 1), const2), pl.BlockSpec((TS, TS), const2)]
    out_specs = [pl.BlockSpec((1, TS, d), lambda bi, si: (bi, si, 0)),
                 pl.BlockSpec((1, 1, TS), flat3), pl.BlockSpec((1, 1, TS), flat3),
                 pl.BlockSpec((N_GROUPS, LANES), const2)]
    out_shape = [jax.ShapeDtypeStruct((b, nt * TS, d), F32),
                 jax.ShapeDtypeStruct((b * nt, 1, TS), I32),
                 jax.ShapeDtypeStruct((b * nt, 1, TS), I32),
                 jax.ShapeDtypeStruct((N_GROUPS, LANES), F32)]
    scratch = [pltpu.VMEM((TS // LANES, N_EXPERTS, LANES), F32), pltpu.VMEM((N_GROUPS, LANES), F32)]
    return args, in_specs, out_specs, out_shape, scratch


def _dispatch_kernel(zrow_ref, pos_ref, src_ref, dst_ref, buf, ld_sem, st_sem, z_sem):
    i = pl.program_id(0)
    n = pl.num_programs(0)
    slot = lax.rem(i, DISPATCH_BUFS)
    nxt = lax.rem(i + 1, DISPATCH_BUFS)
    tile_groups = TS // SUBLANES

    def load(tile, which):
        return pltpu.make_async_copy(src_ref.at[pl.ds(tile * tile_groups, tile_groups)],
                                     buf.at[which], ld_sem.at[which])

    def wait_scatter(which):
        pltpu.make_async_copy(buf.at[which], dst_ref.at[pl.ds(0, tile_groups)],
                              st_sem.at[which]).wait()

    @pl.when(i == 0)
    def _():
        load(i, slot).start()
        zsrc = buf.at[DISPATCH_BUFS - 1]
        zsrc[...] = jnp.zeros(zsrc.shape, F32)
        for z in range(zrow_ref.shape[0]):
            @pl.when(zrow_ref[z] >= 0)
            def _():
                pltpu.make_async_copy(zsrc, dst_ref.at[pl.ds(zrow_ref[z], tile_groups)],
                                      z_sem).start()
        for z in range(zrow_ref.shape[0]):
            @pl.when(zrow_ref[z] >= 0)
            def _():
                pltpu.make_async_copy(zsrc, dst_ref.at[pl.ds(0, tile_groups)], z_sem).wait()

    load(i, slot).wait()

    @pl.when(i + 1 >= DISPATCH_BUFS)
    def _():
        wait_scatter(nxt)

    @pl.when(i + 1 < n)
    def _():
        load(i + 1, nxt).start()

    def issue(r8, carry):
        for j in range(SUBLANES):
            p = pos_ref[0, 0, r8 * SUBLANES + j]
            dst_row = dst_ref.at[lax.shift_right_logical(p, 3), pl.ds(p & (SUBLANES - 1), 1)]
            pltpu.make_async_copy(buf.at[slot, r8, pl.ds(j, 1)], dst_row, st_sem.at[slot]).start()
        return carry

    lax.fori_loop(0, tile_groups, issue, 0)

    @pl.when(i == n - 1)
    def _():
        for back in range(DISPATCH_BUFS - 1):
            @pl.when(i - back >= 0)
            def _():
                wait_scatter(lax.rem(i - back + DISPATCH_BUFS, DISPATCH_BUFS))


def _moe_dispatch(h_flat, pos, zero_rows, n_rows):
    t, d = h_flat.shape
    n_tiles = t // TS
    assert TS == TMG and n_tiles >= DISPATCH_BUFS
    zero_groups = jnp.where(zero_rows >= 0, zero_rows // SUBLANES, -1).astype(I32)
    return pl.pallas_call(
        _dispatch_kernel,
        grid=(n_tiles,),
        in_specs=[
            pl.BlockSpec(memory_space=pltpu.SMEM),
            pl.BlockSpec((1, 1, TS), lambda i: (i, 0, 0), memory_space=pltpu.SMEM),
            pl.BlockSpec(memory_space=pl.ANY),
        ],
        out_specs=pl.BlockSpec(memory_space=pl.ANY),
        out_shape=jax.ShapeDtypeStruct((n_rows // SUBLANES, SUBLANES, d), F32),
        scratch_shapes=[pltpu.VMEM((DISPATCH_BUFS, TS // SUBLANES, SUBLANES, d), F32),
                        pltpu.SemaphoreType.DMA((DISPATCH_BUFS,)),
                        pltpu.SemaphoreType.DMA((DISPATCH_BUFS,)),
                        pltpu.SemaphoreType.DMA(())],
        compiler_params=_cparams(1),
        name="moe_dispatch",
    )(zero_groups, pos, h_flat.reshape(t // SUBLANES, SUBLANES, d))


def _ffn_kernel(tg_ref, tv_ref, xs_ref, rw_ref, rb_ref, wg32_ref, wu32_ref, wd32_ref, o_ref,
                wg_ref, wu_ref, wd_ref):
    i = pl.program_id(0)

    @pl.when(jnp.logical_or(i == 0, tg_ref[i] != tg_ref[jnp.maximum(i - 1, 0)]))
    def _():
        wg_ref[...] = wg32_ref[...].astype(BF16)
        wu_ref[...] = wu32_ref[...].astype(BF16)
        wd_ref[...] = wd32_ref[...].astype(BF16)

    @pl.when(tv_ref[i] == 0)
    def _():
        o_ref[...] = jnp.zeros_like(o_ref)

    @pl.when(tv_ref[i] != 0)
    def _():
        xb = xs_ref[...].astype(BF16)
        logits = jnp.dot(xb, rw_ref[0], preferred_element_type=F32)
        sc = jax.nn.sigmoid(logits)
        bs = sc + rb_ref[0]
        s_col = [sc[:, j:j + 1] for j in range(EXPERTS_PER_GROUP)]
        b_col = [bs[:, j:j + 1] for j in range(EXPERTS_PER_GROUP)]
        sel = []
        for j in range(EXPERTS_PER_GROUP):
            beaten = jnp.zeros_like(b_col[j])
            for k in range(EXPERTS_PER_GROUP):
                if k == j:
                    continue
                wins = (b_col[k] >= b_col[j]) if k < j else (b_col[k] > b_col[j])
                beaten = beaten + jnp.where(wins, 1.0, 0.0)
            sel.append(beaten < 2.0)
        den = sum(jnp.where(sel[j], s_col[j], 0.0) for j in range(EXPERTS_PER_GROUP))
        hes = []
        for j in range(EXPERTS_PER_GROUP):
            cj = jnp.where(sel[j], s_col[j] / den, 0.0)
            gj = jnp.dot(xb, wg_ref[0, j], preferred_element_type=F32)
            uj = jnp.dot(xb, wu_ref[0, j], preferred_element_type=F32)
            hes.append((_silu(gj) * uj * cj).astype(BF16))
        he = jnp.concatenate(hes, axis=-1)
        d_e = wd_ref.shape[2]
        wd = wd_ref[0].reshape(EXPERTS_PER_GROUP * d_e, wd_ref.shape[3])
        o_ref[...] = jnp.dot(he, wd, preferred_element_type=F32)


def _moe_ffn(xs, tile_group, tile_valid, rw_g, rb_g, w_gate, w_up, w_down, layer):
    n_rows, d = xs.shape
    d_e = w_gate.shape[-1]
    n_tiles = n_rows // TMG
    epg = EXPERTS_PER_GROUP
    grid_spec = pltpu.PrefetchScalarGridSpec(
        num_scalar_prefetch=2,
        grid=(n_tiles,),
        in_specs=[
            pl.BlockSpec((TMG, d), lambda i, tg, tv: (i, 0)),
            pl.BlockSpec((1, d, LANES), lambda i, tg, tv: (tg[i], 0, 0)),
            pl.BlockSpec((1, 1, LANES), lambda i, tg, tv: (tg[i], 0, 0)),
            pl.BlockSpec((1, epg, d, d_e), lambda i, tg, tv: (layer, tg[i], 0, 0)),
            pl.BlockSpec((1, epg, d, d_e), lambda i, tg, tv: (layer, tg[i], 0, 0)),
            pl.BlockSpec((1, epg, d_e, d), lambda i, tg, tv: (layer, tg[i], 0, 0)),
        ],
        out_specs=pl.BlockSpec((TMG, d), lambda i, tg, tv: (i, 0)),
        scratch_shapes=[pltpu.VMEM((1, epg, d, d_e), BF16), pltpu.VMEM((1, epg, d, d_e), BF16),
                        pltpu.VMEM((1, epg, d_e, d), BF16)],
    )
    return pl.pallas_call(
        _ffn_kernel,
        grid_spec=grid_spec,
        out_shape=jax.ShapeDtypeStruct((n_rows, d), F32),
        compiler_params=_cparams(1),
        name="moe_ffn",
    )(tile_group, tile_valid, xs, rw_g, rb_g, w_gate, w_up, w_down)


def _combine_kernel(pos_ref, posn_ref, x_ref, mod_ref, ys_ref, o_ref, buf, sem):
    i = pl.program_id(0) * pl.num_programs(1) + pl.program_id(1)
    n = pl.num_programs(0) * pl.num_programs(1)
    slot = lax.rem(i, 2)

    tile_groups = TS // SUBLANES

    def gather(p_ref, which):
        def issue(r8, carry):
            for j in range(SUBLANES):
                p = p_ref[0, 0, r8 * SUBLANES + j]
                src_row = ys_ref.at[lax.shift_right_logical(p, 3), pl.ds(p & (SUBLANES - 1), 1)]
                pltpu.make_async_copy(src_row, buf.at[which, r8, pl.ds(j, 1)], sem.at[which]).start()
            return carry
        lax.fori_loop(0, tile_groups, issue, 0)

    @pl.when(i == 0)
    def _():
        gather(pos_ref, slot)

    @pl.when(i + 1 < n)
    def _():
        gather(posn_ref, 1 - slot)

    pltpu.make_async_copy(ys_ref.at[pl.ds(0, tile_groups)], buf.at[slot], sem.at[slot]).wait()
    gate = mod_ref[0, 0][5:6]
    o_ref[0] = x_ref[0] + gate * buf[slot].reshape(TS, buf.shape[-1])


def _moe_combine(x, modtab, pos, ys, kind_of_tile, tile_off, n_out_tiles):
    b, s_tot, d = x.shape
    nt = s_tot // TS
    n_tiles = b * nt
    def cur(bi, si):
        return (bi * nt + si + tile_off, 0, 0)
    def nxt(bi, si):
        last = si == n_out_tiles - 1
        nb = jnp.where(last, bi + 1, bi)
        ns = jnp.where(last, 0, si + 1)
        return (jnp.minimum(nb * nt + ns + tile_off, n_tiles - 1), 0, 0)
    return pl.pallas_call(
        _combine_kernel,
        grid=(b, n_out_tiles),
        in_specs=[
            pl.BlockSpec((1, 1, TS), cur, memory_space=pltpu.SMEM),
            pl.BlockSpec((1, 1, TS), nxt, memory_space=pltpu.SMEM),
            pl.BlockSpec((1, TS, d), lambda bi, si: (bi, si + tile_off, 0)),
            pl.BlockSpec((1, 1, N_MOD, d), lambda bi, si: (bi, kind_of_tile(si + tile_off), 0, 0)),
            pl.BlockSpec(memory_space=pl.ANY),
        ],
        out_specs=pl.BlockSpec((1, TS, d), lambda bi, si: (bi, si, 0)),
        out_shape=jax.ShapeDtypeStruct((b, n_out_tiles * TS, d), F32),
        scratch_shapes=[pltpu.VMEM((2, TS // SUBLANES, SUBLANES, d), F32),
                        pltpu.SemaphoreType.DMA((2,))],
        compiler_params=_cparams(2),
        name="moe_combine",
    )(pos, pos, x, modtab, ys.reshape(ys.shape[0] // SUBLANES, SUBLANES, d))


def _moe_layer(x, routed, modtab, router_w, router_b, w_gate, w_up, w_down, layer, kind_of_tile,
               tile_off, n_out_tiles):
    b, s_tot, d = x.shape
    t = b * s_tot
    h, grp, rank, cnt = routed
    counts = cnt[:, 0].astype(I32)
    padded = ((counts + TMG - 1) // TMG) * TMG
    ends = jnp.cumsum(padded)
    starts = ends - padded
    pos = starts[grp] + rank
    n_rows = t + N_GROUPS * TMG
    tile_row0 = jnp.arange(n_rows // TMG, dtype=I32) * TMG
    tile_group = jnp.minimum(jnp.sum(tile_row0[:, None] >= ends[None, :], axis=1),
                             N_GROUPS - 1).astype(I32)
    tile_valid = (tile_row0 < ends[-1]).astype(I32)
    tail_rows = jnp.minimum(ends[-1] + jnp.arange(N_GROUPS, dtype=I32) * TMG, n_rows - TMG)
    zero_rows = jnp.concatenate([jnp.where(padded > 0, ends - TMG, tail_rows[-1]), tail_rows])
    n_z = zero_rows.shape[0]
    repeat = jnp.any((zero_rows[:, None] == zero_rows[None, :])
                     & (jnp.arange(n_z)[None, :] < jnp.arange(n_z)[:, None]), axis=1)
    zero_rows = jnp.where(repeat, -1, zero_rows)
    xs = _moe_dispatch(h.reshape(t, d), pos, zero_rows.astype(I32), n_rows)
    epg = EXPERTS_PER_GROUP
    rw_g = jnp.pad(router_w.reshape(d, N_GROUPS, epg).transpose(1, 0, 2),
                   ((0, 0), (0, 0), (0, LANES - epg))).astype(BF16)
    rb_g = jnp.pad(router_b.reshape(N_GROUPS, 1, epg), ((0, 0), (0, 0), (0, LANES - epg)))
    ys = _moe_ffn(xs.reshape(n_rows, d), tile_group, tile_valid, rw_g, rb_g, w_gate, w_up, w_down,
                  layer)
    return _moe_combine(x, modtab, pos, ys, kind_of_tile, tile_off, n_out_tiles)


def _rope_pad(w, axis):
    half = QK_ROPE // 2
    x1, x2 = jnp.split(w, 2, axis=axis)
    z = jnp.zeros_like(x1)
    del half
    return jnp.concatenate([x1, z, x2, z], axis=axis)


def _oddproj_kernel(x_ref, mod_ref, g_ref, win_ref, gcq_ref, wuq_ref, gckv_ref, wukv_ref, gqm_ref,
                    gkm_ref, gqd_ref, gkd_ref, cs_ref, sna_ref, snb_ref, csm_ref, snm_ref,
                    qm_ref, km_ref, vm_ref, qd_ref, kd_ref, vd_ref, p_s, qm_s, kv_s):
    mod = mod_ref[0, 0]
    h = _norm_mod(x_ref[0], g_ref[...], mod[0:1], mod[1:2]).astype(BF16)
    q_lora = gcq_ref.shape[1]
    kv_lora = gckv_ref.shape[1]
    n_qd = DIFF_HEADS * 2 * DIFF_QK
    o_qd = q_lora
    o_ckv = o_qd + n_qd
    o_kr = o_ckv + kv_lora
    o_kd = o_kr + LANES
    o_vd = o_kd + n_qd
    lane = lax.broadcasted_iota(I32, (1, LANES), 1)
    low = lane < DIFF_QK
    d_mla = QK_NOPE + QK_ROPE

    def rms(x, g, n):
        return x * lax.rsqrt(jnp.sum(x * x, axis=-1, keepdims=True) / n + EPS) * g

    def rope_mla(y):
        return y * csm_ref[...] + pltpu.roll(y, LANES // 2, 1) * snm_ref[...]

    def diff_cols(p, g):
        sq = p * p
        s_all = jnp.sum(sq, axis=-1, keepdims=True)
        s_lo = jnp.sum(jnp.where(low, sq, 0.0), axis=-1, keepdims=True)
        r = jnp.where(low, lax.rsqrt(s_lo / DIFF_QK + EPS),
                      lax.rsqrt((s_all - s_lo) / DIFF_QK + EPS))
        y = p * r * g
        return (y * cs_ref[...] + pltpu.roll(y, LANES - DIFF_QK // 2, 1) * sna_ref[...]
                + pltpu.roll(y, DIFF_QK // 2, 1) * snb_ref[...])

    p_s[...] = jnp.dot(h, win_ref[...], preferred_element_type=F32)
    cq = rms(p_s[:, 0:q_lora], gcq_ref[...], q_lora)
    qm_s[...] = jnp.dot(cq.astype(BF16), wuq_ref[...], preferred_element_type=F32)
    ckv = rms(p_s[:, o_ckv:o_ckv + kv_lora], gckv_ref[...], kv_lora)
    kv_s[...] = jnp.dot(ckv.astype(BF16), wukv_ref[...], preferred_element_type=F32)

    for hd in range(DIFF_HEADS):
        cs = slice(LANES * hd, LANES * (hd + 1))
        y = diff_cols(p_s[:, o_qd + LANES * hd:o_qd + LANES * (hd + 1)], gqd_ref[...])
        y = y * (DIFF_SCALE * LOG2E)
        qd_ref[0, 2 * hd] = jnp.where(low, y, 0.0).astype(BF16)
        qd_ref[0, 2 * hd + 1] = jnp.where(low, 0.0, y).astype(BF16)
        kd_ref[0, hd] = diff_cols(p_s[:, o_kd + LANES * hd:o_kd + LANES * (hd + 1)],
                                  gkd_ref[...]).astype(BF16)
        vd_ref[0, hd] = p_s[:, o_vd + DIFF_V * hd:o_vd + DIFF_V * (hd + 1)].T.astype(BF16)
        del cs

    kr = p_s[:, o_kr:o_kr + LANES]
    for hd in range(MLA_HEADS):
        qh = rms(qm_s[:, 2 * LANES * hd:2 * LANES * (hd + 1)], gqm_ref[...], d_mla)
        qh = jnp.concatenate([qh[:, :LANES], rope_mla(qh[:, LANES:])], axis=-1)
        qm_ref[0, hd] = (qh * (MLA_SCALE * LOG2E)).astype(BF16)
        kcat = jnp.concatenate([kv_s[:, 2 * LANES * hd:2 * LANES * hd + LANES], kr], axis=-1)
        kh = rms(kcat, gkm_ref[...], d_mla)
        km_ref[0, hd] = jnp.concatenate([kh[:, :LANES], rope_mla(kh[:, LANES:])], axis=-1).astype(BF16)
        vm_ref[0, hd] = kv_s[:, 2 * LANES * hd + LANES:2 * LANES * (hd + 1)].T.astype(BF16)


def _odd_project(xa, modtab, norm_g, w_in, g_cq, w_uq, g_ckv, w_ukv, g_mla, g_diff, n_ctx):
    b, s_tot, d = xa.shape
    nt = s_tot // TS
    n_lat = s_tot - n_ctx
    q_lora = g_cq.shape[0]
    kv_lora = g_ckv.shape[0]
    n_qd = DIFF_HEADS * 2 * DIFF_QK
    d_mla = QK_NOPE + QK_ROPE
    o = 0
    w_cq = w_in[:, o:o + q_lora]; o += q_lora
    w_qd = w_in[:, o:o + n_qd]; o += n_qd
    w_ckv = w_in[:, o:o + kv_lora]; o += kv_lora
    w_kr = w_in[:, o:o + QK_ROPE]; o += QK_ROPE
    w_kd = w_in[:, o:o + n_qd]; o += n_qd
    w_vd = w_in[:, o:]
    w_in_p = jnp.concatenate([w_cq, w_qd, w_ckv, _rope_pad(w_kr, 1), w_kd, w_vd], axis=1).astype(BF16)
    wuq = w_uq.reshape(q_lora, MLA_HEADS, d_mla)
    wuq_p = jnp.concatenate([wuq[..., :QK_NOPE], _rope_pad(wuq[..., QK_NOPE:], 2)],
                            axis=-1).reshape(q_lora, MLA_HEADS * 2 * LANES).astype(BF16)

    def pad_gain(g):
        return jnp.concatenate([g[:QK_NOPE], _rope_pad(g[QK_NOPE:], 0)]).reshape(1, 2 * LANES)

    rows = n_lat // GRID_W
    row = jnp.repeat(jnp.arange(rows, dtype=F32), GRID_W)
    col = jnp.tile(jnp.arange(GRID_W, dtype=F32), rows)
    axis_dim = QK_ROPE // 2
    inv_freq = ROPE_BASE ** (-jnp.arange(0, axis_dim, 2, dtype=F32) / axis_dim)
    ang = jnp.concatenate([row[:, None] * inv_freq, col[:, None] * inv_freq], axis=-1)
    half = QK_ROPE // 2
    cos = jnp.concatenate([jnp.ones((n_ctx, half), F32), jnp.cos(ang)], axis=0)
    sin = jnp.concatenate([jnp.zeros((n_ctx, half), F32), jnp.sin(ang)], axis=0)
    z = jnp.zeros_like(sin)
    cs_d = jnp.tile(cos, (1, LANES // half))
    sna_d = jnp.tile(jnp.concatenate([-sin, z], axis=1), (1, LANES // QK_ROPE))
    snb_d = jnp.tile(jnp.concatenate([z, sin], axis=1), (1, LANES // QK_ROPE))
    cs_m = jnp.concatenate([cos, z, cos, z], axis=1)
    sn_m = jnp.concatenate([-sin, z, sin, z], axis=1)

    const2 = lambda bi, si: (0, 0)
    tok = lambda bi, si: (si, 0)
    kv4 = lambda bi, si: (bi, 0, si, 0)
    vt4 = lambda bi, si: (bi, 0, 0, si)
    q4 = lambda bi, si: (bi, 0, jnp.maximum(si - n_ctx // TS, 0), 0)
    full = lambda a: pl.BlockSpec(a.shape, const2)
    gq_d = jnp.tile(g_diff[0], LANES // DIFF_QK).reshape(1, LANES)
    gk_d = jnp.tile(g_diff[1], LANES // DIFF_QK).reshape(1, LANES)
    weights = [norm_g.reshape(1, d), w_in_p, g_cq.reshape(1, -1), wuq_p, g_ckv.reshape(1, -1),
               w_ukv.astype(BF16), pad_gain(g_mla[0]), pad_gain(g_mla[1]), gq_d, gk_d]
    args = [xa, modtab, *weights, cs_d, sna_d, snb_d, cs_m, sn_m]
    in_specs = [
        pl.BlockSpec((1, TS, d), lambda bi, si: (bi, si, 0)),
        pl.BlockSpec((1, 1, N_MOD, d), lambda bi, si: (bi, jnp.minimum(si, 1), 0, 0)),
    ] + [full(a) for a in weights] + [pl.BlockSpec((TS, LANES), tok)] * 5
    hm, hd = MLA_HEADS, DIFF_HEADS
    return pl.pallas_call(
        _oddproj_kernel,
        grid=(b, nt),
        in_specs=in_specs,
        out_specs=[
            pl.BlockSpec((1, hm, TS, 2 * LANES), q4),
            pl.BlockSpec((1, hm, TS, 2 * LANES), kv4),
            pl.BlockSpec((1, hm, V_HEAD, TS), vt4),
            pl.BlockSpec((1, 2 * hd, TS, LANES), q4),
            pl.BlockSpec((1, hd, TS, LANES), kv4),
            pl.BlockSpec((1, hd, DIFF_V, TS), vt4),
        ],
        out_shape=[
            jax.ShapeDtypeStruct((b, hm, n_lat, 2 * LANES), BF16),
            jax.ShapeDtypeStruct((b, hm, s_tot, 2 * LANES), BF16),
            jax.ShapeDtypeStruct((b, hm, V_HEAD, s_tot), BF16),
            jax.ShapeDtypeStruct((b, 2 * hd, n_lat, LANES), BF16),
            jax.ShapeDtypeStruct((b, hd, s_tot, LANES), BF16),
            jax.ShapeDtypeStruct((b, hd, DIFF_V, s_tot), BF16),
        ],
        scratch_shapes=[pltpu.VMEM((TS, w_in_p.shape[1]), F32),
                        pltpu.VMEM((TS, wuq_p.shape[1]), F32),
                        pltpu.VMEM((TS, w_ukv.shape[1]), F32)],
        compiler_params=_cparams(2),
        name="odd_project",
    )(*args)


def _attn_kernel(q_ref, k_ref, vt_ref, o_ref, s_s, p_s, *, rep):
    hq = q_ref.shape[1]
    n_items = hq * (q_ref.shape[2] // TQ)
    n_chunks = k_ref.shape[2] // TK
    groups = TK // SUBLANES

    def q_rows(i):
        return pl.ds((i // hq) * TQ, TQ)

    def scores(i):
        u, h = i % 2, i % hq
        s_s[u] = lax.dot_general(k_ref[0, h // rep], q_ref[0, h, q_rows(i), :],
                                 (((1,), (1,)), ((), ())), preferred_element_type=F32)
        m8 = jnp.full((SUBLANES, TQ), -jnp.inf, F32)
        for c in range(n_chunks):
            st = s_s[u, pl.ds(c * TK, TK), :]
            m8 = jnp.maximum(m8, jnp.max(st.reshape(groups, SUBLANES, TQ), axis=0))
        return jnp.max(m8, axis=0, keepdims=True)

    def exponentials(i, row_max):
        u = i % 2
        l8 = jnp.zeros((SUBLANES, TQ), F32)
        for c in range(n_chunks):
            p = jnp.exp2(s_s[u, pl.ds(c * TK, TK), :] - row_max)
            l8 = l8 + jnp.sum(p.reshape(groups, SUBLANES, TQ), axis=0)
            p_s[u, pl.ds(c * TK, TK), :] = p.astype(BF16)
        return jnp.sum(l8, axis=0, keepdims=True)

    def values(i, l):
        h = i % hq
        out_t = jnp.dot(vt_ref[0, h // rep], p_s[i % 2], preferred_element_type=F32)
        o_ref[0, h, q_rows(i), :] = (out_t / l).T.astype(o_ref.dtype)

    row_max, row_sum = {}, {}
    for t in range(n_items + 2):
        if t < n_items:
            row_max[t] = scores(t)
        if 0 <= t - 1 < n_items:
            row_sum[t - 1] = exponentials(t - 1, row_max.pop(t - 1))
        if 0 <= t - 2 < n_items:
            values(t - 2, row_sum.pop(t - 2))


def _diff_attn_kernel(q_ref, k_ref, vt_ref, dl_ref, o_ref, s_s, p_s, a1_s, *, lam_init):
    hq = q_ref.shape[1]
    n_items = hq * (q_ref.shape[2] // TQ)
    n_chunks = k_ref.shape[2] // TK
    groups = TK // SUBLANES
    dl = dl_ref[...]
    lam = (jnp.exp(jnp.sum(dl[0:1] * dl[1:2], axis=-1, keepdims=True))
           - jnp.exp(jnp.sum(dl[2:3] * dl[3:4], axis=-1, keepdims=True)) + lam_init)

    def q_rows(i):
        return pl.ds((i // hq) * TQ, TQ)

    def scores(i):
        u, slot = i % 2, i % hq
        s_s[u] = lax.dot_general(k_ref[0, slot // 2], q_ref[0, slot, q_rows(i), :],
                                 (((1,), (1,)), ((), ())), preferred_element_type=F32)
        m8 = jnp.full((SUBLANES, TQ), -jnp.inf, F32)
        for c in range(n_chunks):
            st = s_s[u, pl.ds(c * TK, TK), :]
            m8 = jnp.maximum(m8, jnp.max(st.reshape(groups, SUBLANES, TQ), axis=0))
        return jnp.max(m8, axis=0, keepdims=True)

    def exponentials(i, row_max):
        u = i % 2
        l8 = jnp.zeros((SUBLANES, TQ), F32)
        for c in range(n_chunks):
            p = jnp.exp2(s_s[u, pl.ds(c * TK, TK), :] - row_max)
            l8 = l8 + jnp.sum(p.reshape(groups, SUBLANES, TQ), axis=0)
            p_s[u, pl.ds(c * TK, TK), :] = p.astype(BF16)
        return jnp.sum(l8, axis=0, keepdims=True)

    def values(i, l):
        slot = i % hq
        a = jnp.dot(vt_ref[0, slot // 2], p_s[i % 2], preferred_element_type=F32) / l
        if slot % 2 == 0:
            a1_s[...] = a
        else:
            o_ref[0, slot // 2, q_rows(i), :] = (a1_s[...] - lam * a).T

    row_max, row_sum = {}, {}
    for t in range(n_items + 2):
        if t < n_items:
            row_max[t] = scores(t)
        if 0 <= t - 1 < n_items:
            row_sum[t - 1] = exponentials(t - 1, row_max.pop(t - 1))
        if 0 <= t - 2 < n_items:
            values(t - 2, row_sum.pop(t - 2))


def _diff_attention(q, k, vt, diff_lambda, lam_init, q_tiles):
    b, hq, lq, dk = q.shape
    hk, lk = k.shape[1], k.shape[2]
    dv = vt.shape[2]
    tq_step = q_tiles * TQ
    assert lk % TK == 0 and lq % tq_step == 0 and hq == 2 * hk
    once = pl.Buffered(1)
    return pl.pallas_call(
        functools.partial(_diff_attn_kernel, lam_init=lam_init),
        grid=(b, lq // tq_step),
        in_specs=[
            pl.BlockSpec((1, hq, tq_step, dk), lambda bi, qi: (bi, 0, qi, 0)),
            pl.BlockSpec((1, hk, lk, dk), lambda bi, qi: (bi, 0, 0, 0), pipeline_mode=once),
            pl.BlockSpec((1, hk, dv, lk), lambda bi, qi: (bi, 0, 0, 0), pipeline_mode=once),
            pl.BlockSpec(diff_lambda.shape, lambda bi, qi: (0, 0)),
        ],
        out_specs=pl.BlockSpec((1, hk, tq_step, dv), lambda bi, qi: (bi, 0, qi, 0)),
        out_shape=jax.ShapeDtypeStruct((b, hk, lq, dv), F32),
        scratch_shapes=[pltpu.VMEM((2, lk, TQ), F32), pltpu.VMEM((2, lk, TQ), BF16),
                        pltpu.VMEM((dv, TQ), F32)],
        compiler_params=_cparams(2),
        name="attn_diff",
    )(q, k, vt, diff_lambda)


def _attention(q, k, vt, out_dtype, name, q_tiles):
    b, hq, lq, dk = q.shape
    hk, lk = k.shape[1], k.shape[2]
    dv = vt.shape[2]
    tq_step = q_tiles * TQ
    assert lk % TK == 0 and lq % tq_step == 0
    once = pl.Buffered(1)
    return pl.pallas_call(
        functools.partial(_attn_kernel, rep=hq // hk),
        grid=(b, lq // tq_step),
        in_specs=[
            pl.BlockSpec((1, hq, tq_step, dk), lambda bi, qi: (bi, 0, qi, 0)),
            pl.BlockSpec((1, hk, lk, dk), lambda bi, qi: (bi, 0, 0, 0), pipeline_mode=once),
            pl.BlockSpec((1, hk, dv, lk), lambda bi, qi: (bi, 0, 0, 0), pipeline_mode=once),
        ],
        out_specs=pl.BlockSpec((1, hq, tq_step, dv), lambda bi, qi: (bi, 0, qi, 0)),
        out_shape=jax.ShapeDtypeStruct((b, hq, lq, dv), out_dtype),
        scratch_shapes=[pltpu.VMEM((2, lk, TQ), F32), pltpu.VMEM((2, lk, TQ), BF16)],
        compiler_params=_cparams(2),
        name=name,
    )(q, k, vt)


def _oddout_kernel(x_ref, mod_ref, om_ref, ad_ref, gs_ref, wout_ref, g2_ref, rwt_ref,
                   rb_ref, tri_ref, o_ref, h2_ref, grp_ref, rank_ref, cnt_ref, sc_s, carry_s, *,
                   lam_init):
    parts = [om_ref[0, hd] for hd in range(MLA_HEADS)]
    for hd in range(DIFF_HEADS):
        df = ad_ref[0, hd]
        od = df * lax.rsqrt(jnp.mean(df * df, axis=-1, keepdims=True) + EPS) * gs_ref[...]
        parts.append((od * (1.0 - lam_init)).astype(BF16))
    y = jnp.dot(jnp.concatenate(parts, axis=-1), wout_ref[...], preferred_element_type=F32)
    mod = mod_ref[0, 0]
    x_new = x_ref[0] + mod[2:3] * y
    o_ref[0] = x_new
    _route_tile(x_new, mod, g2_ref, rwt_ref, rb_ref, tri_ref, h2_ref, grp_ref, rank_ref, cnt_ref,
                sc_s, carry_s)


def _odd_output(xa, modtab, o_m, a_d, g_sub, w_out, lam_init, n_ctx, route):
    b, s_tot, d = xa.shape
    n_lat = s_tot - n_ctx
    off = n_ctx // TS
    const2 = lambda bi, si: (0, 0)
    r_args, r_in_specs, r_out_specs, r_out_shape, r_scratch = _route_plumbing(
        b, n_lat // TS, d, *route)
    return pl.pallas_call(
        functools.partial(_oddout_kernel, lam_init=lam_init),
        grid=(b, n_lat // TS),
        in_specs=[
            pl.BlockSpec((1, TS, d), lambda bi, si: (bi, si + off, 0)),
            pl.BlockSpec((1, 1, N_MOD, d), lambda bi, si: (bi, 1, 0, 0)),
            pl.BlockSpec((1, o_m.shape[1], TS, V_HEAD), lambda bi, si: (bi, 0, si, 0)),
            pl.BlockSpec((1, a_d.shape[1], TS, DIFF_V), lambda bi, si: (bi, 0, si, 0)),
            pl.BlockSpec((1, DIFF_V), const2),
            pl.BlockSpec(w_out.shape, const2),
        ] + r_in_specs,
        out_specs=[pl.BlockSpec((1, TS, d), lambda bi, si: (bi, si, 0))] + r_out_specs,
        out_shape=[jax.ShapeDtypeStruct((b, n_lat, d), F32)] + r_out_shape,
        scratch_shapes=r_scratch,
        compiler_params=_cparams(2),
        name="odd_output",
    )(xa, modtab, o_m, a_d, g_sub.reshape(1, -1), w_out.astype(BF16), *r_args)


def kernel(x, c, ctx, c_ctx, mod_w, mod_b, norm_g, even_w_in, conv_dw_w, conv_dw_b, conv_ln_g,
           conv_ln_b, pool_w, pool_scale, even_w_out, odd_w_in, mla_g_cq, mla_w_uq, mla_g_ckv,
           mla_w_ukv, qk_g_mla, qk_g_diff, diff_lambda, diff_sub_g, odd_w_out, router_w, router_b,
           moe_w_gate, moe_w_up, moe_w_down):
    b, n_lat, d = x.shape
    n_ctx = ctx.shape[1]
    assert n_ctx == TS and n_lat % TS == 0 and mod_w.shape[0] == 2
    nt = (n_ctx + n_lat) // TS

    c_rows = jnp.concatenate([c, c_ctx[None], jnp.zeros((16 - b - 1, d), F32)], axis=0)
    mods = _modulation(c_rows, mod_w, mod_b)
    modtabs = []
    for i in range(2):
        mod_l = mods[i, :b].reshape(b, N_MOD, d)
        mod_c = jnp.broadcast_to(mods[i, b].reshape(1, N_MOD, d), (b, N_MOD, d))
        modtabs.append(jnp.stack([mod_c, mod_l], axis=1))

    ctx_or_lat = lambda si: jnp.minimum(si, 1)
    lat_only = lambda si: 1

    xa, *routed = _even_layer(ctx, x, modtabs[0], norm_g[0, 0], even_w_in[0], conv_dw_w[0],
                              conv_dw_b[0], conv_ln_g[0], conv_ln_b[0], pool_w[0], pool_scale[0],
                              even_w_out[0], (norm_g[0, 1], router_w, router_b))
    xa = _moe_layer(xa, routed, modtabs[0], router_w, router_b, moe_w_gate, moe_w_up, moe_w_down,
                    0, ctx_or_lat, 0, nt)

    qm, km, vm, qd, kd, vd = _odd_project(xa, modtabs[1], norm_g[1, 0], odd_w_in[0], mla_g_cq[0],
                                          mla_w_uq[0], mla_g_ckv[0], mla_w_ukv[0], qk_g_mla[0],
                                          qk_g_diff[0], n_ctx)
    lam_init = 0.8 - 0.6 * math.exp(-0.3 * 1)
    q_tiles = min(ATTN_Q_TILES, n_lat // TQ)
    o_m = _attention(qm, km, vm, BF16, "attn_mla", q_tiles)
    a_d = _diff_attention(qd, kd, vd, diff_lambda[0], lam_init, max(q_tiles // 2, 1))
    xl, *routed = _odd_output(xa, modtabs[1], o_m, a_d, diff_sub_g[0], odd_w_out[0], lam_init,
                              n_ctx, (norm_g[1, 1], router_w, router_b))
    return _moe_layer(xl, routed, modtabs[1], router_w, router_b, moe_w_gate, moe_w_up,
                      moe_w_down, 1, lat_only, 0, n_lat // TS)
```

```python
import functools
import math

import jax
import jax.numpy as jnp
from jax import lax
from jax.experimental import pallas as pl
from jax.experimental.pallas import tpu as pltpu

F32 = jnp.float32
BF16 = jnp.bfloat16
I32 = jnp.int32

GRID_W = 64
N_MOD = 6
EPS = 1e-6
CONV_WIDTH = 31
POOL_WINDOWS = (2, 4, 8, 16)
MLA_HEADS = 4
QK_NOPE = 128
QK_ROPE = 64
V_HEAD = 128
DIFF_HEADS = 4
DIFF_QK = 64
DIFF_V = 128
MLA_SCALE = (QK_NOPE + QK_ROPE) ** -0.5
DIFF_SCALE = DIFF_QK ** -0.5
ROPE_BASE = 10000.0
LOG2E = math.log2(math.e)
N_EXPERTS = 32
N_GROUPS = 8
EXPERTS_PER_GROUP = N_EXPERTS // N_GROUPS

LANES = 128
SUBLANES = 8
TS = 256
HALO = 16
TMG = 256
TQ = 256
TK = 256
ATTN_Q_TILES = 4
DISPATCH_BUFS = 3
VMEM_LIMIT = 48 * 1024 * 1024


def _cparams(n_axes):
    return pltpu.CompilerParams(
        dimension_semantics=("arbitrary",) * n_axes, vmem_limit_bytes=VMEM_LIMIT)


def _norm_mod(x, g, shift, scale):
    ms = jnp.mean(x * x, axis=-1, keepdims=True)
    return (x * lax.rsqrt(ms + EPS) * g) * (1.0 + scale) + shift


def _silu(x):
    return x * jax.nn.sigmoid(x)


def _mod_kernel(c_ref, w_ref, b_ref, o_ref):
    a = _silu(c_ref[...])
    o_ref[0] = jnp.dot(a.astype(BF16), w_ref[0].astype(BF16),
                       preferred_element_type=F32) + b_ref[0]


def _modulation(c_rows, mod_w, mod_b):
    depth, d, n = mod_w.shape
    tn = 1536
    rows = c_rows.shape[0]
    return pl.pallas_call(
        _mod_kernel,
        grid=(depth, n // tn),
        in_specs=[
            pl.BlockSpec((rows, d), lambda l, j: (0, 0)),
            pl.BlockSpec((1, d, tn), lambda l, j: (l, 0, j)),
            pl.BlockSpec((1, 1, tn), lambda l, j: (l, 0, j)),
        ],
        out_specs=pl.BlockSpec((1, rows, tn), lambda l, j: (l, 0, j)),
        out_shape=jax.ShapeDtypeStruct((depth, rows, n), F32),
        compiler_params=_cparams(2),
        name="modulation",
    )(c_rows, mod_w, mod_b.reshape(depth, 1, n))


def _even_kernel(xc_ref, xl_ref, xp_ref, xn_ref, mod_ref, g_ref, win_ref, dww_ref, dwb_ref, lng_ref,
                 lnb_ref, pw_ref, ps_ref, wout_ref, g2_ref, rwt_ref, rb_ref, tri_ref,
                 o_ref, h2_ref, grp_ref, rank_ref, cnt_ref,
                 h_s, glu_s, pool_s, pres_s, mix_s, sc_s, carry_s, *, n_lat):
    s = pl.program_id(1)
    nt = pl.num_programs(1)
    d_conv = glu_s.shape[2]
    mod = mod_ref[0, 0]
    shift, scale, gate = mod[0:1], mod[1:2], mod[2:3]
    g = g_ref[...]
    xm = jnp.where(s == 0, xc_ref[0], xl_ref[0])
    h_s[0:HALO, :] = _norm_mod(xp_ref[0], g, shift, scale).astype(BF16)
    h_s[HALO:HALO + TS, :] = _norm_mod(xm, g, shift, scale).astype(BF16)
    h_s[HALO + TS:, :] = _norm_mod(xn_ref[0], g, shift, scale).astype(BF16)

    rows = HALO + TS + HALO
    ridx = lax.broadcasted_iota(I32, (rows, 1), 0)
    prev_ok = s >= 2
    next_ok = jnp.logical_and(s >= 1, s <= nt - 2)
    valid = jnp.logical_and(jnp.logical_or(ridx >= HALO, prev_ok),
                            jnp.logical_or(ridx < HALO + TS, next_ok))

    h = h_s[...]
    a = jnp.dot(h, win_ref[:, 0:d_conv], preferred_element_type=F32)
    gt = jnp.dot(h, win_ref[:, d_conv:2 * d_conv], preferred_element_type=F32)
    glu_s[0] = jnp.where(valid, a * jax.nn.sigmoid(gt), 0.0)
    for j in range(1, SUBLANES):
        glu_s[j, 0:rows - SUBLANES, :] = glu_s[0, pl.ds(j, rows - SUBLANES), :]
    pu = jnp.dot(h, win_ref[:, 2 * d_conv:], preferred_element_type=F32)
    pool_s[...] = jnp.where(valid, pu, 0.0)

    seq_pos0 = jnp.where(s == 0, 0, (s - 1) * TS)
    seq_len = jnp.where(s == 0, TS, n_lat)
    half = CONV_WIDTH // 2
    rc_rows = 64
    n_cc = d_conv // LANES
    for rc in range(TS // rc_rows):
        r0 = HALO + rc * rc_rows
        ys = []
        for cc in range(n_cc):
            cs = slice(cc * LANES, (cc + 1) * LANES)
            acc = jnp.zeros((rc_rows, LANES), F32) + dwb_ref[:, cs]
            for k in range(CONV_WIDTH):
                sh = (r0 + k - half) % SUBLANES
                acc = acc + dww_ref[k:k + 1, cs] * glu_s[sh, pl.ds(r0 + k - half - sh, rc_rows), cs]
            ys.append(acc)
        mu = sum(jnp.sum(y, axis=-1, keepdims=True) for y in ys) / d_conv
        var = sum(jnp.sum((y - mu) * (y - mu), axis=-1, keepdims=True) for y in ys) / d_conv
        rs = lax.rsqrt(var + EPS)
        for cc in range(n_cc):
            cs = slice(cc * LANES, (cc + 1) * LANES)
            z = (ys[cc] - mu) * rs * lng_ref[:, cs] + lnb_ref[:, cs]
            mix_s[rc * rc_rows:(rc + 1) * rc_rows, cs] = _silu(z).astype(BF16)
        pos = seq_pos0 + rc * rc_rows + lax.broadcasted_iota(I32, (rc_rows, 1), 0)
        for gi, w in enumerate(POOL_WINDOWS):
            cs = slice(gi * LANES, (gi + 1) * LANES)
            ssum = pool_s[pl.ds(r0 - w // 2, rc_rows), cs]
            for j in range(1 - w // 2, w // 2):
                ssum = ssum + pool_s[pl.ds(r0 + j, rc_rows), cs]
            lo = jnp.clip(pos - w // 2, 0, seq_len)
            hi = jnp.clip(pos + w // 2, 0, seq_len)
            cnt = (hi - lo).astype(F32)
            res = ssum / cnt - pool_s[pl.ds(r0, rc_rows), cs]
            pres_s[rc * rc_rows:(rc + 1) * rc_rows, cs] = res.astype(BF16)

    yp = jnp.dot(pres_s[...], pw_ref[...], preferred_element_type=F32) * ps_ref[...]
    mix_s[:, d_conv:] = yp.astype(BF16)
    y = jnp.dot(mix_s[...], wout_ref[...], preferred_element_type=F32)
    x_new = xm + gate * y
    o_ref[0] = x_new
    _route_tile(x_new, mod, g2_ref, rwt_ref, rb_ref, tri_ref, h2_ref, grp_ref, rank_ref, cnt_ref,
                sc_s, carry_s)


def _even_layer(ctx, x, modtab, norm_g, w_in, dw_w, dw_b, ln_g, ln_b, pool_w, pool_scale, w_out,
                route):
    b, n_lat, d = x.shape
    assert ctx.shape[1] == TS
    s_tot = TS + n_lat
    nt = s_tot // TS
    r_args, r_in_specs, r_out_specs, r_out_shape, r_scratch = _route_plumbing(b, nt, d, *route)
    d_conv = dw_w.shape[1]
    d_pool = pool_scale.shape[0]
    assert d_pool == len(POOL_WINDOWS) * LANES and pool_w.shape[1] == LANES
    pool_bd = jnp.zeros((d_pool, d_pool), F32)
    for gi in range(len(POOL_WINDOWS)):
        pool_bd = pool_bd.at[gi * LANES:(gi + 1) * LANES, gi * LANES:(gi + 1) * LANES].set(pool_w[gi])
    hb = TS // HALO
    n_hblk = n_lat // HALO
    const2 = lambda bi, si: (0, 0)
    return pl.pallas_call(
        functools.partial(_even_kernel, n_lat=n_lat),
        grid=(b, nt),
        in_specs=[
            pl.BlockSpec((1, TS, d), lambda bi, si: (bi, 0, 0)),
            pl.BlockSpec((1, TS, d), lambda bi, si: (bi, jnp.maximum(si - 1, 0), 0)),
            pl.BlockSpec((1, HALO, d), lambda bi, si: (bi, jnp.maximum((si - 1) * hb - 1, 0), 0)),
            pl.BlockSpec((1, HALO, d), lambda bi, si: (bi, jnp.clip(si * hb, 0, n_hblk - 1), 0)),
            pl.BlockSpec((1, 1, N_MOD, d), lambda bi, si: (bi, jnp.minimum(si, 1), 0, 0)),
            pl.BlockSpec((1, d), const2),
            pl.BlockSpec(w_in.shape, const2),
            pl.BlockSpec(dw_w.shape, const2),
            pl.BlockSpec((1, d_conv), const2),
            pl.BlockSpec((1, d_conv), const2),
            pl.BlockSpec((1, d_conv), const2),
            pl.BlockSpec((d_pool, d_pool), const2),
            pl.BlockSpec((1, d_pool), const2),
            pl.BlockSpec(w_out.shape, const2),
        ] + r_in_specs,
        out_specs=[pl.BlockSpec((1, TS, d), lambda bi, si: (bi, si, 0))] + r_out_specs,
        out_shape=[jax.ShapeDtypeStruct((b, s_tot, d), F32)] + r_out_shape,
        scratch_shapes=[
            pltpu.VMEM((TS + 2 * HALO, d), BF16),
            pltpu.VMEM((SUBLANES, TS + 2 * HALO, d_conv), F32),
            pltpu.VMEM((TS + 2 * HALO, d_pool), F32),
            pltpu.VMEM((TS, d_pool), BF16),
            pltpu.VMEM((TS, d_conv + d_pool), BF16),
        ] + r_scratch,
        compiler_params=_cparams(2),
        name="even_mixer",
    )(ctx, x, x, x, modtab, norm_g.reshape(1, d), w_in.astype(BF16), dw_w, dw_b.reshape(1, -1),
      ln_g.reshape(1, -1), ln_b.reshape(1, -1), pool_bd.astype(BF16), pool_scale.reshape(1, -1),
      w_out.astype(BF16), *r_args)


def _route_tile(x, mod, g_ref, rwt_ref, rb_ref, tri_ref, h_ref, grp_ref, rank_ref, cnt_ref, sc_s,
                carry_s):
    first = jnp.logical_and(pl.program_id(0) == 0, pl.program_id(1) == 0)

    @pl.when(first)
    def _():
        carry_s[...] = jnp.zeros_like(carry_s)

    h = _norm_mod(x, g_ref[...], mod[3:4], mod[4:5])
    h_ref[0] = h
    logits = lax.dot_general(rwt_ref[...], h.astype(BF16), (((1,), (1,)), ((), ())),
                             preferred_element_type=F32)
    biased = jax.nn.sigmoid(logits) + rb_ref[...]
    n_half = TS // LANES
    for hh in range(n_half):
        sc_s[hh] = biased[:, hh * LANES:(hh + 1) * LANES]
    a, b, c, d = [
        jnp.concatenate([sc_s[hh, pl.ds(j, N_GROUPS, stride=EXPERTS_PER_GROUP), :]
                         for hh in range(n_half)], axis=-1)
        for j in range(EXPERTS_PER_GROUP)]
    top2 = jnp.maximum(jnp.maximum(jnp.maximum(a + b, a + c), jnp.maximum(a + d, b + c)),
                       jnp.maximum(b + d, c + d))
    gmax = jnp.max(top2, axis=0, keepdims=True)
    gi = lax.broadcasted_iota(I32, top2.shape, 0)
    sel = jnp.min(jnp.where(top2 == gmax, gi, N_GROUPS), axis=0, keepdims=True)
    onehot = gi == sel
    ohf = jnp.where(onehot, 1.0, 0.0)
    prefix = jnp.dot(ohf.astype(BF16), tri_ref[...], preferred_element_type=F32)
    carry = carry_s[:, 0:1]
    rank = jnp.sum(jnp.where(onehot, carry + prefix - 1.0, 0.0), axis=0, keepdims=True)
    carry_s[...] = carry_s[...] + jnp.sum(ohf, axis=1, keepdims=True)
    grp_ref[0] = sel
    rank_ref[0] = rank.astype(I32)
    cnt_ref[...] = carry_s[...]


def _route_plumbing(b, nt, d, norm_g, router_w, router_b):
    const2 = lambda bi, si: (0, 0)
    flat3 = lambda bi, si: (bi * nt + si, 0, 0)
    tri = (jnp.arange(TS)[:, None] <= jnp.arange(TS)[None, :]).astype(BF16)
    args = [norm_g.reshape(1, d), router_w.T.astype(BF16), router_b.reshape(N_EXPERTS, 1), tri]
    in_specs = [pl.BlockSpec((1, d), const2), pl.BlockSpec((N_EXPERTS, d), const2),
                pl.BlockSpec((N_EXPERTS, 1), const2), pl.BlockSpec((TS, TS), const2)]
    out_specs = [pl.BlockSpec((1, TS, d), lambda bi, si: (bi, si, 0)),
                 pl.BlockSpec((1, 1, TS), flat3), pl.BlockSpec((1, 1, TS), flat3),
                 pl.BlockSpec((N_GROUPS, LANES), const2)]
    out_shape = [jax.ShapeDtypeStruct((b, nt * TS, d), F32),
                 jax.ShapeDtypeStruct((b * nt, 1, TS), I32),
                 jax.ShapeDtypeStruct((b * nt, 1, TS), I32),
                 jax.ShapeDtypeStruct((N_GROUPS, LANES), F32)]
    scratch = [pltpu.VMEM((TS // LANES, N_EXPERTS, LANES), F32), pltpu.VMEM((N_GROUPS, LANES), F32)]
    return args, in_specs, out_specs, out_shape, scratch


def _dispatch_kernel(zrow_ref, pos_ref, src_ref, dst_ref, buf, ld_sem, st_sem, z_sem):
    i = pl.program_id(0)
    n = pl.num_programs(0)
    slot = lax.rem(i, DISPATCH_BUFS)
    nxt = lax.rem(i + 1, DISPATCH_BUFS)
    tile_groups = TS // SUBLANES

    def load(tile, which):
        return pltpu.make_async_copy(src_ref.at[pl.ds(tile * tile_groups, tile_groups)],
                                     buf.at[which], ld_sem.at[which])

    def wait_scatter(which):
        pltpu.make_async_copy(buf.at[which], dst_ref.at[pl.ds(0, tile_groups)],
                              st_sem.at[which]).wait()

    @pl.when(i == 0)
    def _():
        load(i, slot).start()
        zsrc = buf.at[DISPATCH_BUFS - 1]
        zsrc[...] = jnp.zeros(zsrc.shape, F32)
        for z in range(zrow_ref.shape[0]):
            @pl.when(zrow_ref[z] >= 0)
            def _():
                pltpu.make_async_copy(zsrc, dst_ref.at[pl.ds(zrow_ref[z], tile_groups)],
                                      z_sem).start()
        for z in range(zrow_ref.shape[0]):
            @pl.when(zrow_ref[z] >= 0)
            def _():
                pltpu.make_async_copy(zsrc, dst_ref.at[pl.ds(0, tile_groups)], z_sem).wait()

    load(i, slot).wait()

    @pl.when(i + 1 >= DISPATCH_BUFS)
    def _():
        wait_scatter(nxt)

    @pl.when(i + 1 < n)
    def _():
        load(i + 1, nxt).start()

    def issue(r8, carry):
        for j in range(SUBLANES):
            p = pos_ref[0, 0, r8 * SUBLANES + j]
            dst_row = dst_ref.at[lax.shift_right_logical(p, 3), pl.ds(p & (SUBLANES - 1), 1)]
            pltpu.make_async_copy(buf.at[slot, r8, pl.ds(j, 1)], dst_row, st_sem.at[slot]).start()
        return carry

    lax.fori_loop(0, tile_groups, issue, 0)

    @pl.when(i == n - 1)
    def _():
        for back in range(DISPATCH_BUFS - 1):
            @pl.when(i - back >= 0)
            def _():
                wait_scatter(lax.rem(i - back + DISPATCH_BUFS, DISPATCH_BUFS))


def _moe_dispatch(h_flat, pos, zero_rows, n_rows):
    t, d = h_flat.shape
    n_tiles = t // TS
    assert TS == TMG and n_tiles >= DISPATCH_BUFS
    zero_groups = jnp.where(zero_rows >= 0, zero_rows // SUBLANES, -1).astype(I32)
    return pl.pallas_call(
        _dispatch_kernel,
        grid=(n_tiles,),
        in_specs=[
            pl.BlockSpec(memory_space=pltpu.SMEM),
            pl.BlockSpec((1, 1, TS), lambda i: (i, 0, 0), memory_space=pltpu.SMEM),
            pl.BlockSpec(memory_space=pl.ANY),
        ],
        out_specs=pl.BlockSpec(memory_space=pl.ANY),
        out_shape=jax.ShapeDtypeStruct((n_rows // SUBLANES, SUBLANES, d), F32),
        scratch_shapes=[pltpu.VMEM((DISPATCH_BUFS, TS // SUBLANES, SUBLANES, d), F32),
                        pltpu.SemaphoreType.DMA((DISPATCH_BUFS,)),
                        pltpu.SemaphoreType.DMA((DISPATCH_BUFS,)),
                        pltpu.SemaphoreType.DMA(())],
        compiler_params=_cparams(1),
        name="moe_dispatch",
    )(zero_groups, pos, h_flat.reshape(t // SUBLANES, SUBLANES, d))


def _ffn_kernel(tg_ref, tv_ref, xs_ref, rw_ref, rb_ref, wg32_ref, wu32_ref, wd32_ref, o_ref,
                wg_ref, wu_ref, wd_ref):
    i = pl.program_id(0)

    @pl.when(jnp.logical_or(i == 0, tg_ref[i] != tg_ref[jnp.maximum(i - 1, 0)]))
    def _():
        wg_ref[...] = wg32_ref[...].astype(BF16)
        wu_ref[...] = wu32_ref[...].astype(BF16)
        wd_ref[...] = wd32_ref[...].astype(BF16)

    @pl.when(tv_ref[i] == 0)
    def _():
        o_ref[...] = jnp.zeros_like(o_ref)

    @pl.when(tv_ref[i] != 0)
    def _():
        xb = xs_ref[...].astype(BF16)
        logits = jnp.dot(xb, rw_ref[0], preferred_element_type=F32)
        sc = jax.nn.sigmoid(logits)
        bs = sc + rb_ref[0]
        s_col = [sc[:, j:j + 1] for j in range(EXPERTS_PER_GROUP)]
        b_col = [bs[:, j:j + 1] for j in range(EXPERTS_PER_GROUP)]
        sel = []
        for j in range(EXPERTS_PER_GROUP):
            beaten = jnp.zeros_like(b_col[j])
            for k in range(EXPERTS_PER_GROUP):
                if k == j:
                    continue
                wins = (b_col[k] >= b_col[j]) if k < j else (b_col[k] > b_col[j])
                beaten = beaten + jnp.where(wins, 1.0, 0.0)
            sel.append(beaten < 2.0)
        den = sum(jnp.where(sel[j], s_col[j], 0.0) for j in range(EXPERTS_PER_GROUP))
        hes = []
        for j in range(EXPERTS_PER_GROUP):
            cj = jnp.where(sel[j], s_col[j] / den, 0.0)
            gj = jnp.dot(xb, wg_ref[0, j], preferred_element_type=F32)
            uj = jnp.dot(xb, wu_ref[0, j], preferred_element_type=F32)
            hes.append((_silu(gj) * uj * cj).astype(BF16))
        he = jnp.concatenate(hes, axis=-1)
        d_e = wd_ref.shape[2]
        wd = wd_ref[0].reshape(EXPERTS_PER_GROUP * d_e, wd_ref.shape[3])
        o_ref[...] = jnp.dot(he, wd, preferred_element_type=F32)


def _moe_ffn(xs, tile_group, tile_valid, rw_g, rb_g, w_gate, w_up, w_down, layer):
    n_rows, d = xs.shape
    d_e = w_gate.shape[-1]
    n_tiles = n_rows // TMG
    epg = EXPERTS_PER_GROUP
    grid_spec = pltpu.PrefetchScalarGridSpec(
        num_scalar_prefetch=2,
        grid=(n_tiles,),
        in_specs=[
            pl.BlockSpec((TMG, d), lambda i, tg, tv: (i, 0)),
            pl.BlockSpec((1, d, LANES), lambda i, tg, tv: (tg[i], 0, 0)),
            pl.BlockSpec((1, 1, LANES), lambda i, tg, tv: (tg[i], 0, 0)),
            pl.BlockSpec((1, epg, d, d_e), lambda i, tg, tv: (layer, tg[i], 0, 0)),
            pl.BlockSpec((1, epg, d, d_e), lambda i, tg, tv: (layer, tg[i], 0, 0)),
            pl.BlockSpec((1, epg, d_e, d), lambda i, tg, tv: (layer, tg[i], 0, 0)),
        ],
        out_specs=pl.BlockSpec((TMG, d), lambda i, tg, tv: (i, 0)),
        scratch_shapes=[pltpu.VMEM((1, epg, d, d_e), BF16), pltpu.VMEM((1, epg, d, d_e), BF16),
                        pltpu.VMEM((1, epg, d_e, d), BF16)],
    )
    return pl.pallas_call(
        _ffn_kernel,
        grid_spec=grid_spec,
        out_shape=jax.ShapeDtypeStruct((n_rows, d), F32),
        compiler_params=_cparams(1),
        name="moe_ffn",
    )(tile_group, tile_valid, xs, rw_g, rb_g, w_gate, w_up, w_down)


def _combine_kernel(pos_ref, posn_ref, x_ref, mod_ref, ys_ref, o_ref, buf, sem):
    i = pl.program_id(0) * pl.num_programs(1) + pl.program_id(1)
    n = pl.num_programs(0) * pl.num_programs(1)
    slot = lax.rem(i, 2)

    tile_groups = TS // SUBLANES

    def gather(p_ref, which):
        def issue(r8, carry):
            for j in range(SUBLANES):
                p = p_ref[0, 0, r8 * SUBLANES + j]
                src_row = ys_ref.at[lax.shift_right_logical(p, 3), pl.ds(p & (SUBLANES - 1), 1)]
                pltpu.make_async_copy(src_row, buf.at[which, r8, pl.ds(j, 1)], sem.at[which]).start()
            return carry
        lax.fori_loop(0, tile_groups, issue, 0)

    @pl.when(i == 0)
    def _():
        gather(pos_ref, slot)

    @pl.when(i + 1 < n)
    def _():
        gather(posn_ref, 1 - slot)

    pltpu.make_async_copy(ys_ref.at[pl.ds(0, tile_groups)], buf.at[slot], sem.at[slot]).wait()
    gate = mod_ref[0, 0][5:6]
    o_ref[0] = x_ref[0] + gate * buf[slot].reshape(TS, buf.shape[-1])


def _moe_combine(x, modtab, pos, ys, kind_of_tile):
    b, s_tot, d = x.shape
    nt = s_tot // TS
    n_tiles = b * nt
    cur = lambda bi, si: (bi * nt + si, 0, 0)
    nxt = lambda bi, si: (jnp.minimum(bi * nt + si + 1, n_tiles - 1), 0, 0)
    return pl.pallas_call(
        _combine_kernel,
        grid=(b, nt),
        in_specs=[
            pl.BlockSpec((1, 1, TS), cur, memory_space=pltpu.SMEM),
            pl.BlockSpec((1, 1, TS), nxt, memory_space=pltpu.SMEM),
            pl.BlockSpec((1, TS, d), lambda bi, si: (bi, si, 0)),
            pl.BlockSpec((1, 1, N_MOD, d), lambda bi, si: (bi, kind_of_tile(si), 0, 0)),
            pl.BlockSpec(memory_space=pl.ANY),
        ],
        out_specs=pl.BlockSpec((1, TS, d), lambda bi, si: (bi, si, 0)),
        out_shape=jax.ShapeDtypeStruct((b, s_tot, d), F32),
        scratch_shapes=[pltpu.VMEM((2, TS // SUBLANES, SUBLANES, d), F32),
                        pltpu.SemaphoreType.DMA((2,))],
        compiler_params=_cparams(2),
        name="moe_combine",
    )(pos, pos, x, modtab, ys.reshape(ys.shape[0] // SUBLANES, SUBLANES, d))


def _moe_layer(x, routed, modtab, router_w, router_b, w_gate, w_up, w_down, layer, kind_of_tile):
    b, s_tot, d = x.shape
    t = b * s_tot
    h, grp, rank, cnt = routed
    counts = cnt[:, 0].astype(I32)
    padded = ((counts + TMG - 1) // TMG) * TMG
    ends = jnp.cumsum(padded)
    starts = ends - padded
    pos = starts[grp] + rank
    n_rows = t + N_GROUPS * TMG
    tile_row0 = jnp.arange(n_rows // TMG, dtype=I32) * TMG
    tile_group = jnp.minimum(jnp.sum(tile_row0[:, None] >= ends[None, :], axis=1),
                             N_GROUPS - 1).astype(I32)
    tile_valid = (tile_row0 < ends[-1]).astype(I32)
    tail_rows = jnp.minimum(ends[-1] + jnp.arange(N_GROUPS, dtype=I32) * TMG, n_rows - TMG)
    zero_rows = jnp.concatenate([jnp.where(padded > 0, ends - TMG, tail_rows[-1]), tail_rows])
    n_z = zero_rows.shape[0]
    repeat = jnp.any((zero_rows[:, None] == zero_rows[None, :])
                     & (jnp.arange(n_z)[None, :] < jnp.arange(n_z)[:, None]), axis=1)
    zero_rows = jnp.where(repeat, -1, zero_rows)
    xs = _moe_dispatch(h.reshape(t, d), pos, zero_rows.astype(I32), n_rows)
    epg = EXPERTS_PER_GROUP
    rw_g = jnp.pad(router_w.reshape(d, N_GROUPS, epg).transpose(1, 0, 2),
                   ((0, 0), (0, 0), (0, LANES - epg))).astype(BF16)
    rb_g = jnp.pad(router_b.reshape(N_GROUPS, 1, epg), ((0, 0), (0, 0), (0, LANES - epg)))
    ys = _moe_ffn(xs.reshape(n_rows, d), tile_group, tile_valid, rw_g, rb_g, w_gate, w_up, w_down,
                  layer)
    return _moe_combine(x, modtab, pos, ys, kind_of_tile)


def _rope_pad(w, axis):
    x1, x2 = jnp.split(w, 2, axis=axis)
    z = jnp.zeros_like(x1)
    return jnp.concatenate([x1, z, x2, z], axis=axis)


def _oddproj_kernel(x_ref, mod_ref, g_ref, win_ref, gcq_ref, wuq_ref, gckv_ref, wukv_ref, gqm_ref,
                    gkm_ref, gqd_ref, gkd_ref, cs_ref, sna_ref, snb_ref, csm_ref, snm_ref,
                    qm_ref, km_ref, vm_ref, qd_ref, kd_ref, vd_ref, p_s, qm_s, kv_s):
    mod = mod_ref[0, 0]
    h = _norm_mod(x_ref[0], g_ref[...], mod[0:1], mod[1:2]).astype(BF16)
    q_lora = gcq_ref.shape[1]
    kv_lora = gckv_ref.shape[1]
    n_qd = DIFF_HEADS * 2 * DIFF_QK
    o_qd = q_lora
    o_ckv = o_qd + n_qd
    o_kr = o_ckv + kv_lora
    o_kd = o_kr + LANES
    o_vd = o_kd + n_qd
    lane = lax.broadcasted_iota(I32, (1, LANES), 1)
    low = lane < DIFF_QK
    d_mla = QK_NOPE + QK_ROPE

    def rms(x, g, n):
        return x * lax.rsqrt(jnp.sum(x * x, axis=-1, keepdims=True) / n + EPS) * g

    def rope_mla(y):
        return y * csm_ref[...] + pltpu.roll(y, LANES // 2, 1) * snm_ref[...]

    def diff_cols(p, g):
        sq = p * p
        s_all = jnp.sum(sq, axis=-1, keepdims=True)
        s_lo = jnp.sum(jnp.where(low, sq, 0.0), axis=-1, keepdims=True)
        r = jnp.where(low, lax.rsqrt(s_lo / DIFF_QK + EPS),
                      lax.rsqrt((s_all - s_lo) / DIFF_QK + EPS))
        y = p * r * g
        return (y * cs_ref[...] + pltpu.roll(y, LANES - DIFF_QK // 2, 1) * sna_ref[...]
                + pltpu.roll(y, DIFF_QK // 2, 1) * snb_ref[...])

    p_s[...] = jnp.dot(h, win_ref[...], preferred_element_type=F32)
    cq = rms(p_s[:, 0:q_lora], gcq_ref[...], q_lora)
    qm_s[...] = jnp.dot(cq.astype(BF16), wuq_ref[...], preferred_element_type=F32)
    ckv = rms(p_s[:, o_ckv:o_ckv + kv_lora], gckv_ref[...], kv_lora)
    kv_s[...] = jnp.dot(ckv.astype(BF16), wukv_ref[...], preferred_element_type=F32)

    for hd in range(DIFF_HEADS):
        y = diff_cols(p_s[:, o_qd + LANES * hd:o_qd + LANES * (hd + 1)], gqd_ref[...])
        y = y * (DIFF_SCALE * LOG2E)
        qd_ref[0, 2 * hd] = jnp.where(low, y, 0.0).astype(BF16)
        qd_ref[0, 2 * hd + 1] = jnp.where(low, 0.0, y).astype(BF16)
        kd_ref[0, hd] = diff_cols(p_s[:, o_kd + LANES * hd:o_kd + LANES * (hd + 1)],
                                  gkd_ref[...]).astype(BF16)
        vd_ref[0, hd] = p_s[:, o_vd + DIFF_V * hd:o_vd + DIFF_V * (hd + 1)].T.astype(BF16)

    kr = p_s[:, o_kr:o_kr + LANES]
    for hd in range(MLA_HEADS):
        qh = rms(qm_s[:, 2 * LANES * hd:2 * LANES * (hd + 1)], gqm_ref[...], d_mla)
        qh = jnp.concatenate([qh[:, :LANES], rope_mla(qh[:, LANES:])], axis=-1)
        qm_ref[0, hd] = (qh * (MLA_SCALE * LOG2E)).astype(BF16)
        kcat = jnp.concatenate([kv_s[:, 2 * LANES * hd:2 * LANES * hd + LANES], kr], axis=-1)
        kh = rms(kcat, gkm_ref[...], d_mla)
        km_ref[0, hd] = jnp.concatenate([kh[:, :LANES], rope_mla(kh[:, LANES:])], axis=-1).astype(BF16)
        vm_ref[0, hd] = kv_s[:, 2 * LANES * hd + LANES:2 * LANES * (hd + 1)].T.astype(BF16)


def _odd_project(xa, modtab, norm_g, w_in, g_cq, w_uq, g_ckv, w_ukv, g_mla, g_diff, n_ctx):
    b, s_tot, d = xa.shape
    nt = s_tot // TS
    n_lat = s_tot - n_ctx
    q_lora = g_cq.shape[0]
    kv_lora = g_ckv.shape[0]
    n_qd = DIFF_HEADS * 2 * DIFF_QK
    d_mla = QK_NOPE + QK_ROPE
    o = 0
    w_cq = w_in[:, o:o + q_lora]; o += q_lora
    w_qd = w_in[:, o:o + n_qd]; o += n_qd
    w_ckv = w_in[:, o:o + kv_lora]; o += kv_lora
    w_kr = w_in[:, o:o + QK_ROPE]; o += QK_ROPE
    w_kd = w_in[:, o:o + n_qd]; o += n_qd
    w_vd = w_in[:, o:]
    w_in_p = jnp.concatenate([w_cq, w_qd, w_ckv, _rope_pad(w_kr, 1), w_kd, w_vd], axis=1).astype(BF16)
    wuq = w_uq.reshape(q_lora, MLA_HEADS, d_mla)
    wuq_p = jnp.concatenate([wuq[..., :QK_NOPE], _rope_pad(wuq[..., QK_NOPE:], 2)],
                            axis=-1).reshape(q_lora, MLA_HEADS * 2 * LANES).astype(BF16)

    def pad_gain(g):
        return jnp.concatenate([g[:QK_NOPE], _rope_pad(g[QK_NOPE:], 0)]).reshape(1, 2 * LANES)

    rows = n_lat // GRID_W
    row = jnp.repeat(jnp.arange(rows, dtype=F32), GRID_W)
    col = jnp.tile(jnp.arange(GRID_W, dtype=F32), rows)
    axis_dim = QK_ROPE // 2
    inv_freq = ROPE_BASE ** (-jnp.arange(0, axis_dim, 2, dtype=F32) / axis_dim)
    ang = jnp.concatenate([row[:, None] * inv_freq, col[:, None] * inv_freq], axis=-1)
    half = QK_ROPE // 2
    cos = jnp.concatenate([jnp.ones((n_ctx, half), F32), jnp.cos(ang)], axis=0)
    sin = jnp.concatenate([jnp.zeros((n_ctx, half), F32), jnp.sin(ang)], axis=0)
    z = jnp.zeros_like(sin)
    cs_d = jnp.tile(cos, (1, LANES // half))
    sna_d = jnp.tile(jnp.concatenate([-sin, z], axis=1), (1, LANES // QK_ROPE))
    snb_d = jnp.tile(jnp.concatenate([z, sin], axis=1), (1, LANES // QK_ROPE))
    cs_m = jnp.concatenate([cos, z, cos, z], axis=1)
    sn_m = jnp.concatenate([-sin, z, sin, z], axis=1)

    const2 = lambda bi, si: (0, 0)
    tok = lambda bi, si: (si, 0)
    kv4 = lambda bi, si: (bi, 0, si, 0)
    vt4 = lambda bi, si: (bi, 0, 0, si)
    q4 = lambda bi, si: (bi, 0, jnp.maximum(si - n_ctx // TS, 0), 0)
    full = lambda a: pl.BlockSpec(a.shape, const2)
    gq_d = jnp.tile(g_diff[0], LANES // DIFF_QK).reshape(1, LANES)
    gk_d = jnp.tile(g_diff[1], LANES // DIFF_QK).reshape(1, LANES)
    weights = [norm_g.reshape(1, d), w_in_p, g_cq.reshape(1, -1), wuq_p, g_ckv.reshape(1, -1),
               w_ukv.astype(BF16), pad_gain(g_mla[0]), pad_gain(g_mla[1]), gq_d, gk_d]
    args = [xa, modtab, *weights, cs_d, sna_d, snb_d, cs_m, sn_m]
    in_specs = [
        pl.BlockSpec((1, TS, d), lambda bi, si: (bi, si, 0)),
        pl.BlockSpec((1, 1, N_MOD, d), lambda bi, si: (bi, jnp.minimum(si, 1), 0, 0)),
    ] + [full(a) for a in weights] + [pl.BlockSpec((TS, LANES), tok)] * 5
    hm, hd = MLA_HEADS, DIFF_HEADS
    return pl.pallas_call(
        _oddproj_kernel,
        grid=(b, nt),
        in_specs=in_specs,
        out_specs=[
            pl.BlockSpec((1, hm, TS, 2 * LANES), q4),
            pl.BlockSpec((1, hm, TS, 2 * LANES), kv4),
            pl.BlockSpec((1, hm, V_HEAD, TS), vt4),
            pl.BlockSpec((1, 2 * hd, TS, LANES), q4),
            pl.BlockSpec((1, hd, TS, LANES), kv4),
            pl.BlockSpec((1, hd, DIFF_V, TS), vt4),
        ],
        out_shape=[
            jax.ShapeDtypeStruct((b, hm, n_lat, 2 * LANES), BF16),
            jax.ShapeDtypeStruct((b, hm, s_tot, 2 * LANES), BF16),
            jax.ShapeDtypeStruct((b, hm, V_HEAD, s_tot), BF16),
            jax.ShapeDtypeStruct((b, 2 * hd, n_lat, LANES), BF16),
            jax.ShapeDtypeStruct((b, hd, s_tot, LANES), BF16),
            jax.ShapeDtypeStruct((b, hd, DIFF_V, s_tot), BF16),
        ],
        scratch_shapes=[pltpu.VMEM((TS, w_in_p.shape[1]), F32),
                        pltpu.VMEM((TS, wuq_p.shape[1]), F32),
                        pltpu.VMEM((TS, w_ukv.shape[1]), F32)],
        compiler_params=_cparams(2),
        name="odd_project",
    )(*args)


def _attn_kernel(q_ref, k_ref, vt_ref, o_ref, s_s, p_s, *, rep):
    hq = q_ref.shape[1]
    n_items = hq * (q_ref.shape[2] // TQ)
    n_chunks = k_ref.shape[2] // TK
    groups = TK // SUBLANES

    def q_rows(i):
        return pl.ds((i // hq) * TQ, TQ)

    def scores(i):
        u, h = i % 2, i % hq
        s_s[u] = lax.dot_general(k_ref[0, h // rep], q_ref[0, h, q_rows(i), :],
                                 (((1,), (1,)), ((), ())), preferred_element_type=F32)
        m8 = jnp.full((SUBLANES, TQ), -jnp.inf, F32)
        for c in range(n_chunks):
            st = s_s[u, pl.ds(c * TK, TK), :]
            m8 = jnp.maximum(m8, jnp.max(st.reshape(groups, SUBLANES, TQ), axis=0))
        return jnp.max(m8, axis=0, keepdims=True)

    def exponentials(i, row_max):
        u = i % 2
        l8 = jnp.zeros((SUBLANES, TQ), F32)
        for c in range(n_chunks):
            p = jnp.exp2(s_s[u, pl.ds(c * TK, TK), :] - row_max)
            l8 = l8 + jnp.sum(p.reshape(groups, SUBLANES, TQ), axis=0)
            p_s[u, pl.ds(c * TK, TK), :] = p.astype(BF16)
        return jnp.sum(l8, axis=0, keepdims=True)

    def values(i, l):
        h = i % hq
        out_t = jnp.dot(vt_ref[0, h // rep], p_s[i % 2], preferred_element_type=F32)
        o_ref[0, h, q_rows(i), :] = (out_t / l).T.astype(o_ref.dtype)

    row_max, row_sum = {}, {}
    for t in range(n_items + 2):
        if t < n_items:
            row_max[t] = scores(t)
        if 0 <= t - 1 < n_items:
            row_sum[t - 1] = exponentials(t - 1, row_max.pop(t - 1))
        if 0 <= t - 2 < n_items:
            values(t - 2, row_sum.pop(t - 2))


def _diff_attn_kernel(q_ref, k_ref, vt_ref, dl_ref, o_ref, s_s, p_s, a1_s, *, lam_init):
    hq = q_ref.shape[1]
    n_items = hq * (q_ref.shape[2] // TQ)
    n_chunks = k_ref.shape[2] // TK
    groups = TK // SUBLANES
    dl = dl_ref[...]
    lam = (jnp.exp(jnp.sum(dl[0:1] * dl[1:2], axis=-1, keepdims=True))
           - jnp.exp(jnp.sum(dl[2:3] * dl[3:4], axis=-1, keepdims=True)) + lam_init)

    def q_rows(i):
        return pl.ds((i // hq) * TQ, TQ)

    def scores(i):
        u, slot = i % 2, i % hq
        s_s[u] = lax.dot_general(k_ref[0, slot // 2], q_ref[0, slot, q_rows(i), :],
                                 (((1,), (1,)), ((), ())), preferred_element_type=F32)
        m8 = jnp.full((SUBLANES, TQ), -jnp.inf, F32)
        for c in range(n_chunks):
            st = s_s[u, pl.ds(c * TK, TK), :]
            m8 = jnp.maximum(m8, jnp.max(st.reshape(groups, SUBLANES, TQ), axis=0))
        return jnp.max(m8, axis=0, keepdims=True)

    def exponentials(i, row_max):
        u = i % 2
        l8 = jnp.zeros((SUBLANES, TQ), F32)
        for c in range(n_chunks):
            p = jnp.exp2(s_s[u, pl.ds(c * TK, TK), :] - row_max)
            l8 = l8 + jnp.sum(p.reshape(groups, SUBLANES, TQ), axis=0)
            p_s[u, pl.ds(c * TK, TK), :] = p.astype(BF16)
        return jnp.sum(l8, axis=0, keepdims=True)

    def values(i, l):
        slot = i % hq
        a = jnp.dot(vt_ref[0, slot // 2], p_s[i % 2], preferred_element_type=F32) / l
        if slot % 2 == 0:
            a1_s[...] = a
        else:
            o_ref[0, slot // 2, q_rows(i), :] = (a1_s[...] - lam * a).T

    row_max, row_sum = {}, {}
    for t in range(n_items + 2):
        if t < n_items:
            row_max[t] = scores(t)
        if 0 <= t - 1 < n_items:
            row_sum[t - 1] = exponentials(t - 1, row_max.pop(t - 1))
        if 0 <= t - 2 < n_items:
            values(t - 2, row_sum.pop(t - 2))


def _diff_attention(q, k, vt, diff_lambda, lam_init, q_tiles):
    b, hq, lq, dk = q.shape
    hk, lk = k.shape[1], k.shape[2]
    dv = vt.shape[2]
    tq_step = q_tiles * TQ
    assert lk % TK == 0 and lq % tq_step == 0 and hq == 2 * hk
    once = pl.Buffered(1)
    return pl.pallas_call(
        functools.partial(_diff_attn_kernel, lam_init=lam_init),
        grid=(b, lq // tq_step),
        in_specs=[
            pl.BlockSpec((1, hq, tq_step, dk), lambda bi, qi: (bi, 0, qi, 0)),
            pl.BlockSpec((1, hk, lk, dk), lambda bi, qi: (bi, 0, 0, 0), pipeline_mode=once),
            pl.BlockSpec((1, hk, dv, lk), lambda bi, qi: (bi, 0, 0, 0), pipeline_mode=once),
            pl.BlockSpec(diff_lambda.shape, lambda bi, qi: (0, 0)),
        ],
        out_specs=pl.BlockSpec((1, hk, tq_step, dv), lambda bi, qi: (bi, 0, qi, 0)),
        out_shape=jax.ShapeDtypeStruct((b, hk, lq, dv), F32),
        scratch_shapes=[pltpu.VMEM((2, lk, TQ), F32), pltpu.VMEM((2, lk, TQ), BF16),
                        pltpu.VMEM((dv, TQ), F32)],
        compiler_params=_cparams(2),
        name="attn_diff",
    )(q, k, vt, diff_lambda)


def _attention(q, k, vt, out_dtype, name, q_tiles):
    b, hq, lq, dk = q.shape
    hk, lk = k.shape[1], k.shape[2]
    dv = vt.shape[2]
    tq_step = q_tiles * TQ
    assert lk % TK == 0 and lq % tq_step == 0
    once = pl.Buffered(1)
    return pl.pallas_call(
        functools.partial(_attn_kernel, rep=hq // hk),
        grid=(b, lq // tq_step),
        in_specs=[
            pl.BlockSpec((1, hq, tq_step, dk), lambda bi, qi: (bi, 0, qi, 0)),
            pl.BlockSpec((1, hk, lk, dk), lambda bi, qi: (bi, 0, 0, 0), pipeline_mode=once),
            pl.BlockSpec((1, hk, dv, lk), lambda bi, qi: (bi, 0, 0, 0), pipeline_mode=once),
        ],
        out_specs=pl.BlockSpec((1, hq, tq_step, dv), lambda bi, qi: (bi, 0, qi, 0)),
        out_shape=jax.ShapeDtypeStruct((b, hq, lq, dv), out_dtype),
        scratch_shapes=[pltpu.VMEM((2, lk, TQ), F32), pltpu.VMEM((2, lk, TQ), BF16)],
        compiler_params=_cparams(2),
        name=name,
    )(q, k, vt)


def _oddout_kernel(x_ref, mod_ref, om_ref, ad_ref, gs_ref, wout_ref, g2_ref, rwt_ref,
                   rb_ref, tri_ref, o_ref, h2_ref, grp_ref, rank_ref, cnt_ref, sc_s, carry_s, *,
                   lam_init):
    parts = [om_ref[0, hd] for hd in range(MLA_HEADS)]
    for hd in range(DIFF_HEADS):
        df = ad_ref[0, hd]
        od = df * lax.rsqrt(jnp.mean(df * df, axis=-1, keepdims=True) + EPS) * gs_ref[...]
        parts.append((od * (1.0 - lam_init)).astype(BF16))
    y = jnp.dot(jnp.concatenate(parts, axis=-1), wout_ref[...], preferred_element_type=F32)
    mod = mod_ref[0, 0]
    x_new = x_ref[0] + mod[2:3] * y
    o_ref[0] = x_new
    _route_tile(x_new, mod, g2_ref, rwt_ref, rb_ref, tri_ref, h2_ref, grp_ref, rank_ref, cnt_ref,
                sc_s, carry_s)


def _odd_output(xa, modtab, o_m, a_d, g_sub, w_out, lam_init, n_ctx, route):
    b, s_tot, d = xa.shape
    n_lat = s_tot - n_ctx
    off = n_ctx // TS
    const2 = lambda bi, si: (0, 0)
    r_args, r_in_specs, r_out_specs, r_out_shape, r_scratch = _route_plumbing(
        b, n_lat // TS, d, *route)
    return pl.pallas_call(
        functools.partial(_oddout_kernel, lam_init=lam_init),
        grid=(b, n_lat // TS),
        in_specs=[
            pl.BlockSpec((1, TS, d), lambda bi, si: (bi, si + off, 0)),
            pl.BlockSpec((1, 1, N_MOD, d), lambda bi, si: (bi, 1, 0, 0)),
            pl.BlockSpec((1, o_m.shape[1], TS, V_HEAD), lambda bi, si: (bi, 0, si, 0)),
            pl.BlockSpec((1, a_d.shape[1], TS, DIFF_V), lambda bi, si: (bi, 0, si, 0)),
            pl.BlockSpec((1, DIFF_V), const2),
            pl.BlockSpec(w_out.shape, const2),
        ] + r_in_specs,
        out_specs=[pl.BlockSpec((1, TS, d), lambda bi, si: (bi, si, 0))] + r_out_specs,
        out_shape=[jax.ShapeDtypeStruct((b, n_lat, d), F32)] + r_out_shape,
        scratch_shapes=r_scratch,
        compiler_params=_cparams(2),
        name="odd_output",
    )(xa, modtab, o_m, a_d, g_sub.reshape(1, -1), w_out.astype(BF16), *r_args)


def kernel(x, c, ctx, c_ctx, mod_w, mod_b, norm_g, even_w_in, conv_dw_w, conv_dw_b, conv_ln_g,
           conv_ln_b, pool_w, pool_scale, even_w_out, odd_w_in, mla_g_cq, mla_w_uq, mla_g_ckv,
           mla_w_ukv, qk_g_mla, qk_g_diff, diff_lambda, diff_sub_g, odd_w_out, router_w, router_b,
           moe_w_gate, moe_w_up, moe_w_down):
    b, n_lat, d = x.shape
    n_ctx = ctx.shape[1]
    assert n_ctx == TS and n_lat % TS == 0 and mod_w.shape[0] == 2

    c_rows = jnp.concatenate([c, c_ctx[None], jnp.zeros((16 - b - 1, d), F32)], axis=0)
    mods = _modulation(c_rows, mod_w, mod_b)
    modtabs = []
    for i in range(2):
        mod_l = mods[i, :b].reshape(b, N_MOD, d)
        mod_c = jnp.broadcast_to(mods[i, b].reshape(1, N_MOD, d), (b, N_MOD, d))
        modtabs.append(jnp.stack([mod_c, mod_l], axis=1))

    ctx_or_lat = lambda si: jnp.minimum(si, 1)
    lat_only = lambda si: 1

    xa, *routed = _even_layer(ctx, x, modtabs[0], norm_g[0, 0], even_w_in[0], conv_dw_w[0],
                              conv_dw_b[0], conv_ln_g[0], conv_ln_b[0], pool_w[0], pool_scale[0],
                              even_w_out[0], (norm_g[0, 1], router_w, router_b))
    xa = _moe_layer(xa, routed, modtabs[0], router_w, router_b, moe_w_gate, moe_w_up, moe_w_down,
                    0, ctx_or_lat)

    qm, km, vm, qd, kd, vd = _odd_project(xa, modtabs[1], norm_g[1, 0], odd_w_in[0], mla_g_cq[0],
                                          mla_w_uq[0], mla_g_ckv[0], mla_w_ukv[0], qk_g_mla[0],
                                          qk_g_diff[0], n_ctx)
    lam_init = 0.8 - 0.6 * math.exp(-0.3 * 1)
    q_tiles = min(ATTN_Q_TILES, n_lat // TQ)
    o_m = _attention(qm, km, vm, BF16, "attn_mla", q_tiles)
    a_d = _diff_attention(qd, kd, vd, diff_lambda[0], lam_init, max(q_tiles // 2, 1))
    xl, *routed = _odd_output(xa, modtabs[1], o_m, a_d, diff_sub_g[0], odd_w_out[0], lam_init,
                              n_ctx, (norm_g[1, 1], router_w, router_b))
    return _moe_layer(xl, routed, modtabs[1], router_w, router_b, moe_w_gate, moe_w_up,
                      moe_w_down, 1, lat_only)
```

```python
import functools
import math

import jax
import jax.numpy as jnp
from jax import lax
from jax.experimental import pallas as pl
from jax.experimental.pallas import tpu as pltpu

F32 = jnp.float32
BF16 = jnp.bfloat16
I32 = jnp.int32

GRID_W = 64
N_MOD = 6
EPS = 1e-6
CONV_WIDTH = 31
POOL_WINDOWS = (2, 4, 8, 16)
MLA_HEADS = 4
QK_NOPE = 128
QK_ROPE = 64
V_HEAD = 128
DIFF_HEADS = 4
DIFF_QK = 64
DIFF_V = 128
MLA_SCALE = (QK_NOPE + QK_ROPE) ** -0.5
DIFF_SCALE = DIFF_QK ** -0.5
ROPE_BASE = 10000.0
LOG2E = math.log2(math.e)
N_EXPERTS = 32
N_GROUPS = 8
EXPERTS_PER_GROUP = N_EXPERTS // N_GROUPS

LANES = 128
SUBLANES = 8
TS = 256
HALO = 16
TMG = 256
TQ = 256
TK = 256
ATTN_Q_TILES = 4
DISPATCH_BUFS = 3
VMEM_LIMIT = 48 * 1024 * 1024


def _cparams(n_axes):
    return pltpu.CompilerParams(
        dimension_semantics=("arbitrary",) * n_axes, vmem_limit_bytes=VMEM_LIMIT)


def _norm_mod(x, g, shift, scale):
    ms = jnp.mean(x * x, axis=-1, keepdims=True)
    return (x * lax.rsqrt(ms + EPS) * g) * (1.0 + scale) + shift


def _silu(x):
    return x * jax.nn.sigmoid(x)


def _mod_kernel(c_ref, w_ref, b_ref, o_ref):
    a = _silu(c_ref[...])
    o_ref[0] = jnp.dot(a.astype(BF16), w_ref[0].astype(BF16),
                       preferred_element_type=F32) + b_ref[0]


def _modulation(c_rows, mod_w, mod_b):
    depth, d, n = mod_w.shape
    tn = 1536
    rows = c_rows.shape[0]
    return pl.pallas_call(
        _mod_kernel,
        grid=(depth, n // tn),
        in_specs=[
            pl.BlockSpec((rows, d), lambda l, j: (0, 0)),
            pl.BlockSpec((1, d, tn), lambda l, j: (l, 0, j)),
            pl.BlockSpec((1, 1, tn), lambda l, j: (l, 0, j)),
        ],
        out_specs=pl.BlockSpec((1, rows, tn), lambda l, j: (l, 0, j)),
        out_shape=jax.ShapeDtypeStruct((depth, rows, n), F32),
        compiler_params=_cparams(2),
        name="modulation",
    )(c_rows, mod_w, mod_b.reshape(depth, 1, n))


def _even_kernel(xc_ref, xl_ref, xp_ref, xn_ref, mod_ref, g_ref, win_ref, dww_ref, dwb_ref, lng_ref,
                 lnb_ref, pw_ref, ps_ref, wout_ref, g2_ref, rwt_ref, rb_ref, tri_ref,
                 o_ref, h2_ref, grp_ref, rank_ref, cnt_ref,
                 h_s, glu_s, pool_s, pres_s, mix_s, sc_s, carry_s, *, n_lat):
    s = pl.program_id(1)
    nt = pl.num_programs(1)
    d_conv = glu_s.shape[2]
    mod = mod_ref[0, 0]
    shift, scale, gate = mod[0:1], mod[1:2], mod[2:3]
    g = g_ref[...]
    xm = jnp.where(s == 0, xc_ref[0], xl_ref[0])
    h_s[0:HALO, :] = _norm_mod(xp_ref[0], g, shift, scale).astype(BF16)
    h_s[HALO:HALO + TS, :] = _norm_mod(xm, g, shift, scale).astype(BF16)
    h_s[HALO + TS:, :] = _norm_mod(xn_ref[0], g, shift, scale).astype(BF16)

    rows = HALO + TS + HALO
    ridx = lax.broadcasted_iota(I32, (rows, 1), 0)
    prev_ok = s >= 2
    next_ok = jnp.logical_and(s >= 1, s <= nt - 2)
    valid = jnp.logical_and(jnp.logical_or(ridx >= HALO, prev_ok),
                            jnp.logical_or(ridx < HALO + TS, next_ok))

    h = h_s[...]
    a = jnp.dot(h, win_ref[:, 0:d_conv], preferred_element_type=F32)
    gt = jnp.dot(h, win_ref[:, d_conv:2 * d_conv], preferred_element_type=F32)
    glu_s[0] = jnp.where(valid, a * jax.nn.sigmoid(gt), 0.0)
    for j in range(1, SUBLANES):
        glu_s[j, 0:rows - SUBLANES, :] = glu_s[0, pl.ds(j, rows - SUBLANES), :]
    pu = jnp.dot(h, win_ref[:, 2 * d_conv:], preferred_element_type=F32)
    pool_s[...] = jnp.where(valid, pu, 0.0)

    seq_pos0 = jnp.where(s == 0, 0, (s - 1) * TS)
    seq_len = jnp.where(s == 0, TS, n_lat)
    half = CONV_WIDTH // 2
    rc_rows = 64
    n_cc = d_conv // LANES
    for rc in range(TS // rc_rows):
        r0 = HALO + rc * rc_rows
        ys = []
        for cc in range(n_cc):
            cs = slice(cc * LANES, (cc + 1) * LANES)
            acc = jnp.zeros((rc_rows, LANES), F32) + dwb_ref[:, cs]
            for k in range(CONV_WIDTH):
                sh = (r0 + k - half) % SUBLANES
                acc = acc + dww_ref[k:k + 1, cs] * glu_s[sh, pl.ds(r0 + k - half - sh, rc_rows), cs]
            ys.append(acc)
        mu = sum(jnp.sum(y, axis=-1, keepdims=True) for y in ys) / d_conv
        var = sum(jnp.sum((y - mu) * (y - mu), axis=-1, keepdims=True) for y in ys) / d_conv
        rs = lax.rsqrt(var + EPS)
        for cc in range(n_cc):
            cs = slice(cc * LANES, (cc + 1) * LANES)
            z = (ys[cc] - mu) * rs * lng_ref[:, cs] + lnb_ref[:, cs]
            mix_s[rc * rc_rows:(rc + 1) * rc_rows, cs] = _silu(z).astype(BF16)
        pos = seq_pos0 + rc * rc_rows + lax.broadcasted_iota(I32, (rc_rows, 1), 0)
        for gi, w in enumerate(POOL_WINDOWS):
            cs = slice(gi * LANES, (gi + 1) * LANES)
            ssum = pool_s[pl.ds(r0 - w // 2, rc_rows), cs]
            for j in range(1 - w // 2, w // 2):
                ssum = ssum + pool_s[pl.ds(r0 + j, rc_rows), cs]
            lo = jnp.clip(pos - w // 2, 0, seq_len)
            hi = jnp.clip(pos + w // 2, 0, seq_len)
            cnt = (hi - lo).astype(F32)
            res = ssum / cnt - pool_s[pl.ds(r0, rc_rows), cs]
            pres_s[rc * rc_rows:(rc + 1) * rc_rows, cs] = res.astype(BF16)

    yp = jnp.dot(pres_s[...], pw_ref[...], preferred_element_type=F32) * ps_ref[...]
    mix_s[:, d_conv:] = yp.astype(BF16)
    y = jnp.dot(mix_s[...], wout_ref[...], preferred_element_type=F32)
    x_new = xm + gate * y
    o_ref[0] = x_new
    _route_tile(x_new, mod, g2_ref, rwt_ref, rb_ref, tri_ref, h2_ref, grp_ref, rank_ref, cnt_ref,
                sc_s, carry_s)


def _even_layer(ctx, x, modtab, norm_g, w_in, dw_w, dw_b, ln_g, ln_b, pool_w, pool_scale, w_out,
                route):
    b, n_lat, d = x.shape
    assert ctx.shape[1] == TS
    s_tot = TS + n_lat
    nt = s_tot // TS
    r_args, r_in_specs, r_out_specs, r_out_shape, r_scratch = _route_plumbing(b, nt, d, *route)
    d_conv = dw_w.shape[1]
    d_pool = pool_scale.shape[0]
    assert d_pool == len(POOL_WINDOWS) * LANES and pool_w.shape[1] == LANES
    pool_bd = jnp.zeros((d_pool, d_pool), F32)
    for gi in range(len(POOL_WINDOWS)):
        pool_bd = pool_bd.at[gi * LANES:(gi + 1) * LANES, gi * LANES:(gi + 1) * LANES].set(pool_w[gi])
    hb = TS // HALO
    n_hblk = n_lat // HALO
    const2 = lambda bi, si: (0, 0)
    return pl.pallas_call(
        functools.partial(_even_kernel, n_lat=n_lat),
        grid=(b, nt),
        in_specs=[
            pl.BlockSpec((1, TS, d), lambda bi, si: (bi, 0, 0)),
            pl.BlockSpec((1, TS, d), lambda bi, si: (bi, jnp.maximum(si - 1, 0), 0)),
            pl.BlockSpec((1, HALO, d), lambda bi, si: (bi, jnp.maximum((si - 1) * hb - 1, 0), 0)),
            pl.BlockSpec((1, HALO, d), lambda bi, si: (bi, jnp.clip(si * hb, 0, n_hblk - 1), 0)),
            pl.BlockSpec((1, 1, N_MOD, d), lambda bi, si: (bi, jnp.minimum(si, 1), 0, 0)),
            pl.BlockSpec((1, d), const2),
            pl.BlockSpec(w_in.shape, const2),
            pl.BlockSpec(dw_w.shape, const2),
            pl.BlockSpec((1, d_conv), const2),
            pl.BlockSpec((1, d_conv), const2),
            pl.BlockSpec((1, d_conv), const2),
            pl.BlockSpec((d_pool, d_pool), const2),
            pl.BlockSpec((1, d_pool), const2),
            pl.BlockSpec(w_out.shape, const2),
        ] + r_in_specs,
        out_specs=[pl.BlockSpec((1, TS, d), lambda bi, si: (bi, si, 0))] + r_out_specs,
        out_shape=[jax.ShapeDtypeStruct((b, s_tot, d), F32)] + r_out_shape,
        scratch_shapes=[
            pltpu.VMEM((TS + 2 * HALO, d), BF16),
            pltpu.VMEM((SUBLANES, TS + 2 * HALO, d_conv), F32),
            pltpu.VMEM((TS + 2 * HALO, d_pool), F32),
            pltpu.VMEM((TS, d_pool), BF16),
            pltpu.VMEM((TS, d_conv + d_pool), BF16),
        ] + r_scratch,
        compiler_params=_cparams(2),
        name="even_mixer",
    )(ctx, x, x, x, modtab, norm_g.reshape(1, d), w_in.astype(BF16), dw_w, dw_b.reshape(1, -1),
      ln_g.reshape(1, -1), ln_b.reshape(1, -1), pool_bd.astype(BF16), pool_scale.reshape(1, -1),
      w_out.astype(BF16), *r_args)


def _route_tile(x, mod, g_ref, rwt_ref, rb_ref, tri_ref, h_ref, grp_ref, rank_ref, cnt_ref, sc_s,
                carry_s):
    first = jnp.logical_and(pl.program_id(0) == 0, pl.program_id(1) == 0)

    @pl.when(first)
    def _():
        carry_s[...] = jnp.zeros_like(carry_s)

    h = _norm_mod(x, g_ref[...], mod[3:4], mod[4:5])
    h_ref[0] = h
    logits = lax.dot_general(rwt_ref[...], h.astype(BF16), (((1,), (1,)), ((), ())),
                             preferred_element_type=F32)
    biased = jax.nn.sigmoid(logits) + rb_ref[...]
    n_half = TS // LANES
    for hh in range(n_half):
        sc_s[hh] = biased[:, hh * LANES:(hh + 1) * LANES]
    a, b, c, d = [
        jnp.concatenate([sc_s[hh, pl.ds(j, N_GROUPS, stride=EXPERTS_PER_GROUP), :]
                         for hh in range(n_half)], axis=-1)
        for j in range(EXPERTS_PER_GROUP)]
    top2 = jnp.maximum(jnp.maximum(jnp.maximum(a + b, a + c), jnp.maximum(a + d, b + c)),
                       jnp.maximum(b + d, c + d))
    gmax = jnp.max(top2, axis=0, keepdims=True)
    gi = lax.broadcasted_iota(I32, top2.shape, 0)
    sel = jnp.min(jnp.where(top2 == gmax, gi, N_GROUPS), axis=0, keepdims=True)
    onehot = gi == sel
    ohf = jnp.where(onehot, 1.0, 0.0)
    prefix = jnp.dot(ohf.astype(BF16), tri_ref[...], preferred_element_type=F32)
    carry = carry_s[:, 0:1]
    rank = jnp.sum(jnp.where(onehot, carry + prefix - 1.0, 0.0), axis=0, keepdims=True)
    carry_s[...] = carry_s[...] + jnp.sum(ohf, axis=1, keepdims=True)
    grp_ref[0] = sel
    rank_ref[0] = rank.astype(I32)
    cnt_ref[...] = carry_s[...]


def _route_plumbing(b, nt, d, norm_g, router_w, router_b):
    const2 = lambda bi, si: (0, 0)
    flat3 = lambda bi, si: (bi * nt + si, 0, 0)
    tri = (jnp.arange(TS)[:, None] <= jnp.arange(TS)[None, :]).astype(BF16)
    args = [norm_g.reshape(1, d), router_w.T.astype(BF16), router_b.reshape(N_EXPERTS, 1), tri]
    in_specs = [pl.BlockSpec((1, d), const2), pl.BlockSpec((N_EXPERTS, d), const2),
                pl.BlockSpec((N_EXPERTS, 1), const2), pl.BlockSpec((TS, TS), const2)]
    out_specs = [pl.BlockSpec((1, TS, d), lambda bi, si: (bi, si, 0)),
                 pl.BlockSpec((1, 1, TS), flat3), pl.BlockSpec((1, 1, TS), flat3),
                 pl.BlockSpec((N_GROUPS, LANES), const2)]
    out_shape = [jax.ShapeDtypeStruct((b, nt * TS, d), F32),
                 jax.ShapeDtypeStruct((b * nt, 1, TS), I32),
                 jax.ShapeDtypeStruct((b * nt, 1, TS), I32),
                 jax.ShapeDtypeStruct((N_GROUPS, LANES), F32)]
    scratch = [pltpu.VMEM((TS // LANES, N_EXPERTS, LANES), F32), pltpu.VMEM((N_GROUPS, LANES), F32)]
    return args, in_specs, out_specs, out_shape, scratch


def _dispatch_kernel(zrow_ref, pos_ref, src_ref, dst_ref, buf, ld_sem, st_sem, z_sem):
    i = pl.program_id(0)
    n = pl.num_programs(0)
    slot = lax.rem(i, DISPATCH_BUFS)
    nxt = lax.rem(i + 1, DISPATCH_BUFS)
    tile_groups = TS // SUBLANES

    def load(tile, which):
        return pltpu.make_async_copy(src_ref.at[pl.ds(tile * tile_groups, tile_groups)],
                                     buf.at[which], ld_sem.at[which])

    def wait_scatter(which):
        pltpu.make_async_copy(buf.at[which], dst_ref.at[pl.ds(0, tile_groups)],
                              st_sem.at[which]).wait()

    @pl.when(i == 0)
    def _():
        load(i, slot).start()
        zsrc = buf.at[DISPATCH_BUFS - 1]
        zsrc[...] = jnp.zeros(zsrc.shape, F32)
        for z in range(zrow_ref.shape[0]):
            @pl.when(zrow_ref[z] >= 0)
            def _():
                pltpu.make_async_copy(zsrc, dst_ref.at[pl.ds(zrow_ref[z], tile_groups)],
                                      z_sem).start()
        for z in range(zrow_ref.shape[0]):
            @pl.when(zrow_ref[z] >= 0)
            def _():
                pltpu.make_async_copy(zsrc, dst_ref.at[pl.ds(0, tile_groups)], z_sem).wait()

    load(i, slot).wait()

    @pl.when(i + 1 >= DISPATCH_BUFS)
    def _():
        wait_scatter(nxt)

    @pl.when(i + 1 < n)
    def _():
        load(i + 1, nxt).start()

    def issue(r8, carry):
        for j in range(SUBLANES):
            p = pos_ref[0, 0, r8 * SUBLANES + j]
            dst_row = dst_ref.at[lax.shift_right_logical(p, 3), pl.ds(p & (SUBLANES - 1), 1)]
            pltpu.make_async_copy(buf.at[slot, r8, pl.ds(j, 1)], dst_row, st_sem.at[slot]).start()
        return carry

    lax.fori_loop(0, tile_groups, issue, 0)

    @pl.when(i == n - 1)
    def _():
        for back in range(DISPATCH_BUFS - 1):
            @pl.when(i - back >= 0)
            def _():
                wait_scatter(lax.rem(i - back + DISPATCH_BUFS, DISPATCH_BUFS))


def _moe_dispatch(h_flat, pos, zero_rows, n_rows):
    t, d = h_flat.shape
    n_tiles = t // TS
    assert TS == TMG and n_tiles >= DISPATCH_BUFS
    zero_groups = jnp.where(zero_rows >= 0, zero_rows // SUBLANES, -1).astype(I32)
    return pl.pallas_call(
        _dispatch_kernel,
        grid=(n_tiles,),
        in_specs=[
            pl.BlockSpec(memory_space=pltpu.SMEM),
            pl.BlockSpec((1, 1, TS), lambda i: (i, 0, 0), memory_space=pltpu.SMEM),
            pl.BlockSpec(memory_space=pl.ANY),
        ],
        out_specs=pl.BlockSpec(memory_space=pl.ANY),
        out_shape=jax.ShapeDtypeStruct((n_rows // SUBLANES, SUBLANES, d), F32),
        scratch_shapes=[pltpu.VMEM((DISPATCH_BUFS, TS // SUBLANES, SUBLANES, d), F32),
                        pltpu.SemaphoreType.DMA((DISPATCH_BUFS,)),
                        pltpu.SemaphoreType.DMA((DISPATCH_BUFS,)),
                        pltpu.SemaphoreType.DMA(())],
        compiler_params=_cparams(1),
        name="moe_dispatch",
    )(zero_groups, pos, h_flat.reshape(t // SUBLANES, SUBLANES, d))


def _ffn_kernel(tg_ref, tv_ref, xs_ref, rw_ref, rb_ref, wg32_ref, wu32_ref, wd32_ref, o_ref,
                wg_ref, wu_ref, wd_ref):
    i = pl.program_id(0)

    @pl.when(jnp.logical_or(i == 0, tg_ref[i] != tg_ref[jnp.maximum(i - 1, 0)]))
    def _():
        wg_ref[...] = wg32_ref[...].astype(BF16)
        wu_ref[...] = wu32_ref[...].astype(BF16)
        wd_ref[...] = wd32_ref[...].astype(BF16)

    @pl.when(tv_ref[i] == 0)
    def _():
        o_ref[...] = jnp.zeros_like(o_ref)

    @pl.when(tv_ref[i] != 0)
    def _():
        xb = xs_ref[...].astype(BF16)
        logits = jnp.dot(xb, rw_ref[0], preferred_element_type=F32)
        sc = jax.nn.sigmoid(logits)
        bs = sc + rb_ref[0]
        s_col = [sc[:, j:j + 1] for j in range(EXPERTS_PER_GROUP)]
        b_col = [bs[:, j:j + 1] for j in range(EXPERTS_PER_GROUP)]
        sel = []
        for j in range(EXPERTS_PER_GROUP):
            beaten = jnp.zeros_like(b_col[j])
            for k in range(EXPERTS_PER_GROUP):
                if k == j:
                    continue
                wins = (b_col[k] >= b_col[j]) if k < j else (b_col[k] > b_col[j])
                beaten = beaten + jnp.where(wins, 1.0, 0.0)
            sel.append(beaten < 2.0)
        den = sum(jnp.where(sel[j], s_col[j], 0.0) for j in range(EXPERTS_PER_GROUP))
        hes = []
        for j in range(EXPERTS_PER_GROUP):
            cj = jnp.where(sel[j], s_col[j] / den, 0.0)
            gj = jnp.dot(xb, wg_ref[0, j], preferred_element_type=F32)
            uj = jnp.dot(xb, wu_ref[0, j], preferred_element_type=F32)
            hes.append((_silu(gj) * uj * cj).astype(BF16))
        he = jnp.concatenate(hes, axis=-1)
        d_e = wd_ref.shape[2]
        wd = wd_ref[0].reshape(EXPERTS_PER_GROUP * d_e, wd_ref.shape[3])
        o_ref[...] = jnp.dot(he, wd, preferred_element_type=F32)


def _moe_ffn(xs, tile_group, tile_valid, rw_g, rb_g, w_gate, w_up, w_down, layer):
    n_rows, d = xs.shape
    d_e = w_gate.shape[-1]
    n_tiles = n_rows // TMG
    epg = EXPERTS_PER_GROUP
    grid_spec = pltpu.PrefetchScalarGridSpec(
        num_scalar_prefetch=2,
        grid=(n_tiles,),
        in_specs=[
            pl.BlockSpec((TMG, d), lambda i, tg, tv: (i, 0)),
            pl.BlockSpec((1, d, LANES), lambda i, tg, tv: (tg[i], 0, 0)),
            pl.BlockSpec((1, 1, LANES), lambda i, tg, tv: (tg[i], 0, 0)),
            pl.BlockSpec((1, epg, d, d_e), lambda i, tg, tv: (layer, tg[i], 0, 0)),
            pl.BlockSpec((1, epg, d, d_e), lambda i, tg, tv: (layer, tg[i], 0, 0)),
            pl.BlockSpec((1, epg, d_e, d), lambda i, tg, tv: (layer, tg[i], 0, 0)),
        ],
        out_specs=pl.BlockSpec((TMG, d), lambda i, tg, tv: (i, 0)),
        scratch_shapes=[pltpu.VMEM((1, epg, d, d_e), BF16), pltpu.VMEM((1, epg, d, d_e), BF16),
                        pltpu.VMEM((1, epg, d_e, d), BF16)],
    )
    return pl.pallas_call(
        _ffn_kernel,
        grid_spec=grid_spec,
        out_shape=jax.ShapeDtypeStruct((n_rows, d), F32),
        compiler_params=_cparams(1),
        name="moe_ffn",
    )(tile_group, tile_valid, xs, rw_g, rb_g, w_gate, w_up, w_down)


def _combine_kernel(pos_ref, posn_ref, x_ref, mod_ref, ys_ref, o_ref, buf, sem):
    i = pl.program_id(0) * pl.num_programs(1) + pl.program_id(1)
    n = pl.num_programs(0) * pl.num_programs(1)
    slot = lax.rem(i, 2)

    tile_groups = TS // SUBLANES

    def gather(p_ref, which):
        def issue(r8, carry):
            for j in range(SUBLANES):
                p = p_ref[0, 0, r8 * SUBLANES + j]
                src_row = ys_ref.at[lax.shift_right_logical(p, 3), pl.ds(p & (SUBLANES - 1), 1)]
                pltpu.make_async_copy(src_row, buf.at[which, r8, pl.ds(j, 1)], sem.at[which]).start()
            return carry
        lax.fori_loop(0, tile_groups, issue, 0)

    @pl.when(i == 0)
    def _():
        gather(pos_ref, slot)

    @pl.when(i + 1 < n)
    def _():
        gather(posn_ref, 1 - slot)

    pltpu.make_async_copy(ys_ref.at[pl.ds(0, tile_groups)], buf.at[slot], sem.at[slot]).wait()
    gate = mod_ref[0, 0][5:6]
    o_ref[0] = x_ref[0] + gate * buf[slot].reshape(TS, buf.shape[-1])


def _moe_combine(x, modtab, pos, ys, kind_of_tile):
    b, s_tot, d = x.shape
    nt = s_tot // TS
    n_tiles = b * nt
    cur = lambda bi, si: (bi * nt + si, 0, 0)
    nxt = lambda bi, si: (jnp.minimum(bi * nt + si + 1, n_tiles - 1), 0, 0)
    return pl.pallas_call(
        _combine_kernel,
        grid=(b, nt),
        in_specs=[
            pl.BlockSpec((1, 1, TS), cur, memory_space=pltpu.SMEM),
            pl.BlockSpec((1, 1, TS), nxt, memory_space=pltpu.SMEM),
            pl.BlockSpec((1, TS, d), lambda bi, si: (bi, si, 0)),
            pl.BlockSpec((1, 1, N_MOD, d), lambda bi, si: (bi, kind_of_tile(si), 0, 0)),
            pl.BlockSpec(memory_space=pl.ANY),
        ],
        out_specs=pl.BlockSpec((1, TS, d), lambda bi, si: (bi, si, 0)),
        out_shape=jax.ShapeDtypeStruct((b, s_tot, d), F32),
        scratch_shapes=[pltpu.VMEM((2, TS // SUBLANES, SUBLANES, d), F32),
                        pltpu.SemaphoreType.DMA((2,))],
        compiler_params=_cparams(2),
        name="moe_combine",
    )(pos, pos, x, modtab, ys.reshape(ys.shape[0] // SUBLANES, SUBLANES, d))


def _moe_layer(x, routed, modtab, router_w, router_b, w_gate, w_up, w_down, layer, kind_of_tile):
    b, s_tot, d = x.shape
    t = b * s_tot
    h, grp, rank, cnt = routed
    counts = cnt[:, 0].astype(I32)
    padded = ((counts + TMG - 1) // TMG) * TMG
    ends = jnp.cumsum(padded)
    starts = ends - padded
    pos = starts[grp] + rank
    n_rows = t + N_GROUPS * TMG
    tile_row0 = jnp.arange(n_rows // TMG, dtype=I32) * TMG
    tile_group = jnp.minimum(jnp.sum(tile_row0[:, None] >= ends[None, :], axis=1),
                             N_GROUPS - 1).astype(I32)
    tile_valid = (tile_row0 < ends[-1]).astype(I32)
    tail_rows = jnp.minimum(ends[-1] + jnp.arange(N_GROUPS, dtype=I32) * TMG, n_rows - TMG)
    zero_rows = jnp.concatenate([jnp.where(padded > 0, ends - TMG, tail_rows[-1]), tail_rows])
    n_z = zero_rows.shape[0]
    repeat = jnp.any((zero_rows[:, None] == zero_rows[None, :])
                     & (jnp.arange(n_z)[None, :] < jnp.arange(n_z)[:, None]), axis=1)
    zero_rows = jnp.where(repeat, -1, zero_rows)
    xs = _moe_dispatch(h.reshape(t, d), pos, zero_rows.astype(I32), n_rows)
    epg = EXPERTS_PER_GROUP
    rw_g = jnp.pad(router_w.reshape(d, N_GROUPS, epg).transpose(1, 0, 2),
                   ((0, 0), (0, 0), (0, LANES - epg))).astype(BF16)
    rb_g = jnp.pad(router_b.reshape(N_GROUPS, 1, epg), ((0, 0), (0, 0), (0, LANES - epg)))
    ys = _moe_ffn(xs.reshape(n_rows, d), tile_group, tile_valid, rw_g, rb_g, w_gate, w_up, w_down,
                  layer)
    return _moe_combine(x, modtab, pos, ys, kind_of_tile)


def _rope_pad(w, axis):
    x1, x2 = jnp.split(w, 2, axis=axis)
    z = jnp.zeros_like(x1)
    return jnp.concatenate([x1, z, x2, z], axis=axis)


def _oddproj_kernel(x_ref, mod_ref, g_ref, win_ref, gcq_ref, wuq_ref, gckv_ref, wukv_ref, gqm_ref,
                    gkm_ref, gqd_ref, gkd_ref, cs_ref, sna_ref, snb_ref, csm_ref, snm_ref,
                    qm_ref, km_ref, vm_ref, qd_ref, kd_ref, vd_ref, p_s, qm_s, kv_s):
    mod = mod_ref[0, 0]
    h = _norm_mod(x_ref[0], g_ref[...], mod[0:1], mod[1:2]).astype(BF16)
    q_lora = gcq_ref.shape[1]
    kv_lora = gckv_ref.shape[1]
    n_qd = DIFF_HEADS * 2 * DIFF_QK
    o_qd = q_lora
    o_ckv = o_qd + n_qd
    o_kr = o_ckv + kv_lora
    o_kd = o_kr + LANES
    o_vd = o_kd + n_qd
    lane = lax.broadcasted_iota(I32, (1, LANES), 1)
    low = lane < DIFF_QK
    d_mla = QK_NOPE + QK_ROPE

    def rms(x, g, n):
        return x * lax.rsqrt(jnp.sum(x * x, axis=-1, keepdims=True) / n + EPS) * g

    def rope_mla(y):
        return y * csm_ref[...] + pltpu.roll(y, LANES // 2, 1) * snm_ref[...]

    def diff_cols(p, g):
        sq = p * p
        s_all = jnp.sum(sq, axis=-1, keepdims=True)
        s_lo = jnp.sum(jnp.where(low, sq, 0.0), axis=-1, keepdims=True)
        r = jnp.where(low, lax.rsqrt(s_lo / DIFF_QK + EPS),
                      lax.rsqrt((s_all - s_lo) / DIFF_QK + EPS))
        y = p * r * g
        return (y * cs_ref[...] + pltpu.roll(y, LANES - DIFF_QK // 2, 1) * sna_ref[...]
                + pltpu.roll(y, DIFF_QK // 2, 1) * snb_ref[...])

    p_s[...] = jnp.dot(h, win_ref[...], preferred_element_type=F32)
    cq = rms(p_s[:, 0:q_lora], gcq_ref[...], q_lora)
    qm_s[...] = jnp.dot(cq.astype(BF16), wuq_ref[...], preferred_element_type=F32)
    ckv = rms(p_s[:, o_ckv:o_ckv + kv_lora], gckv_ref[...], kv_lora)
    kv_s[...] = jnp.dot(ckv.astype(BF16), wukv_ref[...], preferred_element_type=F32)

    for hd in range(DIFF_HEADS):
        y = diff_cols(p_s[:, o_qd + LANES * hd:o_qd + LANES * (hd + 1)], gqd_ref[...])
        y = y * (DIFF_SCALE * LOG2E)
        qd_ref[0, 2 * hd] = jnp.where(low, y, 0.0).astype(BF16)
        qd_ref[0, 2 * hd + 1] = jnp.where(low, 0.0, y).astype(BF16)
        kd_ref[0, hd] = diff_cols(p_s[:, o_kd + LANES * hd:o_kd + LANES * (hd + 1)],
                                  gkd_ref[...]).astype(BF16)
        vd_ref[0, hd] = p_s[:, o_vd + DIFF_V * hd:o_vd + DIFF_V * (hd + 1)].T.astype(BF16)

    kr = p_s[:, o_kr:o_kr + LANES]
    for hd in range(MLA_HEADS):
        qh = rms(qm_s[:, 2 * LANES * hd:2 * LANES * (hd + 1)], gqm_ref[...], d_mla)
        qh = jnp.concatenate([qh[:, :LANES], rope_mla(qh[:, LANES:])], axis=-1)
        qm_ref[0, hd] = (qh * (MLA_SCALE * LOG2E)).astype(BF16)
        kcat = jnp.concatenate([kv_s[:, 2 * LANES * hd:2 * LANES * hd + LANES], kr], axis=-1)
        kh = rms(kcat, gkm_ref[...], d_mla)
        km_ref[0, hd] = jnp.concatenate([kh[:, :LANES], rope_mla(kh[:, LANES:])], axis=-1).astype(BF16)
        vm_ref[0, hd] = kv_s[:, 2 * LANES * hd + LANES:2 * LANES * (hd + 1)].T.astype(BF16)


def _odd_project(xa, modtab, norm_g, w_in, g_cq, w_uq, g_ckv, w_ukv, g_mla, g_diff, n_ctx):
    b, s_tot, d = xa.shape
    nt = s_tot // TS
    n_lat = s_tot - n_ctx
    q_lora = g_cq.shape[0]
    kv_lora = g_ckv.shape[0]
    n_qd = DIFF_HEADS * 2 * DIFF_QK
    d_mla = QK_NOPE + QK_ROPE
    o = 0
    w_cq = w_in[:, o:o + q_lora]; o += q_lora
    w_qd = w_in[:, o:o + n_qd]; o += n_qd
    w_ckv = w_in[:, o:o + kv_lora]; o += kv_lora
    w_kr = w_in[:, o:o + QK_ROPE]; o += QK_ROPE
    w_kd = w_in[:, o:o + n_qd]; o += n_qd
    w_vd = w_in[:, o:]
    w_in_p = jnp.concatenate([w_cq, w_qd, w_ckv, _rope_pad(w_kr, 1), w_kd, w_vd], axis=1).astype(BF16)
    wuq = w_uq.reshape(q_lora, MLA_HEADS, d_mla)
    wuq_p = jnp.concatenate([wuq[..., :QK_NOPE], _rope_pad(wuq[..., QK_NOPE:], 2)],
                            axis=-1).reshape(q_lora, MLA_HEADS * 2 * LANES).astype(BF16)

    def pad_gain(g):
        return jnp.concatenate([g[:QK_NOPE], _rope_pad(g[QK_NOPE:], 0)]).reshape(1, 2 * LANES)

    rows = n_lat // GRID_W
    row = jnp.repeat(jnp.arange(rows, dtype=F32), GRID_W)
    col = jnp.tile(jnp.arange(GRID_W, dtype=F32), rows)
    axis_dim = QK_ROPE // 2
    inv_freq = ROPE_BASE ** (-jnp.arange(0, axis_dim, 2, dtype=F32) / axis_dim)
    ang = jnp.concatenate([row[:, None] * inv_freq, col[:, None] * inv_freq], axis=-1)
    half = QK_ROPE // 2
    cos = jnp.concatenate([jnp.ones((n_ctx, half), F32), jnp.cos(ang)], axis=0)
    sin = jnp.concatenate([jnp.zeros((n_ctx, half), F32), jnp.sin(ang)], axis=0)
    z = jnp.zeros_like(sin)
    cs_d = jnp.tile(cos, (1, LANES // half))
    sna_d = jnp.tile(jnp.concatenate([-sin, z], axis=1), (1, LANES // QK_ROPE))
    snb_d = jnp.tile(jnp.concatenate([z, sin], axis=1), (1, LANES // QK_ROPE))
    cs_m = jnp.concatenate([cos, z, cos, z], axis=1)
    sn_m = jnp.concatenate([-sin, z, sin, z], axis=1)

    const2 = lambda bi, si: (0, 0)
    tok = lambda bi, si: (si, 0)
    kv4 = lambda bi, si: (bi, 0, si, 0)
    vt4 = lambda bi, si: (bi, 0, 0, si)
    q4 = lambda bi, si: (bi, 0, jnp.maximum(si - n_ctx // TS, 0), 0)
    full = lambda a: pl.BlockSpec(a.shape, const2)
    gq_d = jnp.tile(g_diff[0], LANES // DIFF_QK).reshape(1, LANES)
    gk_d = jnp.tile(g_diff[1], LANES // DIFF_QK).reshape(1, LANES)
    weights = [norm_g.reshape(1, d), w_in_p, g_cq.reshape(1, -1), wuq_p, g_ckv.reshape(1, -1),
               w_ukv.astype(BF16), pad_gain(g_mla[0]), pad_gain(g_mla[1]), gq_d, gk_d]
    args = [xa, modtab, *weights, cs_d, sna_d, snb_d, cs_m, sn_m]
    in_specs = [
        pl.BlockSpec((1, TS, d), lambda bi, si: (bi, si, 0)),
        pl.BlockSpec((1, 1, N_MOD, d), lambda bi, si: (bi, jnp.minimum(si, 1), 0, 0)),
    ] + [full(a) for a in weights] + [pl.BlockSpec((TS, LANES), tok)] * 5
    hm, hd = MLA_HEADS, DIFF_HEADS
    return pl.pallas_call(
        _oddproj_kernel,
        grid=(b, nt),
        in_specs=in_specs,
        out_specs=[
            pl.BlockSpec((1, hm, TS, 2 * LANES), q4),
            pl.BlockSpec((1, hm, TS, 2 * LANES), kv4),
            pl.BlockSpec((1, hm, V_HEAD, TS), vt4),
            pl.BlockSpec((1, 2 * hd, TS, LANES), q4),
            pl.BlockSpec((1, hd, TS, LANES), kv4),
            pl.BlockSpec((1, hd, DIFF_V, TS), vt4),
        ],
        out_shape=[
            jax.ShapeDtypeStruct((b, hm, n_lat, 2 * LANES), BF16),
            jax.ShapeDtypeStruct((b, hm, s_tot, 2 * LANES), BF16),
            jax.ShapeDtypeStruct((b, hm, V_HEAD, s_tot), BF16),
            jax.ShapeDtypeStruct((b, 2 * hd, n_lat, LANES), BF16),
            jax.ShapeDtypeStruct((b, hd, s_tot, LANES), BF16),
            jax.ShapeDtypeStruct((b, hd, DIFF_V, s_tot), BF16),
        ],
        scratch_shapes=[pltpu.VMEM((TS, w_in_p.shape[1]), F32),
                        pltpu.VMEM((TS, wuq_p.shape[1]), F32),
                        pltpu.VMEM((TS, w_ukv.shape[1]), F32)],
        compiler_params=_cparams(2),
        name="odd_project",
    )(*args)


def _attn_kernel(q_ref, k_ref, vt_ref, o_ref, s_s, p_s, *, rep):
    hq = q_ref.shape[1]
    n_items = hq * (q_ref.shape[2] // TQ)
    n_chunks = k_ref.shape[2] // TK
    groups = TK // SUBLANES

    def q_rows(i):
        return pl.ds((i // hq) * TQ, TQ)

    def scores(i):
        u, h = i % 2, i % hq
        s_s[u] = lax.dot_general(k_ref[0, h // rep], q_ref[0, h, q_rows(i), :],
                                 (((1,), (1,)), ((), ())), preferred_element_type=F32)
        m8 = jnp.full((SUBLANES, TQ), -jnp.inf, F32)
        for c in range(n_chunks):
            st = s_s[u, pl.ds(c * TK, TK), :]
            m8 = jnp.maximum(m8, jnp.max(st.reshape(groups, SUBLANES, TQ), axis=0))
        return jnp.max(m8, axis=0, keepdims=True)

    def exponentials(i, row_max):
        u = i % 2
        l8 = jnp.zeros((SUBLANES, TQ), F32)
        for c in range(n_chunks):
            p = jnp.exp2(s_s[u, pl.ds(c * TK, TK), :] - row_max)
            l8 = l8 + jnp.sum(p.reshape(groups, SUBLANES, TQ), axis=0)
            p_s[u, pl.ds(c * TK, TK), :] = p.astype(BF16)
        return jnp.sum(l8, axis=0, keepdims=True)

    def values(i, l):
        h = i % hq
        out_t = jnp.dot(vt_ref[0, h // rep], p_s[i % 2], preferred_element_type=F32)
        o_ref[0, h, q_rows(i), :] = (out_t / l).T.astype(o_ref.dtype)

    row_max, row_sum = {}, {}
    for t in range(n_items + 2):
        if t < n_items:
            row_max[t] = scores(t)
        if 0 <= t - 1 < n_items:
            row_sum[t - 1] = exponentials(t - 1, row_max.pop(t - 1))
        if 0 <= t - 2 < n_items:
            values(t - 2, row_sum.pop(t - 2))


def _diff_attn_kernel(q_ref, k_ref, vt_ref, dl_ref, o_ref, s_s, p_s, a1_s, *, lam_init):
    hq = q_ref.shape[1]
    n_items = hq * (q_ref.shape[2] // TQ)
    n_chunks = k_ref.shape[2] // TK
    groups = TK // SUBLANES
    dl = dl_ref[...]
    lam = (jnp.exp(jnp.sum(dl[0:1] * dl[1:2], axis=-1, keepdims=True))
           - jnp.exp(jnp.sum(dl[2:3] * dl[3:4], axis=-1, keepdims=True)) + lam_init)

    def q_rows(i):
        return pl.ds((i // hq) * TQ, TQ)

    def scores(i):
        u, slot = i % 2, i % hq
        s_s[u] = lax.dot_general(k_ref[0, slot // 2], q_ref[0, slot, q_rows(i), :],
                                 (((1,), (1,)), ((), ())), preferred_element_type=F32)
        m8 = jnp.full((SUBLANES, TQ), -jnp.inf, F32)
        for c in range(n_chunks):
            st = s_s[u, pl.ds(c * TK, TK), :]
            m8 = jnp.maximum(m8, jnp.max(st.reshape(groups, SUBLANES, TQ), axis=0))
        return jnp.max(m8, axis=0, keepdims=True)

    def exponentials(i, row_max):
        u = i % 2
        l8 = jnp.zeros((SUBLANES, TQ), F32)
        for c in range(n_chunks):
            p = jnp.exp2(s_s[u, pl.ds(c * TK, TK), :] - row_max)
            l8 = l8 + jnp.sum(p.reshape(groups, SUBLANES, TQ), axis=0)
            p_s[u, pl.ds(c * TK, TK), :] = p.astype(BF16)
        return jnp.sum(l8, axis=0, keepdims=True)

    def values(i, l):
        slot = i % hq
        a = jnp.dot(vt_ref[0, slot // 2], p_s[i % 2], preferred_element_type=F32) / l
        if slot % 2 == 0:
            a1_s[...] = a
        else:
            o_ref[0, slot // 2, q_rows(i), :] = (a1_s[...] - lam * a).T

    row_max, row_sum = {}, {}
    for t in range(n_items + 2):
        if t < n_items:
            row_max[t] = scores(t)
        if 0 <= t - 1 < n_items:
            row_sum[t - 1] = exponentials(t - 1, row_max.pop(t - 1))
        if 0 <= t - 2 < n_items:
            values(t - 2, row_sum.pop(t - 2))


def _diff_attention(q, k, vt, diff_lambda, lam_init, q_tiles):
    b, hq, lq, dk = q.shape
    hk, lk = k.shape[1], k.shape[2]
    dv = vt.shape[2]
    tq_step = q_tiles * TQ
    assert lk % TK == 0 and lq % tq_step == 0 and hq == 2 * hk
    once = pl.Buffered(1)
    return pl.pallas_call(
        functools.partial(_diff_attn_kernel, lam_init=lam_init),
        grid=(b, lq // tq_step),
        in_specs=[
            pl.BlockSpec((1, hq, tq_step, dk), lambda bi, qi: (bi, 0, qi, 0)),
            pl.BlockSpec((1, hk, lk, dk), lambda bi, qi: (bi, 0, 0, 0), pipeline_mode=once),
            pl.BlockSpec((1, hk, dv, lk), lambda bi, qi: (bi, 0, 0, 0), pipeline_mode=once),
            pl.BlockSpec(diff_lambda.shape, lambda bi, qi: (0, 0)),
        ],
        out_specs=pl.BlockSpec((1, hk, tq_step, dv), lambda bi, qi: (bi, 0, qi, 0)),
        out_shape=jax.ShapeDtypeStruct((b, hk, lq, dv), F32),
        scratch_shapes=[pltpu.VMEM((2, lk, TQ), F32), pltpu.VMEM((2, lk, TQ), BF16),
                        pltpu.VMEM((dv, TQ), F32)],
        compiler_params=_cparams(2),
        name="attn_diff",
    )(q, k, vt, diff_lambda)


def _attention(q, k, vt, out_dtype, name, q_tiles):
    b, hq, lq, dk = q.shape
    hk, lk = k.shape[1], k.shape[2]
    dv = vt.shape[2]
    tq_step = q_tiles * TQ
    assert lk % TK == 0 and lq % tq_step == 0
    once = pl.Buffered(1)
    return pl.pallas_call(
        functools.partial(_attn_kernel, rep=hq // hk),
        grid=(b, lq // tq_step),
        in_specs=[
            pl.BlockSpec((1, hq, tq_step, dk), lambda bi, qi: (bi, 0, qi, 0)),
            pl.BlockSpec((1, hk, lk, dk), lambda bi, qi: (bi, 0, 0, 0), pipeline_mode=once),
            pl.BlockSpec((1, hk, dv, lk), lambda bi, qi: (bi, 0, 0, 0), pipeline_mode=once),
        ],
        out_specs=pl.BlockSpec((1, hq, tq_step, dv), lambda bi, qi: (bi, 0, qi, 0)),
        out_shape=jax.ShapeDtypeStruct((b, hq, lq, dv), out_dtype),
        scratch_shapes=[pltpu.VMEM((2, lk, TQ), F32), pltpu.VMEM((2, lk, TQ), BF16)],
        compiler_params=_cparams(2),
        name=name,
    )(q, k, vt)


def _oddout_kernel(x_ref, mod_ref, om_ref, ad_ref, gs_ref, wout_ref, g2_ref, rwt_ref,
                   rb_ref, tri_ref, o_ref, h2_ref, grp_ref, rank_ref, cnt_ref, sc_s, carry_s, *,
                   lam_init):
    parts = [om_ref[0, hd] for hd in range(MLA_HEADS)]
    for hd in range(DIFF_HEADS):
        df = ad_ref[0, hd]
        od = df * lax.rsqrt(jnp.mean(df * df, axis=-1, keepdims=True) + EPS) * gs_ref[...]
        parts.append((od * (1.0 - lam_init)).astype(BF16))
    y = jnp.dot(jnp.concatenate(parts, axis=-1), wout_ref[...], preferred_element_type=F32)
    mod = mod_ref[0, 0]
    x_new = x_ref[0] + mod[2:3] * y
    o_ref[0] = x_new
    _route_tile(x_new, mod, g2_ref, rwt_ref, rb_ref, tri_ref, h2_ref, grp_ref, rank_ref, cnt_ref,
                sc_s, carry_s)


def _odd_output(xa, modtab, o_m, a_d, g_sub, w_out, lam_init, n_ctx, route):
    b, s_tot, d = xa.shape
    n_lat = s_tot - n_ctx
    off = n_ctx // TS
    const2 = lambda bi, si: (0, 0)
    r_args, r_in_specs, r_out_specs, r_out_shape, r_scratch = _route_plumbing(
        b, n_lat // TS, d, *route)
    return pl.pallas_call(
        functools.partial(_oddout_kernel, lam_init=lam_init),
        grid=(b, n_lat // TS),
        in_specs=[
            pl.BlockSpec((1, TS, d), lambda bi, si: (bi, si + off, 0)),
            pl.BlockSpec((1, 1, N_MOD, d), lambda bi, si: (bi, 1, 0, 0)),
            pl.BlockSpec((1, o_m.shape[1], TS, V_HEAD), lambda bi, si: (bi, 0, si, 0)),
            pl.BlockSpec((1, a_d.shape[1], TS, DIFF_V), lambda bi, si: (bi, 0, si, 0)),
            pl.BlockSpec((1, DIFF_V), const2),
            pl.BlockSpec(w_out.shape, const2),
        ] + r_in_specs,
        out_specs=[pl.BlockSpec((1, TS, d), lambda bi, si: (bi, si, 0))] + r_out_specs,
        out_shape=[jax.ShapeDtypeStruct((b, n_lat, d), F32)] + r_out_shape,
        scratch_shapes=r_scratch,
        compiler_params=_cparams(2),
        name="odd_output",
    )(xa, modtab, o_m, a_d, g_sub.reshape(1, -1), w_out.astype(BF16), *r_args)


def kernel(x, c, ctx, c_ctx, mod_w, mod_b, norm_g, even_w_in, conv_dw_w, conv_dw_b, conv_ln_g,
           conv_ln_b, pool_w, pool_scale, even_w_out, odd_w_in, mla_g_cq, mla_w_uq, mla_g_ckv,
           mla_w_ukv, qk_g_mla, qk_g_diff, diff_lambda, diff_sub_g, odd_w_out, router_w, router_b,
           moe_w_gate, moe_w_up, moe_w_down):
    b, n_lat, d = x.shape
    n_ctx = ctx.shape[1]
    assert n_ctx == TS and n_lat % TS == 0 and mod_w.shape[0] == 2

    c_rows = jnp.concatenate([c, c_ctx[None], jnp.zeros((16 - b - 1, d), F32)], axis=0)
    mods = _modulation(c_rows, mod_w, mod_b)
    modtabs = []
    for i in range(2):
        mod_l = mods[i, :b].reshape(b, N_MOD, d)
        mod_c = jnp.broadcast_to(mods[i, b].reshape(1, N_MOD, d), (b, N_MOD, d))
        modtabs.append(jnp.stack([mod_c, mod_l], axis=1))

    ctx_or_lat = lambda si: jnp.minimum(si, 1)
    lat_only = lambda si: 1

    xa, *routed = _even_layer(ctx, x, modtabs[0], norm_g[0, 0], even_w_in[0], conv_dw_w[0],
                              conv_dw_b[0], conv_ln_g[0], conv_ln_b[0], pool_w[0], pool_scale[0],
                              even_w_out[0], (norm_g[0, 1], router_w, router_b))
    xa = _moe_layer(xa, routed, modtabs[0], router_w, router_b, moe_w_gate, moe_w_up, moe_w_down,
                    0, ctx_or_lat)

    qm, km, vm, qd, kd, vd = _odd_project(xa, modtabs[1], norm_g[1, 0], odd_w_in[0], mla_g_cq[0],
                                          mla_w_uq[0], mla_g_ckv[0], mla_w_ukv[0], qk_g_mla[0],
                                          qk_g_diff[0], n_ctx)
    lam_init = 0.8 - 0.6 * math.exp(-0.3 * 1)
    q_tiles = min(ATTN_Q_TILES, n_lat // TQ)
    o_m = _attention(qm, km, vm, BF16, "attn_mla", q_tiles)
    a_d = _diff_attention(qd, kd, vd, diff_lambda[0], lam_init, q_tiles)
    xl, *routed = _odd_output(xa, modtabs[1], o_m, a_d, diff_sub_g[0], odd_w_out[0], lam_init,
                              n_ctx, (norm_g[1, 1], router_w, router_b))
    return _moe_layer(xl, routed, modtabs[1], router_w, router_b, moe_w_gate, moe_w_up,
                      moe_w_down, 1, lat_only)
```

```python
import functools
import math

import jax
import jax.numpy as jnp
from jax import lax
from jax.experimental import pallas as pl
from jax.experimental.pallas import tpu as pltpu

F32 = jnp.float32
BF16 = jnp.bfloat16
I32 = jnp.int32

GRID_W = 64
N_MOD = 6
EPS = 1e-6
CONV_WIDTH = 31
POOL_WINDOWS = (2, 4, 8, 16)
MLA_HEADS = 4
QK_NOPE = 128
QK_ROPE = 64
V_HEAD = 128
DIFF_HEADS = 4
DIFF_QK = 64
DIFF_V = 128
MLA_SCALE = (QK_NOPE + QK_ROPE) ** -0.5
DIFF_SCALE = DIFF_QK ** -0.5
ROPE_BASE = 10000.0
LOG2E = math.log2(math.e)
N_EXPERTS = 32
N_GROUPS = 8
EXPERTS_PER_GROUP = N_EXPERTS // N_GROUPS

LANES = 128
SUBLANES = 8
TS = 256
HALO = 16
TMG = 256
TQ = 256
TK = 256
ATTN_Q_TILES = 4
DISPATCH_BUFS = 3
VMEM_LIMIT = 48 * 1024 * 1024


def _cparams(n_axes):
    return pltpu.CompilerParams(
        dimension_semantics=("arbitrary",) * n_axes, vmem_limit_bytes=VMEM_LIMIT)


def _norm_mod(x, g, shift, scale):
    ms = jnp.mean(x * x, axis=-1, keepdims=True)
    return (x * lax.rsqrt(ms + EPS) * g) * (1.0 + scale) + shift


def _silu(x):
    return x * jax.nn.sigmoid(x)


def _mod_kernel(c_ref, w_ref, b_ref, o_ref):
    a = _silu(c_ref[...])
    o_ref[0] = jnp.dot(a.astype(BF16), w_ref[0].astype(BF16),
                       preferred_element_type=F32) + b_ref[0]


def _modulation(c_rows, mod_w, mod_b):
    depth, d, n = mod_w.shape
    tn = 1536
    rows = c_rows.shape[0]
    return pl.pallas_call(
        _mod_kernel,
        grid=(depth, n // tn),
        in_specs=[
            pl.BlockSpec((rows, d), lambda l, j: (0, 0)),
            pl.BlockSpec((1, d, tn), lambda l, j: (l, 0, j)),
            pl.BlockSpec((1, 1, tn), lambda l, j: (l, 0, j)),
        ],
        out_specs=pl.BlockSpec((1, rows, tn), lambda l, j: (l, 0, j)),
        out_shape=jax.ShapeDtypeStruct((depth, rows, n), F32),
        compiler_params=_cparams(2),
        name="modulation",
    )(c_rows, mod_w, mod_b.reshape(depth, 1, n))


def _even_kernel(xc_ref, xl_ref, xp_ref, xn_ref, mod_ref, g_ref, win_ref, dww_ref, dwb_ref, lng_ref,
                 lnb_ref, pw_ref, ps_ref, wout_ref, g2_ref, rwt_ref, rb_ref, tri_ref,
                 o_ref, h2_ref, grp_ref, rank_ref, cnt_ref,
                 h_s, glu_s, pool_s, pres_s, mix_s, sc_s, carry_s, *, n_lat):
    s = pl.program_id(1)
    nt = pl.num_programs(1)
    d_conv = glu_s.shape[2]
    mod = mod_ref[0, 0]
    shift, scale, gate = mod[0:1], mod[1:2], mod[2:3]
    g = g_ref[...]
    xm = jnp.where(s == 0, xc_ref[0], xl_ref[0])
    h_s[0:HALO, :] = _norm_mod(xp_ref[0], g, shift, scale).astype(BF16)
    h_s[HALO:HALO + TS, :] = _norm_mod(xm, g, shift, scale).astype(BF16)
    h_s[HALO + TS:, :] = _norm_mod(xn_ref[0], g, shift, scale).astype(BF16)

    rows = HALO + TS + HALO
    ridx = lax.broadcasted_iota(I32, (rows, 1), 0)
    prev_ok = s >= 2
    next_ok = jnp.logical_and(s >= 1, s <= nt - 2)
    valid = jnp.logical_and(jnp.logical_or(ridx >= HALO, prev_ok),
                            jnp.logical_or(ridx < HALO + TS, next_ok))

    h = h_s[...]
    a = jnp.dot(h, win_ref[:, 0:d_conv], preferred_element_type=F32)
    gt = jnp.dot(h, win_ref[:, d_conv:2 * d_conv], preferred_element_type=F32)
    glu_s[0] = jnp.where(valid, a * jax.nn.sigmoid(gt), 0.0)
    for j in range(1, SUBLANES):
        glu_s[j, 0:rows - SUBLANES, :] = glu_s[0, pl.ds(j, rows - SUBLANES), :]
    pu = jnp.dot(h, win_ref[:, 2 * d_conv:], preferred_element_type=F32)
    pool_s[...] = jnp.where(valid, pu, 0.0)

    seq_pos0 = jnp.where(s == 0, 0, (s - 1) * TS)
    seq_len = jnp.where(s == 0, TS, n_lat)
    half = CONV_WIDTH // 2
    rc_rows = 64
    n_cc = d_conv // LANES
    for rc in range(TS // rc_rows):
        r0 = HALO + rc * rc_rows
        ys = []
        for cc in range(n_cc):
            cs = slice(cc * LANES, (cc + 1) * LANES)
            acc = jnp.zeros((rc_rows, LANES), F32) + dwb_ref[:, cs]
            for k in range(CONV_WIDTH):
                sh = (r0 + k - half) % SUBLANES
                acc = acc + dww_ref[k:k + 1, cs] * glu_s[sh, pl.ds(r0 + k - half - sh, rc_rows), cs]
            ys.append(acc)
        mu = sum(jnp.sum(y, axis=-1, keepdims=True) for y in ys) / d_conv
        var = sum(jnp.sum((y - mu) * (y - mu), axis=-1, keepdims=True) for y in ys) / d_conv
        rs = lax.rsqrt(var + EPS)
        for cc in range(n_cc):
            cs = slice(cc * LANES, (cc + 1) * LANES)
            z = (ys[cc] - mu) * rs * lng_ref[:, cs] + lnb_ref[:, cs]
            mix_s[rc * rc_rows:(rc + 1) * rc_rows, cs] = _silu(z).astype(BF16)
        pos = seq_pos0 + rc * rc_rows + lax.broadcasted_iota(I32, (rc_rows, 1), 0)
        for gi, w in enumerate(POOL_WINDOWS):
            cs = slice(gi * LANES, (gi + 1) * LANES)
            ssum = pool_s[pl.ds(r0 - w // 2, rc_rows), cs]
            for j in range(1 - w // 2, w // 2):
                ssum = ssum + pool_s[pl.ds(r0 + j, rc_rows), cs]
            lo = jnp.clip(pos - w // 2, 0, seq_len)
            hi = jnp.clip(pos + w // 2, 0, seq_len)
            cnt = (hi - lo).astype(F32)
            res = ssum / cnt - pool_s[pl.ds(r0, rc_rows), cs]
            pres_s[rc * rc_rows:(rc + 1) * rc_rows, cs] = res.astype(BF16)

    yp = jnp.dot(pres_s[...], pw_ref[...], preferred_element_type=F32) * ps_ref[...]
    mix_s[:, d_conv:] = yp.astype(BF16)
    y = jnp.dot(mix_s[...], wout_ref[...], preferred_element_type=F32)
    x_new = xm + gate * y
    o_ref[0] = x_new
    _route_tile(x_new, mod, g2_ref, rwt_ref, rb_ref, tri_ref, h2_ref, grp_ref, rank_ref, cnt_ref,
                sc_s, carry_s)


def _even_layer(ctx, x, modtab, norm_g, w_in, dw_w, dw_b, ln_g, ln_b, pool_w, pool_scale, w_out,
                route):
    b, n_lat, d = x.shape
    assert ctx.shape[1] == TS
    s_tot = TS + n_lat
    nt = s_tot // TS
    r_args, r_in_specs, r_out_specs, r_out_shape, r_scratch = _route_plumbing(b, nt, d, *route)
    d_conv = dw_w.shape[1]
    d_pool = pool_scale.shape[0]
    assert d_pool == len(POOL_WINDOWS) * LANES and pool_w.shape[1] == LANES
    pool_bd = jnp.zeros((d_pool, d_pool), F32)
    for gi in range(len(POOL_WINDOWS)):
        pool_bd = pool_bd.at[gi * LANES:(gi + 1) * LANES, gi * LANES:(gi + 1) * LANES].set(pool_w[gi])
    hb = TS // HALO
    n_hblk = n_lat // HALO
    const2 = lambda bi, si: (0, 0)
    return pl.pallas_call(
        functools.partial(_even_kernel, n_lat=n_lat),
        grid=(b, nt),
        in_specs=[
            pl.BlockSpec((1, TS, d), lambda bi, si: (bi, 0, 0)),
            pl.BlockSpec((1, TS, d), lambda bi, si: (bi, jnp.maximum(si - 1, 0), 0)),
            pl.BlockSpec((1, HALO, d), lambda bi, si: (bi, jnp.maximum((si - 1) * hb - 1, 0), 0)),
            pl.BlockSpec((1, HALO, d), lambda bi, si: (bi, jnp.clip(si * hb, 0, n_hblk - 1), 0)),
            pl.BlockSpec((1, 1, N_MOD, d), lambda bi, si: (bi, jnp.minimum(si, 1), 0, 0)),
            pl.BlockSpec((1, d), const2),
            pl.BlockSpec(w_in.shape, const2),
            pl.BlockSpec(dw_w.shape, const2),
            pl.BlockSpec((1, d_conv), const2),
            pl.BlockSpec((1, d_conv), const2),
            pl.BlockSpec((1, d_conv), const2),
            pl.BlockSpec((d_pool, d_pool), const2),
            pl.BlockSpec((1, d_pool), const2),
            pl.BlockSpec(w_out.shape, const2),
        ] + r_in_specs,
        out_specs=[pl.BlockSpec((1, TS, d), lambda bi, si: (bi, si, 0))] + r_out_specs,
        out_shape=[jax.ShapeDtypeStruct((b, s_tot, d), F32)] + r_out_shape,
        scratch_shapes=[
            pltpu.VMEM((TS + 2 * HALO, d), BF16),
            pltpu.VMEM((SUBLANES, TS + 2 * HALO, d_conv), F32),
            pltpu.VMEM((TS + 2 * HALO, d_pool), F32),
            pltpu.VMEM((TS, d_pool), BF16),
            pltpu.VMEM((TS, d_conv + d_pool), BF16),
        ] + r_scratch,
        compiler_params=_cparams(2),
        name="even_mixer",
    )(ctx, x, x, x, modtab, norm_g.reshape(1, d), w_in.astype(BF16), dw_w, dw_b.reshape(1, -1),
      ln_g.reshape(1, -1), ln_b.reshape(1, -1), pool_bd.astype(BF16), pool_scale.reshape(1, -1),
      w_out.astype(BF16), *r_args)


def _route_tile(x, mod, g_ref, rwt_ref, rb_ref, tri_ref, h_ref, grp_ref, rank_ref, cnt_ref, sc_s,
                carry_s, tile=0):
    if tile == 0:
        first = jnp.logical_and(pl.program_id(0) == 0, pl.program_id(1) == 0)

        @pl.when(first)
        def _():
            carry_s[...] = jnp.zeros_like(carry_s)

    h = _norm_mod(x, g_ref[...], mod[3:4], mod[4:5])
    h_ref[0, tile * TS:(tile + 1) * TS, :] = h
    logits = lax.dot_general(rwt_ref[...], h.astype(BF16), (((1,), (1,)), ((), ())),
                             preferred_element_type=F32)
    biased = jax.nn.sigmoid(logits) + rb_ref[...]
    n_half = TS // LANES
    for hh in range(n_half):
        sc_s[hh] = biased[:, hh * LANES:(hh + 1) * LANES]
    a, b, c, d = [
        jnp.concatenate([sc_s[hh, pl.ds(j, N_GROUPS, stride=EXPERTS_PER_GROUP), :]
                         for hh in range(n_half)], axis=-1)
        for j in range(EXPERTS_PER_GROUP)]
    top2 = jnp.maximum(jnp.maximum(jnp.maximum(a + b, a + c), jnp.maximum(a + d, b + c)),
                       jnp.maximum(b + d, c + d))
    gmax = jnp.max(top2, axis=0, keepdims=True)
    gi = lax.broadcasted_iota(I32, top2.shape, 0)
    sel = jnp.min(jnp.where(top2 == gmax, gi, N_GROUPS), axis=0, keepdims=True)
    onehot = gi == sel
    ohf = jnp.where(onehot, 1.0, 0.0)
    prefix = jnp.dot(ohf.astype(BF16), tri_ref[...], preferred_element_type=F32)
    carry = carry_s[:, 0:1]
    rank = jnp.sum(jnp.where(onehot, carry + prefix - 1.0, 0.0), axis=0, keepdims=True)
    carry_s[...] = carry_s[...] + jnp.sum(ohf, axis=1, keepdims=True)
    grp_ref[tile] = sel
    rank_ref[tile] = rank.astype(I32)
    cnt_ref[...] = carry_s[...]


def _route_plumbing(b, nt, d, norm_g, router_w, router_b, tiles=1):
    const2 = lambda bi, si: (0, 0)
    flat3 = lambda bi, si: (bi * (nt // tiles) + si, 0, 0)
    tri = (jnp.arange(TS)[:, None] <= jnp.arange(TS)[None, :]).astype(BF16)
    args = [norm_g.reshape(1, d), router_w.T.astype(BF16), router_b.reshape(N_EXPERTS, 1), tri]
    in_specs = [pl.BlockSpec((1, d), const2), pl.BlockSpec((N_EXPERTS, d), const2),
                pl.BlockSpec((N_EXPERTS, 1), const2), pl.BlockSpec((TS, TS), const2)]
    out_specs = [pl.BlockSpec((1, tiles * TS, d), lambda bi, si: (bi, si, 0)),
                 pl.BlockSpec((tiles, 1, TS), flat3), pl.BlockSpec((tiles, 1, TS), flat3),
                 pl.BlockSpec((N_GROUPS, LANES), const2)]
    out_shape = [jax.ShapeDtypeStruct((b, nt * TS, d), F32),
                 jax.ShapeDtypeStruct((b * nt, 1, TS), I32),
                 jax.ShapeDtypeStruct((b * nt, 1, TS), I32),
                 jax.ShapeDtypeStruct((N_GROUPS, LANES), F32)]
    scratch = [pltpu.VMEM((TS // LANES, N_EXPERTS, LANES), F32), pltpu.VMEM((N_GROUPS, LANES), F32)]
    return args, in_specs, out_specs, out_shape, scratch


def _dispatch_kernel(zrow_ref, pos_ref, src_ref, dst_ref, buf, ld_sem, st_sem, z_sem):
    i = pl.program_id(0)
    n = pl.num_programs(0)
    slot = lax.rem(i, DISPATCH_BUFS)
    nxt = lax.rem(i + 1, DISPATCH_BUFS)
    tile_groups = TS // SUBLANES

    def load(tile, which):
        return pltpu.make_async_copy(src_ref.at[pl.ds(tile * tile_groups, tile_groups)],
                                     buf.at[which], ld_sem.at[which])

    def wait_scatter(which):
        pltpu.make_async_copy(buf.at[which], dst_ref.at[pl.ds(0, tile_groups)],
                              st_sem.at[which]).wait()

    @pl.when(i == 0)
    def _():
        load(i, slot).start()
        zsrc = buf.at[DISPATCH_BUFS - 1]
        zsrc[...] = jnp.zeros(zsrc.shape, F32)
        for z in range(zrow_ref.shape[0]):
            @pl.when(zrow_ref[z] >= 0)
            def _():
                pltpu.make_async_copy(zsrc, dst_ref.at[pl.ds(zrow_ref[z], tile_groups)],
                                      z_sem).start()
        for z in range(zrow_ref.shape[0]):
            @pl.when(zrow_ref[z] >= 0)
            def _():
                pltpu.make_async_copy(zsrc, dst_ref.at[pl.ds(0, tile_groups)], z_sem).wait()

    load(i, slot).wait()

    @pl.when(i + 1 >= DISPATCH_BUFS)
    def _():
        wait_scatter(nxt)

    @pl.when(i + 1 < n)
    def _():
        load(i + 1, nxt).start()

    def issue(r8, carry):
        for j in range(SUBLANES):
            p = pos_ref[0, 0, r8 * SUBLANES + j]
            dst_row = dst_ref.at[lax.shift_right_logical(p, 3), pl.ds(p & (SUBLANES - 1), 1)]
            pltpu.make_async_copy(buf.at[slot, r8, pl.ds(j, 1)], dst_row, st_sem.at[slot]).start()
        return carry

    lax.fori_loop(0, tile_groups, issue, 0)

    @pl.when(i == n - 1)
    def _():
        for back in range(DISPATCH_BUFS - 1):
            @pl.when(i - back >= 0)
            def _():
                wait_scatter(lax.rem(i - back + DISPATCH_BUFS, DISPATCH_BUFS))


def _moe_dispatch(h_flat, pos, zero_rows, n_rows):
    t, d = h_flat.shape
    n_tiles = t // TS
    assert TS == TMG and n_tiles >= DISPATCH_BUFS
    zero_groups = jnp.where(zero_rows >= 0, zero_rows // SUBLANES, -1).astype(I32)
    return pl.pallas_call(
        _dispatch_kernel,
        grid=(n_tiles,),
        in_specs=[
            pl.BlockSpec(memory_space=pltpu.SMEM),
            pl.BlockSpec((1, 1, TS), lambda i: (i, 0, 0), memory_space=pltpu.SMEM),
            pl.BlockSpec(memory_space=pl.ANY),
        ],
        out_specs=pl.BlockSpec(memory_space=pl.ANY),
        out_shape=jax.ShapeDtypeStruct((n_rows // SUBLANES, SUBLANES, d), F32),
        scratch_shapes=[pltpu.VMEM((DISPATCH_BUFS, TS // SUBLANES, SUBLANES, d), F32),
                        pltpu.SemaphoreType.DMA((DISPATCH_BUFS,)),
                        pltpu.SemaphoreType.DMA((DISPATCH_BUFS,)),
                        pltpu.SemaphoreType.DMA(())],
        compiler_params=_cparams(1),
        name="moe_dispatch",
    )(zero_groups, pos, h_flat.reshape(t // SUBLANES, SUBLANES, d))


def _ffn_kernel(tg_ref, tv_ref, xs_ref, rw_ref, rb_ref, wg32_ref, wu32_ref, wd32_ref, o_ref,
                wg_ref, wu_ref, wd_ref):
    i = pl.program_id(0)

    @pl.when(jnp.logical_or(i == 0, tg_ref[i] != tg_ref[jnp.maximum(i - 1, 0)]))
    def _():
        wg_ref[...] = wg32_ref[...].astype(BF16)
        wu_ref[...] = wu32_ref[...].astype(BF16)
        wd_ref[...] = wd32_ref[...].astype(BF16)

    @pl.when(tv_ref[i] == 0)
    def _():
        o_ref[...] = jnp.zeros_like(o_ref)

    @pl.when(tv_ref[i] != 0)
    def _():
        xb = xs_ref[...].astype(BF16)
        logits = jnp.dot(xb, rw_ref[0], preferred_element_type=F32)
        sc = jax.nn.sigmoid(logits)
        bs = sc + rb_ref[0]
        s_col = [sc[:, j:j + 1] for j in range(EXPERTS_PER_GROUP)]
        b_col = [bs[:, j:j + 1] for j in range(EXPERTS_PER_GROUP)]
        sel = []
        for j in range(EXPERTS_PER_GROUP):
            beaten = jnp.zeros_like(b_col[j])
            for k in range(EXPERTS_PER_GROUP):
                if k == j:
                    continue
                wins = (b_col[k] >= b_col[j]) if k < j else (b_col[k] > b_col[j])
                beaten = beaten + jnp.where(wins, 1.0, 0.0)
            sel.append(beaten < 2.0)
        den = sum(jnp.where(sel[j], s_col[j], 0.0) for j in range(EXPERTS_PER_GROUP))
        hes = []
        for j in range(EXPERTS_PER_GROUP):
            cj = jnp.where(sel[j], s_col[j] / den, 0.0)
            gj = jnp.dot(xb, wg_ref[0, j], preferred_element_type=F32)
            uj = jnp.dot(xb, wu_ref[0, j], preferred_element_type=F32)
            hes.append((_silu(gj) * uj * cj).astype(BF16))
        he = jnp.concatenate(hes, axis=-1)
        d_e = wd_ref.shape[2]
        wd = wd_ref[0].reshape(EXPERTS_PER_GROUP * d_e, wd_ref.shape[3])
        o_ref[...] = jnp.dot(he, wd, preferred_element_type=F32)


def _moe_ffn(xs, tile_group, tile_valid, rw_g, rb_g, w_gate, w_up, w_down, layer):
    n_rows, d = xs.shape
    d_e = w_gate.shape[-1]
    n_tiles = n_rows // TMG
    epg = EXPERTS_PER_GROUP
    grid_spec = pltpu.PrefetchScalarGridSpec(
        num_scalar_prefetch=2,
        grid=(n_tiles,),
        in_specs=[
            pl.BlockSpec((TMG, d), lambda i, tg, tv: (i, 0)),
            pl.BlockSpec((1, d, LANES), lambda i, tg, tv: (tg[i], 0, 0)),
            pl.BlockSpec((1, 1, LANES), lambda i, tg, tv: (tg[i], 0, 0)),
            pl.BlockSpec((1, epg, d, d_e), lambda i, tg, tv: (layer, tg[i], 0, 0)),
            pl.BlockSpec((1, epg, d, d_e), lambda i, tg, tv: (layer, tg[i], 0, 0)),
            pl.BlockSpec((1, epg, d_e, d), lambda i, tg, tv: (layer, tg[i], 0, 0)),
        ],
        out_specs=pl.BlockSpec((TMG, d), lambda i, tg, tv: (i, 0)),
        scratch_shapes=[pltpu.VMEM((1, epg, d, d_e), BF16), pltpu.VMEM((1, epg, d, d_e), BF16),
                        pltpu.VMEM((1, epg, d_e, d), BF16)],
    )
    return pl.pallas_call(
        _ffn_kernel,
        grid_spec=grid_spec,
        out_shape=jax.ShapeDtypeStruct((n_rows, d), F32),
        compiler_params=_cparams(1),
        name="moe_ffn",
    )(tile_group, tile_valid, xs, rw_g, rb_g, w_gate, w_up, w_down)


def _combine_kernel(pos_ref, posn_ref, x_ref, mod_ref, ys_ref, o_ref, buf, sem):
    i = pl.program_id(0) * pl.num_programs(1) + pl.program_id(1)
    n = pl.num_programs(0) * pl.num_programs(1)
    slot = lax.rem(i, 2)

    tile_groups = TS // SUBLANES

    def gather(p_ref, which):
        def issue(r8, carry):
            for j in range(SUBLANES):
                p = p_ref[0, 0, r8 * SUBLANES + j]
                src_row = ys_ref.at[lax.shift_right_logical(p, 3), pl.ds(p & (SUBLANES - 1), 1)]
                pltpu.make_async_copy(src_row, buf.at[which, r8, pl.ds(j, 1)], sem.at[which]).start()
            return carry
        lax.fori_loop(0, tile_groups, issue, 0)

    @pl.when(i == 0)
    def _():
        gather(pos_ref, slot)

    @pl.when(i + 1 < n)
    def _():
        gather(posn_ref, 1 - slot)

    pltpu.make_async_copy(ys_ref.at[pl.ds(0, tile_groups)], buf.at[slot], sem.at[slot]).wait()
    gate = mod_ref[0, 0][5:6]
    o_ref[0] = x_ref[0] + gate * buf[slot].reshape(TS, buf.shape[-1])


def _moe_combine(x, modtab, pos, ys, kind_of_tile):
    b, s_tot, d = x.shape
    nt = s_tot // TS
    n_tiles = b * nt
    cur = lambda bi, si: (bi * nt + si, 0, 0)
    nxt = lambda bi, si: (jnp.minimum(bi * nt + si + 1, n_tiles - 1), 0, 0)
    return pl.pallas_call(
        _combine_kernel,
        grid=(b, nt),
        in_specs=[
            pl.BlockSpec((1, 1, TS), cur, memory_space=pltpu.SMEM),
            pl.BlockSpec((1, 1, TS), nxt, memory_space=pltpu.SMEM),
            pl.BlockSpec((1, TS, d), lambda bi, si: (bi, si, 0)),
            pl.BlockSpec((1, 1, N_MOD, d), lambda bi, si: (bi, kind_of_tile(si), 0, 0)),
            pl.BlockSpec(memory_space=pl.ANY),
        ],
        out_specs=pl.BlockSpec((1, TS, d), lambda bi, si: (bi, si, 0)),
        out_shape=jax.ShapeDtypeStruct((b, s_tot, d), F32),
        scratch_shapes=[pltpu.VMEM((2, TS // SUBLANES, SUBLANES, d), F32),
                        pltpu.SemaphoreType.DMA((2,))],
        compiler_params=_cparams(2),
        name="moe_combine",
    )(pos, pos, x, modtab, ys.reshape(ys.shape[0] // SUBLANES, SUBLANES, d))


def _moe_layer(x, routed, modtab, router_w, router_b, w_gate, w_up, w_down, layer, kind_of_tile):
    b, s_tot, d = x.shape
    t = b * s_tot
    h, grp, rank, cnt = routed
    counts = cnt[:, 0].astype(I32)
    padded = ((counts + TMG - 1) // TMG) * TMG
    ends = jnp.cumsum(padded)
    starts = ends - padded
    pos = starts[grp] + rank
    n_rows = t + N_GROUPS * TMG
    tile_row0 = jnp.arange(n_rows // TMG, dtype=I32) * TMG
    tile_group = jnp.minimum(jnp.sum(tile_row0[:, None] >= ends[None, :], axis=1),
                             N_GROUPS - 1).astype(I32)
    tile_valid = (tile_row0 < ends[-1]).astype(I32)
    tail_rows = jnp.minimum(ends[-1] + jnp.arange(N_GROUPS, dtype=I32) * TMG, n_rows - TMG)
    zero_rows = jnp.concatenate([jnp.where(padded > 0, ends - TMG, tail_rows[-1]), tail_rows])
    n_z = zero_rows.shape[0]
    repeat = jnp.any((zero_rows[:, None] == zero_rows[None, :])
                     & (jnp.arange(n_z)[None, :] < jnp.arange(n_z)[:, None]), axis=1)
    zero_rows = jnp.where(repeat, -1, zero_rows)
    xs = _moe_dispatch(h.reshape(t, d), pos, zero_rows.astype(I32), n_rows)
    epg = EXPERTS_PER_GROUP
    rw_g = jnp.pad(router_w.reshape(d, N_GROUPS, epg).transpose(1, 0, 2),
                   ((0, 0), (0, 0), (0, LANES - epg))).astype(BF16)
    rb_g = jnp.pad(router_b.reshape(N_GROUPS, 1, epg), ((0, 0), (0, 0), (0, LANES - epg)))
    ys = _moe_ffn(xs.reshape(n_rows, d), tile_group, tile_valid, rw_g, rb_g, w_gate, w_up, w_down,
                  layer)
    return _moe_combine(x, modtab, pos, ys, kind_of_tile)


def _rope_pad(w, axis):
    x1, x2 = jnp.split(w, 2, axis=axis)
    z = jnp.zeros_like(x1)
    return jnp.concatenate([x1, z, x2, z], axis=axis)


def _oddproj_kernel(x_ref, mod_ref, g_ref, win_ref, gcq_ref, wuq_ref, gckv_ref, wukv_ref, gqm_ref,
                    gkm_ref, gqd_ref, gkd_ref, cs_ref, sna_ref, snb_ref, csm_ref, snm_ref,
                    qm_ref, km_ref, vm_ref, qd_ref, kd_ref, vd_ref, p_s, qm_s, kv_s):
    mod = mod_ref[0, 0]
    h = _norm_mod(x_ref[0], g_ref[...], mod[0:1], mod[1:2]).astype(BF16)
    q_lora = gcq_ref.shape[1]
    kv_lora = gckv_ref.shape[1]
    n_qd = DIFF_HEADS * 2 * DIFF_QK
    o_qd = q_lora
    o_ckv = o_qd + n_qd
    o_kr = o_ckv + kv_lora
    o_kd = o_kr + LANES
    o_vd = o_kd + n_qd
    lane = lax.broadcasted_iota(I32, (1, LANES), 1)
    low = lane < DIFF_QK
    d_mla = QK_NOPE + QK_ROPE

    def rms(x, g, n):
        return x * lax.rsqrt(jnp.sum(x * x, axis=-1, keepdims=True) / n + EPS) * g

    def rope_mla(y):
        return y * csm_ref[...] + pltpu.roll(y, LANES // 2, 1) * snm_ref[...]

    def diff_cols(p, g):
        sq = p * p
        s_all = jnp.sum(sq, axis=-1, keepdims=True)
        s_lo = jnp.sum(jnp.where(low, sq, 0.0), axis=-1, keepdims=True)
        r = jnp.where(low, lax.rsqrt(s_lo / DIFF_QK + EPS),
                      lax.rsqrt((s_all - s_lo) / DIFF_QK + EPS))
        y = p * r * g
        return (y * cs_ref[...] + pltpu.roll(y, LANES - DIFF_QK // 2, 1) * sna_ref[...]
                + pltpu.roll(y, DIFF_QK // 2, 1) * snb_ref[...])

    p_s[...] = jnp.dot(h, win_ref[...], preferred_element_type=F32)
    cq = rms(p_s[:, 0:q_lora], gcq_ref[...], q_lora)
    qm_s[...] = jnp.dot(cq.astype(BF16), wuq_ref[...], preferred_element_type=F32)
    ckv = rms(p_s[:, o_ckv:o_ckv + kv_lora], gckv_ref[...], kv_lora)
    kv_s[...] = jnp.dot(ckv.astype(BF16), wukv_ref[...], preferred_element_type=F32)

    for hd in range(DIFF_HEADS):
        y = diff_cols(p_s[:, o_qd + LANES * hd:o_qd + LANES * (hd + 1)], gqd_ref[...])
        y = y * (DIFF_SCALE * LOG2E)
        qd_ref[0, 2 * hd] = jnp.where(low, y, 0.0).astype(BF16)
        qd_ref[0, 2 * hd + 1] = jnp.where(low, 0.0, y).astype(BF16)
        kd_ref[0, hd] = diff_cols(p_s[:, o_kd + LANES * hd:o_kd + LANES * (hd + 1)],
                                  gkd_ref[...]).astype(BF16)
        vd_ref[0, hd] = p_s[:, o_vd + DIFF_V * hd:o_vd + DIFF_V * (hd + 1)].T.astype(BF16)

    kr = p_s[:, o_kr:o_kr + LANES]
    for hd in range(MLA_HEADS):
        qh = rms(qm_s[:, 2 * LANES * hd:2 * LANES * (hd + 1)], gqm_ref[...], d_mla)
        qh = jnp.concatenate([qh[:, :LANES], rope_mla(qh[:, LANES:])], axis=-1)
        qm_ref[0, hd] = (qh * (MLA_SCALE * LOG2E)).astype(BF16)
        kcat = jnp.concatenate([kv_s[:, 2 * LANES * hd:2 * LANES * hd + LANES], kr], axis=-1)
        kh = rms(kcat, gkm_ref[...], d_mla)
        km_ref[0, hd] = jnp.concatenate([kh[:, :LANES], rope_mla(kh[:, LANES:])], axis=-1).astype(BF16)
        vm_ref[0, hd] = kv_s[:, 2 * LANES * hd + LANES:2 * LANES * (hd + 1)].T.astype(BF16)


def _odd_project(xa, modtab, norm_g, w_in, g_cq, w_uq, g_ckv, w_ukv, g_mla, g_diff, n_ctx):
    b, s_tot, d = xa.shape
    nt = s_tot // TS
    n_lat = s_tot - n_ctx
    q_lora = g_cq.shape[0]
    kv_lora = g_ckv.shape[0]
    n_qd = DIFF_HEADS * 2 * DIFF_QK
    d_mla = QK_NOPE + QK_ROPE
    o = 0
    w_cq = w_in[:, o:o + q_lora]; o += q_lora
    w_qd = w_in[:, o:o + n_qd]; o += n_qd
    w_ckv = w_in[:, o:o + kv_lora]; o += kv_lora
    w_kr = w_in[:, o:o + QK_ROPE]; o += QK_ROPE
    w_kd = w_in[:, o:o + n_qd]; o += n_qd
    w_vd = w_in[:, o:]
    w_in_p = jnp.concatenate([w_cq, w_qd, w_ckv, _rope_pad(w_kr, 1), w_kd, w_vd], axis=1).astype(BF16)
    wuq = w_uq.reshape(q_lora, MLA_HEADS, d_mla)
    wuq_p = jnp.concatenate([wuq[..., :QK_NOPE], _rope_pad(wuq[..., QK_NOPE:], 2)],
                            axis=-1).reshape(q_lora, MLA_HEADS * 2 * LANES).astype(BF16)

    def pad_gain(g):
        return jnp.concatenate([g[:QK_NOPE], _rope_pad(g[QK_NOPE:], 0)]).reshape(1, 2 * LANES)

    rows = n_lat // GRID_W
    row = jnp.repeat(jnp.arange(rows, dtype=F32), GRID_W)
    col = jnp.tile(jnp.arange(GRID_W, dtype=F32), rows)
    axis_dim = QK_ROPE // 2
    inv_freq = ROPE_BASE ** (-jnp.arange(0, axis_dim, 2, dtype=F32) / axis_dim)
    ang = jnp.concatenate([row[:, None] * inv_freq, col[:, None] * inv_freq], axis=-1)
    half = QK_ROPE // 2
    cos = jnp.concatenate([jnp.ones((n_ctx, half), F32), jnp.cos(ang)], axis=0)
    sin = jnp.concatenate([jnp.zeros((n_ctx, half), F32), jnp.sin(ang)], axis=0)
    z = jnp.zeros_like(sin)
    cs_d = jnp.tile(cos, (1, LANES // half))
    sna_d = jnp.tile(jnp.concatenate([-sin, z], axis=1), (1, LANES // QK_ROPE))
    snb_d = jnp.tile(jnp.concatenate([z, sin], axis=1), (1, LANES // QK_ROPE))
    cs_m = jnp.concatenate([cos, z, cos, z], axis=1)
    sn_m = jnp.concatenate([-sin, z, sin, z], axis=1)

    const2 = lambda bi, si: (0, 0)
    tok = lambda bi, si: (si, 0)
    kv4 = lambda bi, si: (bi, 0, si, 0)
    vt4 = lambda bi, si: (bi, 0, 0, si)
    q4 = lambda bi, si: (bi, 0, jnp.maximum(si - n_ctx // TS, 0), 0)
    full = lambda a: pl.BlockSpec(a.shape, const2)
    gq_d = jnp.tile(g_diff[0], LANES // DIFF_QK).reshape(1, LANES)
    gk_d = jnp.tile(g_diff[1], LANES // DIFF_QK).reshape(1, LANES)
    weights = [norm_g.reshape(1, d), w_in_p, g_cq.reshape(1, -1), wuq_p, g_ckv.reshape(1, -1),
               w_ukv.astype(BF16), pad_gain(g_mla[0]), pad_gain(g_mla[1]), gq_d, gk_d]
    args = [xa, modtab, *weights, cs_d, sna_d, snb_d, cs_m, sn_m]
    in_specs = [
        pl.BlockSpec((1, TS, d), lambda bi, si: (bi, si, 0)),
        pl.BlockSpec((1, 1, N_MOD, d), lambda bi, si: (bi, jnp.minimum(si, 1), 0, 0)),
    ] + [full(a) for a in weights] + [pl.BlockSpec((TS, LANES), tok)] * 5
    hm, hd = MLA_HEADS, DIFF_HEADS
    return pl.pallas_call(
        _oddproj_kernel,
        grid=(b, nt),
        in_specs=in_specs,
        out_specs=[
            pl.BlockSpec((1, hm, TS, 2 * LANES), q4),
            pl.BlockSpec((1, hm, TS, 2 * LANES), kv4),
            pl.BlockSpec((1, hm, V_HEAD, TS), vt4),
            pl.BlockSpec((1, 2 * hd, TS, LANES), q4),
            pl.BlockSpec((1, hd, TS, LANES), kv4),
            pl.BlockSpec((1, hd, DIFF_V, TS), vt4),
        ],
        out_shape=[
            jax.ShapeDtypeStruct((b, hm, n_lat, 2 * LANES), BF16),
            jax.ShapeDtypeStruct((b, hm, s_tot, 2 * LANES), BF16),
            jax.ShapeDtypeStruct((b, hm, V_HEAD, s_tot), BF16),
            jax.ShapeDtypeStruct((b, 2 * hd, n_lat, LANES), BF16),
            jax.ShapeDtypeStruct((b, hd, s_tot, LANES), BF16),
            jax.ShapeDtypeStruct((b, hd, DIFF_V, s_tot), BF16),
        ],
        scratch_shapes=[pltpu.VMEM((TS, w_in_p.shape[1]), F32),
                        pltpu.VMEM((TS, wuq_p.shape[1]), F32),
                        pltpu.VMEM((TS, w_ukv.shape[1]), F32)],
        compiler_params=_cparams(2),
        name="odd_project",
    )(*args)


def _attn_kernel(q_ref, k_ref, vt_ref, o_ref, s_s, p_s, *, rep):
    hq = q_ref.shape[1]
    n_items = hq * (q_ref.shape[2] // TQ)
    n_chunks = k_ref.shape[2] // TK
    groups = TK // SUBLANES

    def q_rows(i):
        return pl.ds((i // hq) * TQ, TQ)

    def scores(i):
        u, h = i % 2, i % hq
        s_s[u] = lax.dot_general(k_ref[0, h // rep], q_ref[0, h, q_rows(i), :],
                                 (((1,), (1,)), ((), ())), preferred_element_type=F32)
        m8 = jnp.full((SUBLANES, TQ), -jnp.inf, F32)
        for c in range(n_chunks):
            st = s_s[u, pl.ds(c * TK, TK), :]
            m8 = jnp.maximum(m8, jnp.max(st.reshape(groups, SUBLANES, TQ), axis=0))
        return jnp.max(m8, axis=0, keepdims=True)

    def exponentials(i, row_max):
        u = i % 2
        l8 = jnp.zeros((SUBLANES, TQ), F32)
        for c in range(n_chunks):
            p = jnp.exp2(s_s[u, pl.ds(c * TK, TK), :] - row_max)
            l8 = l8 + jnp.sum(p.reshape(groups, SUBLANES, TQ), axis=0)
            p_s[u, pl.ds(c * TK, TK), :] = p.astype(BF16)
        return jnp.sum(l8, axis=0, keepdims=True)

    def values(i, l):
        h = i % hq
        out_t = jnp.dot(vt_ref[0, h // rep], p_s[i % 2], preferred_element_type=F32)
        o_ref[0, h, q_rows(i), :] = (out_t / l).T.astype(o_ref.dtype)

    row_max, row_sum = {}, {}
    for t in range(n_items + 2):
        if t < n_items:
            row_max[t] = scores(t)
        if 0 <= t - 1 < n_items:
            row_sum[t - 1] = exponentials(t - 1, row_max.pop(t - 1))
        if 0 <= t - 2 < n_items:
            values(t - 2, row_sum.pop(t - 2))


def _diff_attn_kernel(q_ref, k_ref, vt_ref, dl_ref, o_ref, s_s, p_s, a1_s, *, lam_init):
    hq = q_ref.shape[1]
    n_items = hq * (q_ref.shape[2] // TQ)
    n_chunks = k_ref.shape[2] // TK
    groups = TK // SUBLANES
    dl = dl_ref[...]
    lam = (jnp.exp(jnp.sum(dl[0:1] * dl[1:2], axis=-1, keepdims=True))
           - jnp.exp(jnp.sum(dl[2:3] * dl[3:4], axis=-1, keepdims=True)) + lam_init)

    def q_rows(i):
        return pl.ds((i // hq) * TQ, TQ)

    def scores(i):
        u, slot = i % 2, i % hq
        s_s[u] = lax.dot_general(k_ref[0, slot // 2], q_ref[0, slot, q_rows(i), :],
                                 (((1,), (1,)), ((), ())), preferred_element_type=F32)
        m8 = jnp.full((SUBLANES, TQ), -jnp.inf, F32)
        for c in range(n_chunks):
            st = s_s[u, pl.ds(c * TK, TK), :]
            m8 = jnp.maximum(m8, jnp.max(st.reshape(groups, SUBLANES, TQ), axis=0))
        return jnp.max(m8, axis=0, keepdims=True)

    def exponentials(i, row_max):
        u = i % 2
        l8 = jnp.zeros((SUBLANES, TQ), F32)
        for c in range(n_chunks):
            p = jnp.exp2(s_s[u, pl.ds(c * TK, TK), :] - row_max)
            l8 = l8 + jnp.sum(p.reshape(groups, SUBLANES, TQ), axis=0)
            p_s[u, pl.ds(c * TK, TK), :] = p.astype(BF16)
        return jnp.sum(l8, axis=0, keepdims=True)

    def values(i, l):
        slot = i % hq
        a = jnp.dot(vt_ref[0, slot // 2], p_s[i % 2], preferred_element_type=F32) / l
        if slot % 2 == 0:
            a1_s[...] = a
        else:
            o_ref[0, slot // 2, q_rows(i), :] = (a1_s[...] - lam * a).T

    row_max, row_sum = {}, {}
    for t in range(n_items + 2):
        if t < n_items:
            row_max[t] = scores(t)
        if 0 <= t - 1 < n_items:
            row_sum[t - 1] = exponentials(t - 1, row_max.pop(t - 1))
        if 0 <= t - 2 < n_items:
            values(t - 2, row_sum.pop(t - 2))


def _diff_attention(q, k, vt, diff_lambda, lam_init, q_tiles):
    b, hq, lq, dk = q.shape
    hk, lk = k.shape[1], k.shape[2]
    dv = vt.shape[2]
    tq_step = q_tiles * TQ
    assert lk % TK == 0 and lq % tq_step == 0 and hq == 2 * hk
    once = pl.Buffered(1)
    return pl.pallas_call(
        functools.partial(_diff_attn_kernel, lam_init=lam_init),
        grid=(b, lq // tq_step),
        in_specs=[
            pl.BlockSpec((1, hq, tq_step, dk), lambda bi, qi: (bi, 0, qi, 0)),
            pl.BlockSpec((1, hk, lk, dk), lambda bi, qi: (bi, 0, 0, 0), pipeline_mode=once),
            pl.BlockSpec((1, hk, dv, lk), lambda bi, qi: (bi, 0, 0, 0), pipeline_mode=once),
            pl.BlockSpec(diff_lambda.shape, lambda bi, qi: (0, 0)),
        ],
        out_specs=pl.BlockSpec((1, hk, tq_step, dv), lambda bi, qi: (bi, 0, qi, 0)),
        out_shape=jax.ShapeDtypeStruct((b, hk, lq, dv), F32),
        scratch_shapes=[pltpu.VMEM((2, lk, TQ), F32), pltpu.VMEM((2, lk, TQ), BF16),
                        pltpu.VMEM((dv, TQ), F32)],
        compiler_params=_cparams(2),
        name="attn_diff",
    )(q, k, vt, diff_lambda)


def _attention(q, k, vt, out_dtype, name, q_tiles):
    b, hq, lq, dk = q.shape
    hk, lk = k.shape[1], k.shape[2]
    dv = vt.shape[2]
    tq_step = q_tiles * TQ
    assert lk % TK == 0 and lq % tq_step == 0
    once = pl.Buffered(1)
    return pl.pallas_call(
        functools.partial(_attn_kernel, rep=hq // hk),
        grid=(b, lq // tq_step),
        in_specs=[
            pl.BlockSpec((1, hq, tq_step, dk), lambda bi, qi: (bi, 0, qi, 0)),
            pl.BlockSpec((1, hk, lk, dk), lambda bi, qi: (bi, 0, 0, 0), pipeline_mode=once),
            pl.BlockSpec((1, hk, dv, lk), lambda bi, qi: (bi, 0, 0, 0), pipeline_mode=once),
        ],
        out_specs=pl.BlockSpec((1, hq, tq_step, dv), lambda bi, qi: (bi, 0, qi, 0)),
        out_shape=jax.ShapeDtypeStruct((b, hq, lq, dv), out_dtype),
        scratch_shapes=[pltpu.VMEM((2, lk, TQ), F32), pltpu.VMEM((2, lk, TQ), BF16)],
        compiler_params=_cparams(2),
        name=name,
    )(q, k, vt)


def _oddout_kernel(xa_ref, xb_ref, mod_ref, om_ref, ad_ref, gs_ref, wout_ref, g2_ref, rwt_ref,
                   rb_ref, tri_ref, o_ref, h2_ref, grp_ref, rank_ref, cnt_ref, sc_s, carry_s, *,
                   lam_init):
    mod = mod_ref[0, 0]
    x_tiles = (xa_ref, xb_ref)
    mixes = []
    for t in range(2):
        rows = slice(t * TS, (t + 1) * TS)
        parts = [om_ref[0, hd, rows, :] for hd in range(MLA_HEADS)]
        for hd in range(DIFF_HEADS):
            df = ad_ref[0, hd, rows, :]
            od = df * lax.rsqrt(jnp.mean(df * df, axis=-1, keepdims=True) + EPS) * gs_ref[...]
            parts.append((od * (1.0 - lam_init)).astype(BF16))
        mixes.append(jnp.concatenate(parts, axis=-1))
    new_tiles = []
    for t in range(2):
        y = jnp.dot(mixes[t], wout_ref[...], preferred_element_type=F32)
        x_new = x_tiles[t][0] + mod[2:3] * y
        o_ref[0, t * TS:(t + 1) * TS, :] = x_new
        new_tiles.append(x_new)
    for t in range(2):
        _route_tile(new_tiles[t], mod, g2_ref, rwt_ref, rb_ref, tri_ref, h2_ref, grp_ref, rank_ref,
                    cnt_ref, sc_s, carry_s, tile=t)


def _odd_output(xa, modtab, o_m, a_d, g_sub, w_out, lam_init, n_ctx, route):
    b, s_tot, d = xa.shape
    n_lat = s_tot - n_ctx
    off = n_ctx // TS
    const2 = lambda bi, si: (0, 0)
    assert (n_lat // TS) % 2 == 0
    r_args, r_in_specs, r_out_specs, r_out_shape, r_scratch = _route_plumbing(
        b, n_lat // TS, d, *route, tiles=2)
    return pl.pallas_call(
        functools.partial(_oddout_kernel, lam_init=lam_init),
        grid=(b, n_lat // (2 * TS)),
        in_specs=[
            pl.BlockSpec((1, TS, d), lambda bi, si: (bi, 2 * si + off, 0)),
            pl.BlockSpec((1, TS, d), lambda bi, si: (bi, 2 * si + 1 + off, 0)),
            pl.BlockSpec((1, 1, N_MOD, d), lambda bi, si: (bi, 1, 0, 0)),
            pl.BlockSpec((1, o_m.shape[1], 2 * TS, V_HEAD), lambda bi, si: (bi, 0, si, 0)),
            pl.BlockSpec((1, a_d.shape[1], 2 * TS, DIFF_V), lambda bi, si: (bi, 0, si, 0)),
            pl.BlockSpec((1, DIFF_V), const2),
            pl.BlockSpec(w_out.shape, const2),
        ] + r_in_specs,
        out_specs=[pl.BlockSpec((1, 2 * TS, d), lambda bi, si: (bi, si, 0))] + r_out_specs,
        out_shape=[jax.ShapeDtypeStruct((b, n_lat, d), F32)] + r_out_shape,
        scratch_shapes=r_scratch,
        compiler_params=_cparams(2),
        name="odd_output",
    )(xa, xa, modtab, o_m, a_d, g_sub.reshape(1, -1), w_out.astype(BF16), *r_args)


def kernel(x, c, ctx, c_ctx, mod_w, mod_b, norm_g, even_w_in, conv_dw_w, conv_dw_b, conv_ln_g,
           conv_ln_b, pool_w, pool_scale, even_w_out, odd_w_in, mla_g_cq, mla_w_uq, mla_g_ckv,
           mla_w_ukv, qk_g_mla, qk_g_diff, diff_lambda, diff_sub_g, odd_w_out, router_w, router_b,
           moe_w_gate, moe_w_up, moe_w_down):
    b, n_lat, d = x.shape
    n_ctx = ctx.shape[1]
    assert n_ctx == TS and n_lat % TS == 0 and mod_w.shape[0] == 2

    c_rows = jnp.concatenate([c, c_ctx[None], jnp.zeros((16 - b - 1, d), F32)], axis=0)
    mods = _modulation(c_rows, mod_w, mod_b)
    modtabs = []
    for i in range(2):
        mod_l = mods[i, :b].reshape(b, N_MOD, d)
        mod_c = jnp.broadcast_to(mods[i, b].reshape(1, N_MOD, d), (b, N_MOD, d))
        modtabs.append(jnp.stack([mod_c, mod_l], axis=1))

    ctx_or_lat = lambda si: jnp.minimum(si, 1)
    lat_only = lambda si: 1

    xa, *routed = _even_layer(ctx, x, modtabs[0], norm_g[0, 0], even_w_in[0], conv_dw_w[0],
                              conv_dw_b[0], conv_ln_g[0], conv_ln_b[0], pool_w[0], pool_scale[0],
                              even_w_out[0], (norm_g[0, 1], router_w, router_b))
    xa = _moe_layer(xa, routed, modtabs[0], router_w, router_b, moe_w_gate, moe_w_up, moe_w_down,
                    0, ctx_or_lat)

    qm, km, vm, qd, kd, vd = _odd_project(xa, modtabs[1], norm_g[1, 0], odd_w_in[0], mla_g_cq[0],
                                          mla_w_uq[0], mla_g_ckv[0], mla_w_ukv[0], qk_g_mla[0],
                                          qk_g_diff[0], n_ctx)
    lam_init = 0.8 - 0.6 * math.exp(-0.3 * 1)
    q_tiles = min(ATTN_Q_TILES, n_lat // TQ)
    o_m = _attention(qm, km, vm, BF16, "attn_mla", q_tiles)
    a_d = _diff_attention(qd, kd, vd, diff_lambda[0], lam_init, q_tiles)
    xl, *routed = _odd_output(xa, modtabs[1], o_m, a_d, diff_sub_g[0], odd_w_out[0], lam_init,
                              n_ctx, (norm_g[1, 1], router_w, router_b))
    return _moe_layer(xl, routed, modtabs[1], router_w, router_b, moe_w_gate, moe_w_up,
                      moe_w_down, 1, lat_only)
```
